```python
import math, functools
import jax, jax.numpy as jnp
from jax import lax
import numpy as np

D_MODEL = 2048
BATCH = 8
SEQ = 2048
DEPTH = 1
DEC_BATCH = 32
DEC_SEQ = 8
PAST_LEN = 8192
PAGE_SIZE = 128

ATT_WIDTH = D_MODEL // 2
RG_WIDTH = D_MODEL - ATT_WIDTH
N_ATT_HEADS = 8
V_HEAD = ATT_WIDTH // N_ATT_HEADS
QK_HALF = V_HEAD // 2
K_ROW = 2 * QK_HALF
N_RG_BLOCKS = 8
RG_BLOCK = RG_WIDTH // N_RG_BLOCKS
CONV_WIDTH = 4
RG_C = 8.0
N_EXPERTS = 256
TOP_K = 8
N_GROUPS = 8
TOPK_GROUPS = 4
EXPERT_FF = D_MODEL // 4
SHARED_FF = EXPERT_FF
ROUTED_SCALE = 2.5
Q_BLOCK = 128
NORM_EPS = 1e-6
SUBLN_EPS = 1e-5
N_IN = 3 * ATT_WIDTH + 2 * RG_WIDTH
NEG = -1e30

kernel_name = 'hymba_diffattn_rglru_moe_adaln_step'


def lambda_init(layer):
    return 0.8 - 0.6 * math.exp(-0.3 * layer)


def alibi_slopes():
    return jnp.exp2(-8.0 * jnp.arange(1, N_ATT_HEADS + 1, dtype=jnp.float32) / N_ATT_HEADS)


def rmsnorm(x, g, eps=NORM_EPS):
    xf = x.astype(jnp.float32)
    y = xf * lax.rsqrt(jnp.mean(xf * xf, axis=-1, keepdims=True) + eps)
    return (y * g.astype(jnp.float32)).astype(x.dtype)


def modulate(h, shift, scale):
    return h * (1 + scale[:, None, :]) + shift[:, None, :]


def diff_weights(s, lam):
    p = jax.nn.softmax(s, axis=-1)
    return p[..., 0, :, :] - lam * p[..., 1, :, :]


def attend_prompt(q, k, v, lam):
    B, S = q.shape[0], q.shape[1]
    nb = S // Q_BLOCK
    slopes = alibi_slopes()
    kpos = jnp.arange(S)
    scale = QK_HALF ** -0.5
    qb = q.reshape(B, nb, Q_BLOCK, N_ATT_HEADS, 2, QK_HALF).transpose(1, 0, 2, 3, 4, 5)

    def one_block(args):
        qi, start = args
        qpos = start + jnp.arange(Q_BLOCK)
        s = jnp.einsum('bqhcd,bkhcd->bhcqk', qi, k).astype(jnp.float32) * scale
        dist = (qpos[:, None] - kpos[None, :]).astype(jnp.float32)
        bias = jnp.where(dist[None] >= 0, -slopes[:, None, None] * dist[None], NEG)
        w = diff_weights(s + bias[None, :, None], lam)
        return jnp.einsum('bhqk,bkhd->bqhd', w.astype(v.dtype), v)

    out = lax.map(one_block, (qb, jnp.arange(nb) * Q_BLOCK))
    return out.transpose(1, 0, 2, 3, 4).reshape(B, S, N_ATT_HEADS, V_HEAD)


def attend_sample(q, k, v, lam, cache_k, cache_v, page_table, layer):
    L = q.shape[1]
    past = page_table.shape[1] * PAGE_SIZE
    slopes = alibi_slopes()
    scale = QK_HALF ** -0.5
    qpos = past + jnp.arange(L)
    d_past = (qpos[:, None] - jnp.arange(past)[None, :]).astype(jnp.float32)
    bias_past = -slopes[:, None, None] * d_past[None]
    d_new = (jnp.arange(L)[:, None] - jnp.arange(L)[None, :]).astype(jnp.float32)
    bias_new = jnp.where(d_new[None] >= 0, -slopes[:, None, None] * d_new[None], NEG)

    def one_seq(args):
        qi, ki, vi, pages = args
        kp = cache_k[layer, pages].reshape(past, N_ATT_HEADS, 2, QK_HALF)
        vp = cache_v[layer, pages].reshape(past, N_ATT_HEADS, V_HEAD)
        s_past = jnp.einsum('qhcd,khcd->hcqk', qi, kp).astype(jnp.float32) * scale + bias_past[:, None]
        s_new = jnp.einsum('qhcd,khcd->hcqk', qi, ki).astype(jnp.float32) * scale + bias_new[:, None]
        w = diff_weights(jnp.concatenate([s_past, s_new], axis=-1), lam).astype(vi.dtype)
        return (jnp.einsum('hqk,khd->qhd', w[..., :past], vp)
                + jnp.einsum('hqk,khd->qhd', w[..., past:], vi))

    return lax.map(one_seq, (q, k, v, page_table))


def rglru(u, conv_buf, h0, conv_w, conv_b, w_a, b_a, w_i, b_i, rg_lambda):
    B, L = u.shape[0], u.shape[1]
    u_ext = jnp.concatenate([conv_buf.astype(u.dtype), u], axis=1)
    xc = conv_b + sum(conv_w[j] * u_ext[:, j:j + L] for j in range(CONV_WIDTH))
    xb = xc.reshape(B, L, N_RG_BLOCKS, RG_BLOCK)
    r = jax.nn.sigmoid((jnp.einsum('blnc,ncd->blnd', xb, w_a).reshape(B, L, RG_WIDTH) + b_a).astype(jnp.float32))
    i = jax.nn.sigmoid((jnp.einsum('blnc,ncd->blnd', xb, w_i).reshape(B, L, RG_WIDTH) + b_i).astype(jnp.float32))
    log_a = -RG_C * r * jax.nn.softplus(-rg_lambda.astype(jnp.float32))
    a = jnp.exp(log_a)
    b = xc.astype(jnp.float32) * i * jnp.sqrt(-jnp.expm1(2.0 * log_a))

    def step(h, inp):
        a_t, b_t = inp
        h = a_t * h + b_t
        return h, h

    hT, hs = lax.scan(step, h0.astype(jnp.float32), (a.transpose(1, 0, 2), b.transpose(1, 0, 2)))
    return hs.transpose(1, 0, 2).astype(u.dtype), hT.astype(h0.dtype), u_ext[:, -(CONV_WIDTH - 1):]


def moe(h, w_router, router_bias, experts, layer, w_s_gate, w_s_up, w_s_down):
    w_gate, w_up, w_down = experts
    T = h.shape[0]
    s = jax.nn.sigmoid((h @ w_router).astype(jnp.float32))
    biased = s + router_bias.astype(jnp.float32)
    grp = biased.reshape(T, N_GROUPS, N_EXPERTS // N_GROUPS)
    grp_score = lax.top_k(grp, 2)[0].sum(-1)
    _, top_g = lax.top_k(grp_score, TOPK_GROUPS)
    gmask = jax.nn.one_hot(top_g, N_GROUPS, dtype=jnp.float32).sum(1) > 0
    masked = jnp.where(gmask[:, :, None], grp, -jnp.inf).reshape(T, N_EXPERTS)
    _, idx = lax.top_k(masked, TOP_K)
    wts = jnp.take_along_axis(s, idx, axis=-1)
    wts = wts / jnp.sum(wts, axis=-1, keepdims=True) * ROUTED_SCALE

    A = T * TOP_K
    per = max(A // N_EXPERTS, 1)
    blk = int(min(128, max(8, 2 ** int(math.log2(per)))))
    n_blocks = -(-A // blk) + N_EXPERTS
    n_rows = n_blocks * blk
    flat_e = idx.reshape(A)
    flat_tok = jnp.repeat(jnp.arange(T, dtype=jnp.int32), TOP_K)
    flat_w = wts.reshape(A)
    order = jnp.argsort(flat_e)
    se = flat_e[order]
    counts = jnp.bincount(flat_e, length=N_EXPERTS)
    starts = jnp.cumsum(counts) - counts
    pcounts = ((counts + blk - 1) // blk) * blk
    pends = jnp.cumsum(pcounts)
    pstarts = pends - pcounts
    dest = pstarts[se] + (jnp.arange(A) - starts[se])
    slot_tok = jnp.full((n_rows,), T, jnp.int32).at[dest].set(flat_tok[order])
    slot_w = jnp.zeros((n_rows,), jnp.float32).at[dest].set(flat_w[order])
    block_e = jnp.minimum(jnp.searchsorted(pends, jnp.arange(n_blocks) * blk, side='right'), N_EXPERTS - 1)
    h_pad = jnp.concatenate([h, jnp.zeros((1, h.shape[1]), h.dtype)], axis=0)

    def run_block(args):
        tok, e, wt = args
        xb = h_pad[tok]
        g = xb @ w_gate[layer, e]
        up = xb @ w_up[layer, e]
        return ((jax.nn.silu(g) * up) @ w_down[layer, e]) * wt[:, None].astype(xb.dtype)

    out = lax.map(run_block, (slot_tok.reshape(n_blocks, blk), block_e, slot_w.reshape(n_blocks, blk)))
    routed = jax.ops.segment_sum(out.reshape(n_rows, -1), slot_tok, num_segments=T + 1)[:T]
    shared = (jax.nn.silu(h @ w_s_gate) * (h @ w_s_up)) @ w_s_down
    return routed.astype(h.dtype) + shared


def decoder_layer(x, c, P, experts, layer, attend, conv_buf, h0):
    B, L, _ = x.shape
    lam_init = lambda_init(layer)
    mod = jax.nn.silu(c) @ P['w_ada'] + P['b_ada']
    sh1, sc1, g1, sh2, sc2, g2 = jnp.split(mod, 6, axis=-1)
    h = modulate(rmsnorm(x, P['g_norm1']), sh1, sc1)
    z = h @ P['w_in']
    q, k, v, u, gate = jnp.split(z, [ATT_WIDTH, 2 * ATT_WIDTH, 3 * ATT_WIDTH, 3 * ATT_WIDTH + RG_WIDTH], axis=-1)
    q = q.reshape(B, L, N_ATT_HEADS, 2, QK_HALF)
    k = k.reshape(B, L, N_ATT_HEADS, 2, QK_HALF)
    v = v.reshape(B, L, N_ATT_HEADS, V_HEAD)
    f32 = jnp.float32
    lam = (jnp.exp(jnp.sum(P['lambda_q1'].astype(f32) * P['lambda_k1'].astype(f32)))
           - jnp.exp(jnp.sum(P['lambda_q2'].astype(f32) * P['lambda_k2'].astype(f32))) + lam_init)
    att = attend(q, k, v, lam)
    att = rmsnorm(att, P['g_subln'], SUBLN_EPS) * (1 - lam_init)
    rec, hT, new_buf = rglru(u, conv_buf, h0, P['conv_w'], P['conv_b'], P['w_rg_a'], P['b_rg_a'],
                             P['w_rg_i'], P['b_rg_i'], P['rg_lambda'])
    rec = rmsnorm(rec * jax.nn.gelu(gate, approximate=True), P['g_rgnorm'])
    mix = jnp.concatenate([att.reshape(B, L, ATT_WIDTH), rec], axis=-1) @ P['w_o']
    x = x + g1[:, None, :] * mix
    h2 = modulate(rmsnorm(x, P['g_norm2']), sh2, sc2)
    y = moe(h2.reshape(B * L, D_MODEL), P['w_router'], P['router_bias'], experts, layer,
            P['w_s_gate'], P['w_s_up'], P['w_s_down']).reshape(B, L, D_MODEL)
    x = x + g2[:, None, :] * y
    return x, k.reshape(B, L, N_ATT_HEADS, K_ROW), v, hT, new_buf


def setup_inputs(seed: int = 0) -> dict:
    key = jax.random.key(seed)
    keys = jax.random.split(key, 48)
    counter = [0]
    f32 = jnp.float32

    def nk():
        counter[0] += 1
        return keys[counter[0] - 1]

    def nrm(shape, scale):
        return jax.random.normal(nk(), shape, f32) * scale

    def gain(shape):
        return 1.0 + nrm(shape, 0.01)

    n_pages = PAST_LEN // PAGE_SIZE
    n_used = DEC_BATCH * n_pages
    n_pool = n_used + n_used // 4
    perm = jax.random.permutation(nk(), n_pool)
    page_table = perm[:n_used].reshape(DEC_BATCH, n_pages).astype(jnp.int32)
    ua = jax.random.uniform(nk(), (DEPTH, RG_WIDTH), f32, 0.9, 0.999)
    a_base = ua ** (1.0 / RG_C)
    rg_lambda = jnp.log(a_base) - jnp.log1p(-a_base)
    D = D_MODEL
    return {
        'x_prompt': nrm((BATCH, SEQ, D), 1.0),
        'x_sample': nrm((DEC_BATCH, DEC_SEQ, D), 1.0),
        'cache_k': nrm((DEPTH, n_pool, PAGE_SIZE, N_ATT_HEADS, K_ROW), 1.0),
        'cache_v': nrm((DEPTH, n_pool, PAGE_SIZE, N_ATT_HEADS, V_HEAD), 1.0),
        'state_h': nrm((DEPTH, DEC_BATCH, RG_WIDTH), 0.5),
        'state_conv': nrm((DEPTH, DEC_BATCH, CONV_WIDTH - 1, RG_WIDTH), 1.0),
        'page_table': page_table,
        'c_prompt': nrm((BATCH, D), 1.0),
        'c_sample': nrm((DEC_BATCH, D), 1.0),
        'w_ada': nrm((DEPTH, D, 6 * D), 0.5 * D ** -0.5),
        'b_ada': nrm((DEPTH, 6 * D), 0.02),
        'g_norm1': gain((DEPTH, D)),
        'w_in': nrm((DEPTH, D, N_IN), D ** -0.5),
        'lambda_q1': nrm((DEPTH, QK_HALF), 0.1),
        'lambda_k1': nrm((DEPTH, QK_HALF), 0.1),
        'lambda_q2': nrm((DEPTH, QK_HALF), 0.1),
        'lambda_k2': nrm((DEPTH, QK_HALF), 0.1),
        'g_subln': gain((DEPTH, V_HEAD)),
        'conv_w': nrm((DEPTH, CONV_WIDTH, RG_WIDTH), 0.5),
        'conv_b': nrm((DEPTH, RG_WIDTH), 0.01),
        'w_rg_a': nrm((DEPTH, N_RG_BLOCKS, RG_BLOCK, RG_BLOCK), RG_BLOCK ** -0.5),
        'b_rg_a': nrm((DEPTH, RG_WIDTH), 0.01),
        'w_rg_i': nrm((DEPTH, N_RG_BLOCKS, RG_BLOCK, RG_BLOCK), RG_BLOCK ** -0.5),
        'b_rg_i': nrm((DEPTH, RG_WIDTH), 0.01),
        'rg_lambda': rg_lambda,
        'g_rgnorm': gain((DEPTH, RG_WIDTH)),
        'w_o': nrm((DEPTH, D, D), D ** -0.5),
        'g_norm2': gain((DEPTH, D)),
        'w_router': nrm((DEPTH, D, N_EXPERTS), D ** -0.5),
        'router_bias': nrm((DEPTH, N_EXPERTS), 0.01),
        'w_e_gate': nrm((DEPTH, N_EXPERTS, D, EXPERT_FF), D ** -0.5),
        'w_e_up': nrm((DEPTH, N_EXPERTS, D, EXPERT_FF), D ** -0.5),
        'w_e_down': nrm((DEPTH, N_EXPERTS, EXPERT_FF, D), EXPERT_FF ** -0.5),
        'w_s_gate': nrm((DEPTH, D, SHARED_FF), D ** -0.5),
        'w_s_up': nrm((DEPTH, D, SHARED_FF), D ** -0.5),
        'w_s_down': nrm((DEPTH, SHARED_FF, D), SHARED_FF ** -0.5),
        'g_final': gain((D,)),
    }


def reference(x_prompt, x_sample, cache_k, cache_v, state_h, state_conv, page_table, c_prompt, c_sample,
              w_ada, b_ada, g_norm1, w_in, lambda_q1, lambda_k1, lambda_q2, lambda_k2, g_subln,
              conv_w, conv_b, w_rg_a, b_rg_a, w_rg_i, b_rg_i, rg_lambda, g_rgnorm, w_o, g_norm2,
              w_router, router_bias, w_e_gate, w_e_up, w_e_down, w_s_gate, w_s_up, w_s_down, g_final):
    experts = (w_e_gate, w_e_up, w_e_down)
    xp, xs = x_prompt, x_sample
    kp_l, vp_l, hp_l, cp_l, ks_l, vs_l, hs_l, cs_l = [], [], [], [], [], [], [], []
    for layer in range(DEPTH):
        P = {
            'w_ada': w_ada[layer], 'b_ada': b_ada[layer], 'g_norm1': g_norm1[layer], 'w_in': w_in[layer],
            'lambda_q1': lambda_q1[layer], 'lambda_k1': lambda_k1[layer],
            'lambda_q2': lambda_q2[layer], 'lambda_k2': lambda_k2[layer], 'g_subln': g_subln[layer],
            'conv_w': conv_w[layer], 'conv_b': conv_b[layer], 'w_rg_a': w_rg_a[layer], 'b_rg_a': b_rg_a[layer],
            'w_rg_i': w_rg_i[layer], 'b_rg_i': b_rg_i[layer], 'rg_lambda': rg_lambda[layer],
            'g_rgnorm': g_rgnorm[layer], 'w_o': w_o[layer], 'g_norm2': g_norm2[layer],
            'w_router': w_router[layer], 'router_bias': router_bias[layer],
            'w_s_gate': w_s_gate[layer], 'w_s_up': w_s_up[layer], 'w_s_down': w_s_down[layer],
        }
        conv0 = jnp.zeros((xp.shape[0], CONV_WIDTH - 1, RG_WIDTH), xp.dtype)
        h0 = jnp.zeros((xp.shape[0], RG_WIDTH), xp.dtype)
        xp, kp, vp, hp, cp = decoder_layer(xp, c_prompt, P, experts, layer, attend_prompt, conv0, h0)
        attend_s = functools.partial(attend_sample, cache_k=cache_k, cache_v=cache_v,
                                     page_table=page_table, layer=layer)
        xs, ks, vs, hs, cs = decoder_layer(xs, c_sample, P, experts, layer, attend_s,
                                           state_conv[layer], state_h[layer])
        kp_l.append(kp); vp_l.append(vp); hp_l.append(hp); cp_l.append(cp)
        ks_l.append(ks); vs_l.append(vs); hs_l.append(hs); cs_l.append(cs)
    y_prompt = rmsnorm(xp, g_final)
    y_sample = rmsnorm(xs, g_final)
    return (y_prompt, y_sample, jnp.stack(kp_l), jnp.stack(vp_l), jnp.stack(hp_l), jnp.stack(cp_l),
            jnp.stack(ks_l), jnp.stack(vs_l), jnp.stack(hs_l), jnp.stack(cs_l))
```

```python
import functools
import math

import jax
import jax.numpy as jnp
from jax import lax
from jax.experimental import pallas as pl
from jax.experimental.pallas import tpu as pltpu

F32 = jnp.float32
BF16 = jnp.bfloat16
I32 = jnp.int32
U32 = jnp.uint32

NORM_EPS = 1e-6
SUBLN_EPS = 1e-5
NEG = -1e30
RG_C = 8.0
ROUTED_SCALE = 2.5
N_GROUPS = 8
TOPK_GROUPS = 4
TOP_K = 8
CONV_WIDTH = 4

V7X_VMEM_LIMIT_BYTES = 56 * 1024 * 1024
EXPERT_TILE_ROWS = 128


def _cparams(n_axes):
    return pltpu.CompilerParams(
        dimension_semantics=("arbitrary",) * n_axes, vmem_limit_bytes=V7X_VMEM_LIMIT_BYTES
    )


def _pick(n, candidates):
    for c in candidates:
        if n % c == 0:
            return c
    return n


def _dot(a, b):
    return jnp.dot(a, b, preferred_element_type=F32)


def _dot_nt(a, b):
    return lax.dot_general(a, b, (((1,), (1,)), ((), ())), preferred_element_type=F32)


def _split(x):
    hi = x.astype(BF16)
    lo = (x - hi.astype(F32)).astype(BF16)
    return hi, lo


def _rms(x, eps):
    return x * lax.rsqrt(jnp.mean(x * x, axis=-1, keepdims=True) + eps)


def _silu(x):
    return x * jax.nn.sigmoid(x)


def _unpack_halves(xp):
    lo = pltpu.unpack_elementwise(xp, index=0, packed_dtype=BF16, unpacked_dtype=F32)
    hi = pltpu.unpack_elementwise(xp, index=1, packed_dtype=BF16, unpacked_dtype=F32)
    return lo.astype(BF16), hi.astype(BF16)


def _pack_halves(x):
    n = x.shape[-1] // 2
    return pltpu.pack_elementwise([x[:, :n], x[:, n:]], packed_dtype=BF16)


def _ada_kernel(c_ref, w_ref, b_ref, o_ref):
    a_hi, a_lo = _split(_silu(c_ref[...]))
    w_hi, w_lo = _split(w_ref[...])
    o_ref[...] = _dot(a_hi, w_hi) + (_dot(a_hi, w_lo) + _dot(a_lo, w_hi)) + b_ref[...]


def _ada(c, w, b):
    n, d = c.shape
    d_out = w.shape[1]
    tn = _pick(d_out, (512, 256, 128))
    return pl.pallas_call(
        _ada_kernel,
        grid=(d_out // tn,),
        in_specs=[
            pl.BlockSpec((n, d), lambda j: (0, 0)),
            pl.BlockSpec((d, tn), lambda j: (0, j)),
            pl.BlockSpec((1, tn), lambda j: (0, j)),
        ],
        out_specs=pl.BlockSpec((n, tn), lambda j: (0, j)),
        out_shape=jax.ShapeDtypeStruct((n, d_out), F32),
        compiler_params=_cparams(1),
        name="ada",
    )(c, w, b.reshape(1, d_out))


def _mod_spec(rows_per_batch, tm, d, n_tiles=None):
    clamp = (lambda i: i) if n_tiles is None else (lambda i: jnp.minimum(i, n_tiles - 1))
    if rows_per_batch % tm == 0:
        per = rows_per_batch // tm
        return pl.BlockSpec((None, 1, d), lambda i, *_: (clamp(i) // per, 0, 0))
    return pl.BlockSpec((None, tm, d), lambda i, *_: (clamp(i), 0, 0))


def _mod_array(m, rows_per_batch, tm):
    nb, d = m.shape
    if rows_per_batch % tm == 0:
        return m.reshape(nb, 1, d)
    assert tm % rows_per_batch == 0
    return jnp.repeat(m, rows_per_batch, axis=0).reshape(nb * rows_per_batch // tm, tm, d)


def _inproj_kernel(x_ref, sh_ref, sc_ref, g_ref, w_ref,
                   q_ref, k_ref, v_ref, u_ref, gt_ref, kb_ref, vb_ref, h_scr, *, qscale):
    j = pl.program_id(1)

    @pl.when(j == 0)
    def _():
        y = _rms(x_ref[...], NORM_EPS) * g_ref[...]
        h_scr[...] = (y * (1.0 + sc_ref[...]) + sh_ref[...]).astype(BF16)

    z = _dot(h_scr[...], w_ref[...])

    @pl.when(j == 0)
    def _():
        q_ref[...] = (z * qscale).astype(BF16)

    @pl.when(j == 1)
    def _():
        k_ref[...] = z
        kb_ref[...] = z.astype(BF16)

    @pl.when(j == 2)
    def _():
        v_ref[...] = z
        vb_ref[...] = z.astype(BF16)

    @pl.when(j == 3)
    def _():
        u_ref[...] = z

    @pl.when(j == 4)
    def _():
        gt_ref[...] = z


def _inproj(x2d, shift, scale, g, w_bf, rows_per_batch, qscale):
    r, d = x2d.shape
    wd = w_bf.shape[1] // 5
    tm = _pick(r, (512, 256, 128, 64, 32, 16, 8))
    row = lambda i, j: (i, 0)
    f32o = jax.ShapeDtypeStruct((r, wd), F32)
    bfo = jax.ShapeDtypeStruct((r, wd), BF16)
    return pl.pallas_call(
        functools.partial(_inproj_kernel, qscale=qscale),
        grid=(r // tm, 5),
        in_specs=[
            pl.BlockSpec((tm, d), row),
            _mod_spec(rows_per_batch, tm, d),
            _mod_spec(rows_per_batch, tm, d),
            pl.BlockSpec((1, d), lambda i, j: (0, 0)),
            pl.BlockSpec((d, wd), lambda i, j: (0, j)),
        ],
        out_specs=[pl.BlockSpec((tm, wd), row)] * 7,
        out_shape=[bfo, f32o, f32o, f32o, f32o, bfo, bfo],
        scratch_shapes=[pltpu.VMEM((tm, d), BF16)],
        compiler_params=_cparams(2),
        name="inproj",
    )(x2d, _mod_array(shift, rows_per_batch, tm), _mod_array(scale, rows_per_batch, tm), g.reshape(1, d), w_bf)


def _lam(lamv_ref, lam_init):
    lv = lamv_ref[...]
    s1 = jnp.sum(lv[0:1] * lv[1:2], axis=-1, keepdims=True)
    s2 = jnp.sum(lv[2:3] * lv[3:4], axis=-1, keepdims=True)
    return jnp.exp(s1) - jnp.exp(s2) + lam_init


def _online_update(s, v, m_scr, l_scr, acc_scr):
    m_old = m_scr[...]
    m_new = jnp.maximum(m_old, jnp.max(s, axis=-1, keepdims=True))
    alpha = jnp.exp(m_old - m_new)
    p = jnp.exp(s - m_new)
    l_scr[...] = alpha * l_scr[...] + jnp.sum(p, axis=-1, keepdims=True)
    acc_scr[...] = alpha * acc_scr[...] + _dot(p.astype(BF16), v)
    m_scr[...] = m_new


def _pattn_kernel(slopes_ref, lamv_ref, gs_ref, q_ref, k_ref, v_ref, o_ref,
                  m_scr, l_scr, acc_scr, *, tq, lam_init):
    h = pl.program_id(1)
    qi = pl.program_id(2)
    slope = slopes_ref[h]
    half = q_ref.shape[1] // 2

    q = q_ref[...]
    lane = lax.broadcasted_iota(I32, q.shape, 1)
    zero = jnp.zeros_like(q)
    q12 = jnp.concatenate([jnp.where(lane < half, q, zero), jnp.where(lane >= half, q, zero)], axis=0)
    row = lax.broadcasted_iota(I32, (2 * tq, tq), 0)
    row = jnp.where(row >= tq, row - tq, row)
    col = lax.broadcasted_iota(I32, (2 * tq, tq), 1)
    rel = (col - row).astype(F32) * slope

    m_scr[...] = jnp.full(m_scr.shape, NEG, F32)
    l_scr[...] = jnp.zeros(l_scr.shape, F32)
    acc_scr[...] = jnp.zeros(acc_scr.shape, F32)

    def step(j, masked):
        start = pl.multiple_of(j * tq, tq)
        k = k_ref[pl.ds(start, tq), :]
        v = v_ref[pl.ds(start, tq), :]
        s = _dot_nt(q12, k) + (rel + slope * ((j - qi) * tq).astype(F32))
        if masked:
            s = jnp.where(col <= row, s, NEG)
        _online_update(s, v, m_scr, l_scr, acc_scr)

    def body(j, carry):
        step(j, False)
        return carry

    lax.fori_loop(0, qi, body, 0)
    step(qi, True)

    lam = _lam(lamv_ref, lam_init)
    o = acc_scr[...] / l_scr[...]
    att = o[:tq] - lam * o[tq:]
    att = _rms(att, SUBLN_EPS) * gs_ref[...] * (1.0 - lam_init)
    o_ref[...] = att.astype(BF16)


def _pattn(qb, kb, vb, slopes, lamv, g_subln, n_batch, seq, n_heads, lam_init):
    r, aw = qb.shape
    hd = aw // n_heads
    tq = _pick(seq, (512, 256, 128))
    nq = seq // tq
    return pl.pallas_call(
        functools.partial(_pattn_kernel, tq=tq, lam_init=lam_init),
        grid_spec=pltpu.PrefetchScalarGridSpec(
            num_scalar_prefetch=1,
            grid=(n_batch, n_heads, nq),
            in_specs=[
                pl.BlockSpec(lamv.shape, lambda b, h, i, *_: (0, 0)),
                pl.BlockSpec((1, hd), lambda b, h, i, *_: (0, 0)),
                pl.BlockSpec((tq, hd), lambda b, h, i, *_: (b * nq + i, h)),
                pl.BlockSpec((seq, hd), lambda b, h, i, *_: (b, h)),
                pl.BlockSpec((seq, hd), lambda b, h, i, *_: (b, h)),
            ],
            out_specs=pl.BlockSpec((tq, hd), lambda b, h, i, *_: (b * nq + i, h)),
            scratch_shapes=[
                pltpu.VMEM((2 * tq, 1), F32),
                pltpu.VMEM((2 * tq, 1), F32),
                pltpu.VMEM((2 * tq, hd), F32),
            ],
        ),
        out_shape=jax.ShapeDtypeStruct((r, aw), BF16),
        compiler_params=_cparams(3),
        name="pattn",
    )(slopes, lamv, g_subln.reshape(1, hd), qb, kb, vb)


def _sattn_kernel(pt_ref, lamv_ref, gs_ref, wq_ref, kn_ref, vn_ref, *rest,
                  n_pages_step, page, past, n_new, n_heads, lam_init):
    k_refs = rest[:n_pages_step]
    v_refs = rest[n_pages_step:2 * n_pages_step]
    o_ref = rest[2 * n_pages_step]
    m_scr, l_scr, acc_scr = rest[2 * n_pages_step + 1:]
    j = pl.program_id(1)
    n_rows = wq_ref.shape[0]
    hd = wq_ref.shape[1] // n_heads
    tk = n_pages_step * page

    r = lax.broadcasted_iota(I32, (n_rows, 1), 0)
    head = r // (2 * n_new)
    qi = r % n_new
    slope = jnp.exp2(-(head + 1).astype(F32))

    @pl.when(j == 0)
    def _():
        m_scr[...] = jnp.full(m_scr.shape, NEG, F32)
        l_scr[...] = jnp.zeros(l_scr.shape, F32)
        acc_scr[...] = jnp.zeros(acc_scr.shape, F32)

    wq = wq_ref[...]
    kc = jnp.concatenate([kr[...].astype(BF16) for kr in k_refs], axis=0)
    vc = jnp.concatenate([vr[...].astype(BF16) for vr in v_refs], axis=0)
    t = j * tk + lax.broadcasted_iota(I32, (1, tk), 1)
    s = _dot_nt(wq, kc) - slope * (past + qi - t).astype(F32)
    _online_update(s, vc, m_scr, l_scr, acc_scr)

    @pl.when(j == pl.num_programs(1) - 1)
    def _():
        pad = jnp.zeros((page - n_new, kn_ref.shape[1]), BF16)
        kn = jnp.concatenate([kn_ref[...].astype(BF16), pad], axis=0)
        vn = jnp.concatenate([vn_ref[...].astype(BF16), pad], axis=0)
        tj = lax.broadcasted_iota(I32, (1, page), 1)
        sn = _dot_nt(wq, kn) - slope * (qi - tj).astype(F32)
        sn = jnp.where(tj <= qi, sn, NEG)
        _online_update(sn, vn, m_scr, l_scr, acc_scr)

        lam = _lam(lamv_ref, lam_init)
        o = acc_scr[...] / l_scr[...]
        outs = []
        for h in range(n_heads):
            blk = o[h * 2 * n_new:(h + 1) * 2 * n_new, h * hd:(h + 1) * hd]
            att = blk[:n_new] - lam * blk[n_new:]
            outs.append(_rms(att, SUBLN_EPS) * gs_ref[...] * (1.0 - lam_init))
        o_ref[...] = jnp.concatenate(outs, axis=-1).astype(BF16)


def _sattn(wq, k_new, v_new, cache_k2, cache_v2, page_table, lamv, g_subln, n_heads, lam_init):
    n_seq, n_rows, aw = wq.shape
    n_new = k_new.shape[1]
    hd = aw // n_heads
    page = cache_k2.shape[1]
    n_pages = page_table.shape[1]
    pstep = _pick(n_pages, (8, 4, 2, 1))
    past = n_pages * page

    def page_spec(p):
        return pl.BlockSpec((None, page, aw), lambda b, j, pt: (pt[b, j * pstep + p], 0, 0))

    seq_spec = lambda rows: pl.BlockSpec((None, rows, aw), lambda b, j, pt: (b, 0, 0))
    return pl.pallas_call(
        functools.partial(_sattn_kernel, n_pages_step=pstep, page=page, past=past, n_new=n_new,
                          n_heads=n_heads, lam_init=lam_init),
        grid_spec=pltpu.PrefetchScalarGridSpec(
            num_scalar_prefetch=1,
            grid=(n_seq, n_pages // pstep),
            in_specs=[
                pl.BlockSpec(lamv.shape, lambda b, j, pt: (0, 0)),
                pl.BlockSpec((1, hd), lambda b, j, pt: (0, 0)),
                seq_spec(n_rows), seq_spec(n_new), seq_spec(n_new),
            ] + [page_spec(p) for p in range(pstep)] * 2,
            out_specs=seq_spec(n_new),
            scratch_shapes=[
                pltpu.VMEM((n_rows, 1), F32),
                pltpu.VMEM((n_rows, 1), F32),
                pltpu.VMEM((n_rows, aw), F32),
            ],
        ),
        out_shape=jax.ShapeDtypeStruct((n_seq, n_new, aw), BF16),
        compiler_params=_cparams(2),
        name="sattn",
    )(page_table, lamv, g_subln.reshape(1, hd), wq, k_new, v_new,
      *([cache_k2] * pstep), *([cache_v2] * pstep))


def _gelu_tanh(x):
    return x * (0.5 * (1.0 + jnp.tanh(math.sqrt(2.0 / math.pi) * (x + 0.044715 * (x * x * x)))))


def _softplus(x):
    return jnp.maximum(x, 0.0) + jnp.log1p(jnp.exp(-jnp.abs(x)))


def _rglru_kernel(u_ref, gt_ref, c0_ref, h0_ref, cw_ref, cb_ref, wai_ref, ba_ref, bi_ref, lam_ref, gn_ref,
                  rec_ref, ht_ref, cout_ref, ubuf, hcar, *, tl, n_blocks):
    t = pl.program_id(1)
    halo = CONV_WIDTH - 1
    base = 8

    @pl.when(t == 0)
    def _():
        ubuf[base - halo:base, :] = c0_ref[...]
        hcar[...] = h0_ref[...]

    ubuf[base:base + tl, :] = u_ref[...]
    cw = cw_ref[...]
    xc = cb_ref[...] + cw[0:1] * ubuf[base - halo:base - halo + tl, :]
    for jj in range(1, CONV_WIDTH):
        xc = xc + cw[jj:jj + 1] * ubuf[base - halo + jj:base - halo + jj + tl, :]
    tail = ubuf[base + tl - halo:base + tl, :]
    ubuf[base - halo:base, :] = tail
    cout_ref[...] = tail

    bw = xc.shape[1] // n_blocks
    za, zi = [], []
    for n in range(n_blocks):
        z = _dot(xc[:, n * bw:(n + 1) * bw].astype(BF16), wai_ref[n])
        za.append(z[:, :bw])
        zi.append(z[:, bw:])
    r = jax.nn.sigmoid(jnp.concatenate(za, axis=-1) + ba_ref[...])
    i = jax.nn.sigmoid(jnp.concatenate(zi, axis=-1) + bi_ref[...])
    log_a = -RG_C * r * _softplus(-lam_ref[...])
    a = jnp.exp(log_a)
    th = jnp.tanh(log_a)
    b = xc * i * jnp.sqrt(-2.0 * th / (1.0 - th))

    rowi = lax.broadcasted_iota(I32, a.shape, 0)
    sft = 1
    while sft < tl:
        keep = rowi >= sft
        a_prev = jnp.where(keep, pltpu.roll(a, sft, 0), 1.0)
        b_prev = jnp.where(keep, pltpu.roll(b, sft, 0), 0.0)
        b = a * b_prev + b
        a = a * a_prev
        sft *= 2
    hs = a * hcar[...] + b
    h_last = hs[tl - 1:tl, :]
    hcar[...] = h_last
    ht_ref[...] = h_last

    rec = hs * _gelu_tanh(gt_ref[...])
    rec_ref[...] = (_rms(rec, NORM_EPS) * gn_ref[...]).astype(BF16)


def _rglru(u2d, gate2d, conv0, h0, conv_w, conv_b, wai_bf, b_a, b_i, rg_lambda, g_rgnorm, n_batch, seq):
    r, w = u2d.shape
    tl = _pick(seq, (256, 128, 64, 32, 16, 8))
    nt = seq // tl
    n_blocks = wai_bf.shape[0]
    halo = CONV_WIDTH - 1
    row = lambda b, t: (b * nt + t, 0)
    vec = pl.BlockSpec((1, w), lambda b, t: (0, 0))
    rec, ht, cout = pl.pallas_call(
        functools.partial(_rglru_kernel, tl=tl, n_blocks=n_blocks),
        grid=(n_batch, nt),
        in_specs=[
            pl.BlockSpec((tl, w), row),
            pl.BlockSpec((tl, w), row),
            pl.BlockSpec((None, halo, w), lambda b, t: (b, 0, 0)),
            pl.BlockSpec((None, 1, w), lambda b, t: (b, 0, 0)),
            pl.BlockSpec((CONV_WIDTH, w), lambda b, t: (0, 0)),
            vec,
            pl.BlockSpec(wai_bf.shape, lambda b, t: (0, 0, 0)),
            vec, vec, vec, vec,
        ],
        out_specs=[
            pl.BlockSpec((tl, w), row),
            pl.BlockSpec((None, 1, w), lambda b, t: (b, 0, 0)),
            pl.BlockSpec((None, halo, w), lambda b, t: (b, 0, 0)),
        ],
        out_shape=[
            jax.ShapeDtypeStruct((r, w), BF16),
            jax.ShapeDtypeStruct((n_batch, 1, w), F32),
            jax.ShapeDtypeStruct((n_batch, halo, w), F32),
        ],
        scratch_shapes=[pltpu.VMEM((tl + 8, w), F32), pltpu.VMEM((1, w), F32)],
        compiler_params=_cparams(2),
        name="rglru",
    )(u2d, gate2d, conv0, h0.reshape(n_batch, 1, w), conv_w, conv_b.reshape(1, w), wai_bf,
      b_a.reshape(1, w), b_i.reshape(1, w), rg_lambda.reshape(1, w), g_rgnorm.reshape(1, w))
    return rec, ht.reshape(n_batch, w), cout


def _oproj_kernel(att_ref, rec_ref, x_ref, g1_ref, sh_ref, sc_ref, gn_ref, wo_ref, wrt_ref, *rest,
                  aliased, n_tiles):
    x1_ref, h2p_ref, st_ref = rest[2:] if aliased else rest
    aw = att_ref.shape[1]
    i = pl.program_id(0)

    @pl.when(i < n_tiles)
    def _():
        mix = _dot(att_ref[...], wo_ref[:aw, :]) + _dot(rec_ref[...], wo_ref[aw:, :])
        x1 = x_ref[...] + g1_ref[...] * mix
        x1_ref[...] = x1
        h2 = (_rms(x1, NORM_EPS) * gn_ref[...]) * (1.0 + sc_ref[...]) + sh_ref[...]
        h2p_ref[...] = _pack_halves(h2)
        h_hi, h_lo = _split(h2)
        w_hi, w_lo = _split(wrt_ref[...])
        logits_t = _dot_nt(w_hi, h_hi) + (_dot_nt(w_hi, h_lo) + _dot_nt(w_lo, h_hi))
        st_ref[...] = jax.nn.sigmoid(logits_t)

    @pl.when(i >= n_tiles)
    def _():
        h2p_ref[...] = jnp.zeros(h2p_ref.shape, U32)
        st_ref[...] = jnp.zeros(st_ref.shape, F32)


def _oproj(att, rec, x2d, g1, shift, scale, g_norm2, wo_bf, wr_t, rows_per_batch, row_offset, total_rows,
           h2p_all=None, st_all=None):
    r, d = x2d.shape
    aw = att.shape[1]
    n_exp = wr_t.shape[0]
    tm = _pick(r, (256, 128))
    assert row_offset % tm == 0
    off = row_offset // tm
    aliased = h2p_all is not None
    n_tiles = r // tm
    n_fill = 0 if aliased else (total_rows - r) // tm
    assert aliased or (row_offset == 0 and (total_rows - r) % tm == 0)
    row = lambda i: (jnp.minimum(i, n_tiles - 1), 0)
    mspec = _mod_spec(rows_per_batch, tm, d, n_tiles)
    in_specs = [
        pl.BlockSpec((tm, aw), row),
        pl.BlockSpec((tm, d - aw), row),
        pl.BlockSpec((tm, d), row),
        mspec, mspec, mspec,
        pl.BlockSpec((1, d), lambda i: (0, 0)),
        pl.BlockSpec((d, d), lambda i: (0, 0)),
        pl.BlockSpec((n_exp, d), lambda i: (0, 0)),
    ]
    args = [att, rec, x2d, _mod_array(g1, rows_per_batch, tm), _mod_array(shift, rows_per_batch, tm),
            _mod_array(scale, rows_per_batch, tm), g_norm2.reshape(1, d), wo_bf, wr_t]
    io_alias = {}
    if aliased:
        in_specs += [pl.BlockSpec(memory_space=pl.ANY), pl.BlockSpec(memory_space=pl.ANY)]
        io_alias = {len(args): 1, len(args) + 1: 2}
        args += [h2p_all, st_all]
    return pl.pallas_call(
        functools.partial(_oproj_kernel, aliased=aliased, n_tiles=n_tiles),
        grid=(n_tiles + n_fill,),
        in_specs=in_specs,
        out_specs=[
            pl.BlockSpec((tm, d), row),
            pl.BlockSpec((tm, d // 2), lambda i: (i + off, 0)),
            pl.BlockSpec((n_exp, tm), lambda i: (0, i + off)),
        ],
        out_shape=[
            jax.ShapeDtypeStruct((r, d), F32),
            jax.ShapeDtypeStruct((total_rows, d // 2), U32),
            jax.ShapeDtypeStruct((n_exp, total_rows), F32),
        ],
        input_output_aliases=io_alias,
        compiler_params=_cparams(1),
        name="oproj",
    )(*args)


def _route_kernel(st_ref, rb_ref, idx_ref, wts_ref, rank_ref, cnt_ref, carry):
    i = pl.program_id(0)
    n_exp, tr = st_ref.shape
    gsz = n_exp // N_GROUPS

    @pl.when(i == 0)
    def _():
        carry[...] = jnp.zeros(carry.shape, F32)

    s = st_ref[...]
    biased = s + rb_ref[...]
    g = biased.reshape(N_GROUPS, gsz, tr)
    within = lax.broadcasted_iota(I32, g.shape, 1)
    m1 = jnp.max(g, axis=1, keepdims=True)
    first = jnp.min(jnp.where(g == m1, within, gsz), axis=1, keepdims=True)
    m2 = jnp.max(jnp.where(within == first, -jnp.inf, g), axis=1, keepdims=True)
    gscore = (m1 + m2).reshape(N_GROUPS, tr)

    gidx = lax.broadcasted_iota(I32, gscore.shape, 0)
    gsel = jnp.zeros(gscore.shape, F32)
    for _ in range(TOPK_GROUPS):
        mg = jnp.max(gscore, axis=0, keepdims=True)
        fg = jnp.min(jnp.where(gscore == mg, gidx, N_GROUPS), axis=0, keepdims=True)
        hit = gidx == fg
        gsel = jnp.where(hit, 1.0, gsel)
        gscore = jnp.where(hit, -jnp.inf, gscore)
    masked = jnp.where(gsel.reshape(N_GROUPS, 1, tr) > 0.5, g, -jnp.inf).reshape(n_exp, tr)

    eidx = lax.broadcasted_iota(I32, (n_exp, tr), 0)
    idxs, ws = [], []
    chosen = jnp.zeros((n_exp, tr), jnp.bool_)
    for _ in range(TOP_K):
        mv = jnp.max(masked, axis=0, keepdims=True)
        fe = jnp.min(jnp.where(masked == mv, eidx, n_exp), axis=0, keepdims=True)
        hit = eidx == fe
        idxs.append(fe)
        ws.append(jnp.sum(jnp.where(hit, s, 0.0), axis=0, keepdims=True))
        chosen = chosen | hit
        masked = jnp.where(hit, -jnp.inf, masked)
    idx = jnp.concatenate(idxs, axis=0)
    w = jnp.concatenate(ws, axis=0)
    idx_ref[...] = idx
    wts_ref[...] = w / jnp.sum(w, axis=0, keepdims=True) * ROUTED_SCALE

    cmat = jnp.where(chosen, 1.0, 0.0)
    before = lax.broadcasted_iota(I32, (tr, tr), 0) < lax.broadcasted_iota(I32, (tr, tr), 1)
    prior = _dot(cmat.astype(BF16), jnp.where(before, 1.0, 0.0).astype(BF16)) + carry[...]
    ranks = [jnp.sum(jnp.where(eidx == idxs[k], prior, 0.0), axis=0, keepdims=True) for k in range(TOP_K)]
    rank_ref[...] = jnp.concatenate(ranks, axis=0).astype(I32)
    carry[...] = carry[...] + jnp.sum(cmat, axis=1, keepdims=True)
    cnt_ref[...] = carry[...].astype(I32)


def _route(st_all, router_bias):
    n_exp, t_all = st_all.shape
    tr = _pick(t_all, (640, 512, 256, 128))
    col = lambda i: (0, i)
    o8 = lambda dt: jax.ShapeDtypeStruct((TOP_K, t_all), dt)
    return pl.pallas_call(
        _route_kernel,
        grid=(t_all // tr,),
        in_specs=[pl.BlockSpec((n_exp, tr), col), pl.BlockSpec((n_exp, 1), lambda i: (0, 0))],
        out_specs=[pl.BlockSpec((TOP_K, tr), col)] * 3 + [pl.BlockSpec((n_exp, 1), lambda i: (0, 0))],
        out_shape=[o8(I32), o8(F32), o8(I32), jax.ShapeDtypeStruct((n_exp, 1), I32)],
        scratch_shapes=[pltpu.VMEM((n_exp, 1), F32)],
        compiler_params=_cparams(1),
        name="route",
    )(st_all, router_bias.reshape(n_exp, 1))


def _row_copy(src, src_row, dst, dst_row, sem):
    return pltpu.make_async_copy(src.at[pl.ds(src_row, 1)], dst.at[pl.ds(dst_row, 1)], sem)


def _dispatch_kernel(tnv_ref, pos_ref, x_ref, xs_ref, zeros, sem, *, td, n_tok_tiles, tm):
    i = pl.program_id(0)

    @pl.when(i == 0)
    def _():
        zeros[...] = jnp.zeros(zeros.shape, U32)

    @pl.when(i < n_tok_tiles)
    def _():
        def issue(t, carry):
            for k in range(TOP_K):
                _row_copy(x_ref, t, xs_ref, pos_ref[0, k * td + t], sem).start()
            return carry

        lax.fori_loop(0, td, issue, 0)

        def drain(n, carry):
            _row_copy(x_ref, 0, xs_ref, 0, sem).wait()
            return carry

        lax.fori_loop(0, td * TOP_K, drain, 0)

    @pl.when(i >= n_tok_tiles)
    def _():
        tile = i - n_tok_tiles
        nv = tnv_ref[tile]
        base = pl.multiple_of(tile * tm, tm)

        @pl.when(nv == 0)
        def _():
            cp = pltpu.make_async_copy(zeros, xs_ref.at[pl.ds(base, tm)], sem)
            cp.start()
            cp.wait()

        @pl.when(jnp.logical_and(nv > 0, nv < tm))
        def _():
            def fill(r, carry):
                _row_copy(zeros, 0, xs_ref, base + r, sem).start()
                return carry

            lax.fori_loop(nv, tm, fill, 0)

            def drain(r, carry):
                _row_copy(zeros, 0, xs_ref, 0, sem).wait()
                return carry

            lax.fori_loop(nv, tm, drain, 0)


def _tile_pos(pos, tile):
    k, t = pos.shape
    return pos.reshape(k, t // tile, tile).transpose(1, 0, 2).reshape(t // tile, 1, k * tile)


def _dispatch(h2p_all, pos, tile_nv):
    t_all, dp = h2p_all.shape
    tm = EXPERT_TILE_ROWS
    n_tiles = tile_nv.shape[0]
    td = _pick(t_all, (256, 128, 64, 32, 16, 8))
    n_tok_tiles = t_all // td
    tok = lambda i: jnp.minimum(i, n_tok_tiles - 1)
    return pl.pallas_call(
        functools.partial(_dispatch_kernel, td=td, n_tok_tiles=n_tok_tiles, tm=tm),
        grid_spec=pltpu.PrefetchScalarGridSpec(
            num_scalar_prefetch=1,
            grid=(n_tok_tiles + n_tiles,),
            in_specs=[
                pl.BlockSpec((None, 1, TOP_K * td), lambda i, tnv: (tok(i), 0, 0), memory_space=pltpu.SMEM),
                pl.BlockSpec((td, dp), lambda i, tnv: (tok(i), 0)),
            ],
            out_specs=pl.BlockSpec(memory_space=pl.ANY),
            scratch_shapes=[pltpu.VMEM((tm, dp), U32), pltpu.SemaphoreType.DMA(())],
        ),
        out_shape=jax.ShapeDtypeStruct((n_tiles * tm, dp), U32),
        compiler_params=_cparams(1),
        name="dispatch",
    )(tile_nv, _tile_pos(pos, td), h2p_all)


def _swiglu_packed(xp, wg, wu, wd):
    xa, xb = _unpack_halves(xp)
    half = xp.shape[1]
    g = _dot(xa, wg[:half, :]) + _dot(xb, wg[half:, :])
    u = _dot(xa, wu[:half, :]) + _dot(xb, wu[half:, :])
    return _dot((_silu(g) * u).astype(BF16), wd[...])


def _experts_kernel(te_ref, tnv_ref, xs_ref, wg_ref, wu_ref, wd_ref, ys_ref, wg_b, wu_b, wd_b):
    i = pl.program_id(0)
    nv = tnv_ref[i]
    new_expert = jnp.logical_or(i == 0, te_ref[i] != te_ref[jnp.maximum(i - 1, 0)])

    @pl.when(jnp.logical_and(nv > 0, new_expert))
    def _():
        wg_b[...] = wg_ref[...].astype(BF16)
        wu_b[...] = wu_ref[...].astype(BF16)
        wd_b[...] = wd_ref[...].astype(BF16)

    @pl.when(nv > 0)
    def _():
        rowi = lax.broadcasted_iota(I32, xs_ref.shape, 0)
        xp = jnp.where(rowi < nv, xs_ref[...], jnp.uint32(0))
        ys_ref[...] = _swiglu_packed(xp, wg_b, wu_b, wd_b)

    @pl.when(nv == 0)
    def _():
        ys_ref[...] = jnp.zeros(ys_ref.shape, F32)


def _experts(xs, tile_e, tile_nv, w_gate, w_up, w_down):
    n_rows, dp = xs.shape
    tm = EXPERT_TILE_ROWS
    n_exp, d, ff = w_gate.shape
    wspec = lambda shp: pl.BlockSpec((None,) + shp, lambda i, te, tnv: (te[i], 0, 0))
    return pl.pallas_call(
        _experts_kernel,
        grid_spec=pltpu.PrefetchScalarGridSpec(
            num_scalar_prefetch=2,
            grid=(n_rows // tm,),
            in_specs=[
                pl.BlockSpec((tm, dp), lambda i, te, tnv: (i, 0)),
                wspec((d, ff)), wspec((d, ff)), wspec((ff, d)),
            ],
            out_specs=pl.BlockSpec((tm, d), lambda i, te, tnv: (i, 0)),
            scratch_shapes=[pltpu.VMEM((d, ff), BF16), pltpu.VMEM((d, ff), BF16), pltpu.VMEM((ff, d), BF16)],
        ),
        out_shape=jax.ShapeDtypeStruct((n_rows, d), F32),
        compiler_params=_cparams(1),
        name="experts",
    )(tile_e, tile_nv, xs, w_gate, w_up, w_down)


def _combine_kernel(pos_ref, w_ref, h2p_ref, x1_ref, g2_ref, gf_ref, wsg_ref, wsu_ref, wsd_ref, ys_ref,
                    o_ref, buf, sem, *, tc):
    def issue(t, carry):
        for k in range(TOP_K):
            _row_copy(ys_ref, pos_ref[0, k * tc + t], buf.at[k], t, sem).start()
        return carry

    lax.fori_loop(0, tc, issue, 0)
    shared = _swiglu_packed(h2p_ref[...], wsg_ref, wsu_ref, wsd_ref)

    def drain(n, carry):
        _row_copy(ys_ref, 0, buf.at[0], 0, sem).wait()
        return carry

    lax.fori_loop(0, tc * TOP_K, drain, 0)
    w = w_ref[...]
    routed = buf[0] * w[:, 0:1]
    for k in range(1, TOP_K):
        routed = routed + buf[k] * w[:, k:k + 1]
    x2 = x1_ref[...] + g2_ref[...] * (routed + shared)
    o_ref[...] = _rms(x2, NORM_EPS) * gf_ref[...]


def _combine(ys, pos, wts_t, h2p_all, x1, g2, g_final, wsg_bf, wsu_bf, wsd_bf, rows_per_batch, row_offset):
    r, d = x1.shape
    tc = _pick(r, (128, 64, 32, 16, 8))
    assert row_offset % tc == 0
    off = row_offset // tc
    ff = wsg_bf.shape[1]
    const = lambda shp: pl.BlockSpec(shp, lambda i: (0,) * len(shp))
    return pl.pallas_call(
        functools.partial(_combine_kernel, tc=tc),
        grid=(r // tc,),
        in_specs=[
            pl.BlockSpec((None, 1, TOP_K * tc), lambda i: (i, 0, 0), memory_space=pltpu.SMEM),
            pl.BlockSpec((tc, TOP_K), lambda i: (i, 0)),
            pl.BlockSpec((tc, d // 2), lambda i: (i + off, 0)),
            pl.BlockSpec((tc, d), lambda i: (i, 0)),
            _mod_spec(rows_per_batch, tc, d),
            const((1, d)), const((d, ff)), const((d, ff)), const((ff, d)),
            pl.BlockSpec(memory_space=pl.ANY),
        ],
        out_specs=pl.BlockSpec((tc, d), lambda i: (i, 0)),
        out_shape=jax.ShapeDtypeStruct((r, d), F32),
        scratch_shapes=[pltpu.VMEM((TOP_K, tc, d), F32), pltpu.SemaphoreType.DMA(())],
        compiler_params=_cparams(1),
        name="combine",
    )(_tile_pos(pos, tc), wts_t, h2p_all, x1, _mod_array(g2, rows_per_batch, tc), g_final.reshape(1, d),
      wsg_bf, wsu_bf, wsd_bf, ys)


def kernel(x_prompt, x_sample, cache_k, cache_v, state_h, state_conv, page_table, c_prompt, c_sample,
           w_ada, b_ada, g_norm1, w_in, lambda_q1, lambda_k1, lambda_q2, lambda_k2, g_subln,
           conv_w, conv_b, w_rg_a, b_rg_a, w_rg_i, b_rg_i, rg_lambda, g_rgnorm, w_o, g_norm2,
           w_router, router_bias, w_e_gate, w_e_up, w_e_down, w_s_gate, w_s_up, w_s_down, g_final):
    depth = w_ada.shape[0]
    assert depth == 1, "single-layer step"
    bp, seq, d = x_prompt.shape
    bs, n_new, _ = x_sample.shape
    n_heads = cache_k.shape[3]
    k_row = cache_k.shape[4]
    v_head = cache_v.shape[4]
    aw = n_heads * v_head
    rw = d - aw
    assert k_row == v_head and w_in.shape[2] == 3 * aw + 2 * rw and aw == rw
    qk_half = k_row // 2
    n_exp = w_router.shape[2]
    lam_init = 0.8 - 0.6 * math.exp(-0.3 * 0)
    tp, ts = bp * seq, bs * n_new
    t_all = tp + ts

    w_in_bf = w_in[0].astype(BF16)
    wo_bf = w_o[0].astype(BF16)
    wr_t = w_router[0].T
    wai_bf = jnp.concatenate([w_rg_a[0], w_rg_i[0]], axis=-1).astype(BF16)
    wsg_bf, wsu_bf, wsd_bf = w_s_gate[0].astype(BF16), w_s_up[0].astype(BF16), w_s_down[0].astype(BF16)
    lamv = jnp.stack([lambda_q1[0], lambda_k1[0], lambda_q2[0], lambda_k2[0]])
    slopes = jnp.exp2(-8.0 * jnp.arange(1, n_heads + 1, dtype=F32) / n_heads)

    mod = _ada(jnp.concatenate([c_prompt, c_sample], axis=0), w_ada[0], b_ada[0])
    mod_p = [mod[:bp, i * d:(i + 1) * d] for i in range(6)]
    mod_s = [mod[bp:, i * d:(i + 1) * d] for i in range(6)]

    xp2, xs2 = x_prompt.reshape(tp, d), x_sample.reshape(ts, d)
    qscale = qk_half ** -0.5
    qp, kp, vp, up, gp, kpb, vpb = _inproj(xp2, mod_p[0], mod_p[1], g_norm1[0], w_in_bf, seq, qscale)
    qs, ks, vs, us, gs, _, _ = _inproj(xs2, mod_s[0], mod_s[1], g_norm1[0], w_in_bf, n_new, qscale)

    att_p = _pattn(qp, kpb, vpb, slopes, lamv, g_subln[0], bp, seq, n_heads, lam_init)

    q5 = qs.reshape(bs, n_new, n_heads, 2, qk_half).transpose(0, 2, 3, 1, 4)
    eye_h = jnp.eye(n_heads, dtype=BF16)
    eye_c = jnp.eye(2, dtype=BF16)
    wq = (q5[:, :, :, :, None, None, :] * eye_h[None, :, None, None, :, None, None]
          * eye_c[None, None, :, None, None, :, None]).reshape(bs, n_heads * 2 * n_new, aw)
    n_pool, page = cache_k.shape[1], cache_k.shape[2]
    att_s = _sattn(wq, ks.reshape(bs, n_new, aw), vs.reshape(bs, n_new, aw),
                   cache_k[0].reshape(n_pool, page, aw), cache_v[0].reshape(n_pool, page, aw),
                   page_table, lamv, g_subln[0], n_heads, lam_init).reshape(ts, aw)

    rg_args = (conv_w[0], conv_b[0], wai_bf, b_rg_a[0], b_rg_i[0], rg_lambda[0], g_rgnorm[0])
    rec_p, h_p, conv_p = _rglru(up, gp, jnp.zeros((bp, CONV_WIDTH - 1, rw), F32), jnp.zeros((bp, rw), F32),
                                *rg_args, bp, seq)
    rec_s, h_s, conv_s = _rglru(us, gs, state_conv[0], state_h[0], *rg_args, bs, n_new)

    x1p, h2p_all, st_all = _oproj(att_p, rec_p, xp2, mod_p[2], mod_p[3], mod_p[4], g_norm2[0], wo_bf, wr_t,
                                  seq, 0, t_all)
    x1s, h2p_all, st_all = _oproj(att_s, rec_s, xs2, mod_s[2], mod_s[3], mod_s[4], g_norm2[0], wo_bf, wr_t,
                                  n_new, tp, t_all, h2p_all, st_all)

    idx, wts, rank, counts = _route(st_all, router_bias[0])

    tm = EXPERT_TILE_ROWS
    counts = counts.reshape(n_exp)
    ptiles = (counts + tm - 1) // tm
    pend = jnp.cumsum(ptiles)
    pstart = pend - ptiles
    pos = pstart[idx] * tm + rank
    n_tiles = (t_all * TOP_K) // tm + n_exp
    tile_ids = jnp.arange(n_tiles, dtype=I32)
    tile_e = jnp.minimum(jnp.searchsorted(pend, tile_ids, side="right"), n_exp - 1).astype(I32)
    tile_nv = jnp.where(tile_ids < pend[-1],
                        jnp.clip(counts[tile_e] - (tile_ids - pstart[tile_e]) * tm, 0, tm), 0).astype(I32)

    xs_rows = _dispatch(h2p_all, pos, tile_nv)
    ys = _experts(xs_rows, tile_e, tile_nv, w_e_gate[0], w_e_up[0], w_e_down[0])

    wts_t = wts.T
    y_p = _combine(ys, pos[:, :tp], wts_t[:tp], h2p_all, x1p, mod_p[5], g_final, wsg_bf, wsu_bf, wsd_bf, seq, 0)
    y_s = _combine(ys, pos[:, tp:], wts_t[tp:], h2p_all, x1s, mod_s[5], g_final, wsg_bf, wsu_bf, wsd_bf, n_new, tp)

    return (y_p.reshape(bp, seq, d), y_s.reshape(bs, n_new, d),
            kp.reshape(1, bp, seq, n_heads, k_row), vp.reshape(1, bp, seq, n_heads, v_head),
            h_p.reshape(1, bp, rw), conv_p.reshape(1, bp, CONV_WIDTH - 1, rw),
            ks.reshape(1, bs, n_new, n_heads, k_row), vs.reshape(1, bs, n_new, n_heads, v_head),
            h_s.reshape(1, bs, rw), conv_s.reshape(1, bs, CONV_WIDTH - 1, rw))
```

```python
import functools
import math

import jax
import jax.numpy as jnp
import numpy as np
from jax import lax
from jax.experimental import pallas as pl
from jax.experimental.pallas import tpu as pltpu

F32 = jnp.float32
BF16 = jnp.bfloat16
I32 = jnp.int32
U32 = jnp.uint32

NORM_EPS = 1e-6
SUBLN_EPS = 1e-5
NEG = -1e30
RG_C = 8.0
ROUTED_SCALE = 2.5
N_GROUPS = 8
TOPK_GROUPS = 4
TOP_K = 8
CONV_WIDTH = 4

V7X_VMEM_LIMIT_BYTES = 56 * 1024 * 1024
EXPERT_TILE_ROWS = 128


def _cparams(n_axes):
    return pltpu.CompilerParams(
        dimension_semantics=("arbitrary",) * n_axes, vmem_limit_bytes=V7X_VMEM_LIMIT_BYTES
    )


def _pick(n, candidates):
    for c in candidates:
        if n % c == 0:
            return c
    return n


def _dot(a, b):
    return jnp.dot(a, b, preferred_element_type=F32)


def _dot_nt(a, b):
    return lax.dot_general(a, b, (((1,), (1,)), ((), ())), preferred_element_type=F32)


def _split(x):
    hi = x.astype(BF16)
    lo = (x - hi.astype(F32)).astype(BF16)
    return hi, lo


def _rms(x, eps):
    return x * lax.rsqrt(jnp.mean(x * x, axis=-1, keepdims=True) + eps)


def _silu(x):
    return x * jax.nn.sigmoid(x)


def _bf16_terms(x, n):
    terms = []
    for _ in range(n):
        bits = np.float32(x).view(np.uint32)
        bits = (bits + np.uint32(0x7FFF) + ((bits >> np.uint32(16)) & np.uint32(1))) & np.uint32(0xFFFF0000)
        t = float(bits.view(np.float32))
        terms.append(t)
        x -= t
    return tuple(terms)


LOG2E = math.log2(math.e)
LOG2E_BF16_TERMS = _bf16_terms(LOG2E, 3)
TILE_SUBLANES = 8


def _rows_from_tiles(ref, n_rows):
    return jnp.concatenate(
        [ref[pl.ds(sub, n_rows, stride=TILE_SUBLANES), :] for sub in range(TILE_SUBLANES)], axis=-1)


def _rows_to_tiles(ref, x):
    n_rows, width = x.shape
    lanes = width // TILE_SUBLANES
    for sub in range(TILE_SUBLANES):
        ref[pl.ds(sub, n_rows, stride=TILE_SUBLANES), :] = x[:, sub * lanes:(sub + 1) * lanes]


def _unpack_halves(xp):
    lo = pltpu.unpack_elementwise(xp, index=0, packed_dtype=BF16, unpacked_dtype=F32)
    hi = pltpu.unpack_elementwise(xp, index=1, packed_dtype=BF16, unpacked_dtype=F32)
    return lo.astype(BF16), hi.astype(BF16)


def _pack_halves(x):
    n = x.shape[-1] // 2
    return pltpu.pack_elementwise([x[:, :n], x[:, n:]], packed_dtype=BF16)


def _ada_kernel(c_ref, w_ref, b_ref, o_ref):
    a_hi, a_lo = _split(_silu(c_ref[...]))
    w_hi, w_lo = _split(w_ref[...])
    o_ref[...] = _dot(a_hi, w_hi) + (_dot(a_hi, w_lo) + _dot(a_lo, w_hi)) + b_ref[...]


def _ada(c, w, b):
    n, d = c.shape
    d_out = w.shape[1]
    tn = _pick(d_out, (512, 256, 128))
    return pl.pallas_call(
        _ada_kernel,
        grid=(d_out // tn,),
        in_specs=[
            pl.BlockSpec((n, d), lambda j: (0, 0)),
            pl.BlockSpec((d, tn), lambda j: (0, j)),
            pl.BlockSpec((1, tn), lambda j: (0, j)),
        ],
        out_specs=pl.BlockSpec((n, tn), lambda j: (0, j)),
        out_shape=jax.ShapeDtypeStruct((n, d_out), F32),
        compiler_params=_cparams(1),
        name="ada",
    )(c, w, b.reshape(1, d_out))


def _mod_spec(rows_per_batch, tm, d, n_tiles=None):
    clamp = (lambda i: i) if n_tiles is None else (lambda i: jnp.minimum(i, n_tiles - 1))
    if rows_per_batch % tm == 0:
        per = rows_per_batch // tm
        return pl.BlockSpec((None, 1, d), lambda i, *_: (clamp(i) // per, 0, 0))
    return pl.BlockSpec((None, tm, d), lambda i, *_: (clamp(i), 0, 0))


def _mod_array(m, rows_per_batch, tm):
    nb, d = m.shape
    if rows_per_batch % tm == 0:
        return m.reshape(nb, 1, d)
    assert tm % rows_per_batch == 0
    return jnp.repeat(m, rows_per_batch, axis=0).reshape(nb * rows_per_batch // tm, tm, d)


def _inproj_kernel(x_ref, sh_ref, sc_ref, g_ref, w_ref,
                   q_ref, k_ref, v_ref, u_ref, gt_ref, kb_ref, vb_ref, h_scr, *, qscale):
    j = pl.program_id(1)

    @pl.when(j == 0)
    def _():
        y = _rms(x_ref[...], NORM_EPS) * g_ref[...]
        h_scr[...] = (y * (1.0 + sc_ref[...]) + sh_ref[...]).astype(BF16)

    z = _dot(h_scr[...], w_ref[...])

    @pl.when(j == 0)
    def _():
        q_ref[...] = (z * qscale).astype(BF16)

    @pl.when(j == 1)
    def _():
        k_ref[...] = z
        kb_ref[...] = z.astype(BF16)

    @pl.when(j == 2)
    def _():
        v_ref[...] = z
        vb_ref[...] = z.astype(BF16)

    @pl.when(j == 3)
    def _():
        u_ref[...] = z

    @pl.when(j == 4)
    def _():
        gt_ref[...] = z


def _inproj(x2d, shift, scale, g, w_bf, rows_per_batch, qscale):
    r, d = x2d.shape
    wd = w_bf.shape[1] // 5
    tm = _pick(r, (512, 256, 128, 64, 32, 16, 8))
    row = lambda i, j: (i, 0)
    f32o = jax.ShapeDtypeStruct((r, wd), F32)
    bfo = jax.ShapeDtypeStruct((r, wd), BF16)
    return pl.pallas_call(
        functools.partial(_inproj_kernel, qscale=qscale),
        grid=(r // tm, 5),
        in_specs=[
            pl.BlockSpec((tm, d), row),
            _mod_spec(rows_per_batch, tm, d),
            _mod_spec(rows_per_batch, tm, d),
            pl.BlockSpec((1, d), lambda i, j: (0, 0)),
            pl.BlockSpec((d, wd), lambda i, j: (0, j)),
        ],
        out_specs=[pl.BlockSpec((tm, wd), row)] * 7,
        out_shape=[bfo, f32o, f32o, f32o, f32o, bfo, bfo],
        scratch_shapes=[pltpu.VMEM((tm, d), BF16)],
        compiler_params=_cparams(2),
        name="inproj",
    )(x2d, _mod_array(shift, rows_per_batch, tm), _mod_array(scale, rows_per_batch, tm), g.reshape(1, d), w_bf)


def _lam(lamv_ref, lam_init):
    lv = lamv_ref[...]
    s1 = jnp.sum(lv[0:1] * lv[1:2], axis=-1, keepdims=True)
    s2 = jnp.sum(lv[2:3] * lv[3:4], axis=-1, keepdims=True)
    return jnp.exp(s1) - jnp.exp(s2) + lam_init


def _online_update(s, v, m_scr, l_scr, acc_scr, rows=slice(None), shift=None):
    m_old = m_scr[rows, :]
    s_max = jnp.max(s, axis=-1, keepdims=True)
    m_new = jnp.maximum(m_old, s_max if shift is None else s_max + shift)
    alpha = jnp.exp2(m_old - m_new)
    p = jnp.exp2(s - (m_new if shift is None else m_new - shift))
    l_scr[rows, :] = alpha * l_scr[rows, :] + jnp.sum(p, axis=-1, keepdims=True)
    acc_scr[rows, :] = alpha * acc_scr[rows, :] + _dot(p.astype(BF16), v)
    m_scr[rows, :] = m_new


def _pattn_kernel(slopes_ref, lamv_ref, gs_ref, q_ref, k_ref, v_ref, o_ref,
                  q_scr, ka_scr, m_scr, l_scr, acc_scr, *, tq, rc, lam_init):
    h = pl.program_id(1)
    qi = pl.program_id(2)
    slope = slopes_ref[h]
    hd = q_ref.shape[1]
    half = hd // 2
    n_terms = len(LOG2E_BF16_TERMS)

    q = q_ref[...]
    lane = lax.broadcasted_iota(I32, q.shape, 1)
    zero = jnp.zeros_like(q)
    qa = jnp.zeros(q.shape, F32)
    for n, term in enumerate(LOG2E_BF16_TERMS):
        qa = jnp.where(jnp.logical_or(lane == n, lane == n + n_terms), term, qa)
    qa = qa.astype(BF16)
    q_scr[0:tq, 0:hd] = jnp.where(lane < half, q, zero)
    q_scr[tq:2 * tq, 0:hd] = jnp.where(lane >= half, q, zero)
    q_scr[0:tq, hd:2 * hd] = qa
    q_scr[tq:2 * tq, hd:2 * hd] = qa
    c = lax.broadcasted_iota(I32, (tq, hd), 0)
    c_lo = c % 256
    ka = jnp.where(lane < n_terms, c_lo.astype(F32) * slope,
                   jnp.where(lane < 2 * n_terms, (c - c_lo).astype(F32) * slope, 0.0))
    ka_scr[...] = ka.astype(BF16)

    m_scr[...] = jnp.full(m_scr.shape, NEG, F32)
    l_scr[...] = jnp.zeros(l_scr.shape, F32)
    acc_scr[...] = jnp.zeros(acc_scr.shape, F32)

    def step(j, masked):
        start = pl.multiple_of(j * tq, tq)
        kaug = jnp.concatenate([k_ref[pl.ds(start, tq), :], ka_scr[...]], axis=-1)
        v = v_ref[pl.ds(start, tq), :]
        for r0 in range(0, 2 * tq, rc):
            q0 = r0 % tq
            ncol = q0 + rc if masked else tq
            rowpos = q0 + lax.broadcasted_iota(I32, (rc, 1), 0)
            s = _dot_nt(q_scr[r0:r0 + rc, :], kaug[:ncol])
            if masked:
                s = jnp.where(lax.broadcasted_iota(I32, (rc, ncol), 1) <= rowpos, s, NEG)
            shift = ((j - qi) * tq - rowpos).astype(F32) * (slope * LOG2E)
            _online_update(s, v[:ncol], m_scr, l_scr, acc_scr, rows=slice(r0, r0 + rc), shift=shift)

    def body(j, carry):
        step(j, False)
        return carry

    lax.fori_loop(0, qi, body, 0)
    step(qi, True)

    lam = _lam(lamv_ref, lam_init)
    o = acc_scr[...] / l_scr[...]
    att = o[:tq] - lam * o[tq:]
    att = _rms(att, SUBLN_EPS) * gs_ref[...] * (1.0 - lam_init)
    o_ref[...] = att.astype(BF16)


def _pattn(qb, kb, vb, slopes, lamv, g_subln, n_batch, seq, n_heads, lam_init):
    r, aw = qb.shape
    hd = aw // n_heads
    tq = _pick(seq, (512, 256, 128))
    rc = min(tq, 256)
    nq = seq // tq
    return pl.pallas_call(
        functools.partial(_pattn_kernel, tq=tq, rc=rc, lam_init=lam_init),
        grid_spec=pltpu.PrefetchScalarGridSpec(
            num_scalar_prefetch=1,
            grid=(n_batch, n_heads, nq),
            in_specs=[
                pl.BlockSpec(lamv.shape, lambda b, h, i, *_: (0, 0)),
                pl.BlockSpec((1, hd), lambda b, h, i, *_: (0, 0)),
                pl.BlockSpec((tq, hd), lambda b, h, i, *_: (b * nq + i, h)),
                pl.BlockSpec((seq, hd), lambda b, h, i, *_: (b, h)),
                pl.BlockSpec((seq, hd), lambda b, h, i, *_: (b, h)),
            ],
            out_specs=pl.BlockSpec((tq, hd), lambda b, h, i, *_: (b * nq + i, h)),
            scratch_shapes=[
                pltpu.VMEM((2 * tq, 2 * hd), BF16),
                pltpu.VMEM((tq, hd), BF16),
                pltpu.VMEM((2 * tq, 1), F32),
                pltpu.VMEM((2 * tq, 1), F32),
                pltpu.VMEM((2 * tq, hd), F32),
            ],
        ),
        out_shape=jax.ShapeDtypeStruct((r, aw), BF16),
        compiler_params=_cparams(3),
        name="pattn",
    )(slopes, lamv, g_subln.reshape(1, hd), qb, kb, vb)


def _sattn_kernel(pt_ref, lamv_ref, gs_ref, wq_ref, kn_ref, vn_ref, *rest,
                  n_pages_step, page, past, n_new, n_heads, lam_init):
    k_refs = rest[:n_pages_step]
    v_refs = rest[n_pages_step:2 * n_pages_step]
    o_ref = rest[2 * n_pages_step]
    m_scr, l_scr, acc_scr = rest[2 * n_pages_step + 1:]
    j = pl.program_id(1)
    n_rows = wq_ref.shape[0]
    hd = wq_ref.shape[1] // n_heads
    tk = n_pages_step * page

    r = lax.broadcasted_iota(I32, (n_rows, 1), 0)
    head = r // (2 * n_new)
    qi = r % n_new
    slope = jnp.exp2(-(head + 1).astype(F32)) * LOG2E

    @pl.when(j == 0)
    def _():
        m_scr[...] = jnp.full(m_scr.shape, NEG, F32)
        l_scr[...] = jnp.zeros(l_scr.shape, F32)
        acc_scr[...] = jnp.zeros(acc_scr.shape, F32)

    wq = wq_ref[...]
    kc = jnp.concatenate([_rows_from_tiles(kr, page).astype(BF16) for kr in k_refs], axis=0)
    vc = jnp.concatenate([_rows_from_tiles(vr, page).astype(BF16) for vr in v_refs], axis=0)
    t = j * tk + lax.broadcasted_iota(I32, (1, tk), 1)
    s = _dot_nt(wq, kc) - slope * (past + qi - t).astype(F32)
    _online_update(s, vc, m_scr, l_scr, acc_scr)

    @pl.when(j == pl.num_programs(1) - 1)
    def _():
        pad = jnp.zeros((page - n_new, kn_ref.shape[1]), BF16)
        kn = jnp.concatenate([kn_ref[...].astype(BF16), pad], axis=0)
        vn = jnp.concatenate([vn_ref[...].astype(BF16), pad], axis=0)
        tj = lax.broadcasted_iota(I32, (1, page), 1)
        sn = _dot_nt(wq, kn) - slope * (qi - tj).astype(F32)
        sn = jnp.where(tj <= qi, sn, NEG)
        _online_update(sn, vn, m_scr, l_scr, acc_scr)

        lam = _lam(lamv_ref, lam_init)
        o = acc_scr[...] / l_scr[...]
        outs = []
        for h in range(n_heads):
            blk = o[h * 2 * n_new:(h + 1) * 2 * n_new, h * hd:(h + 1) * hd]
            att = blk[:n_new] - lam * blk[n_new:]
            outs.append(_rms(att, SUBLN_EPS) * gs_ref[...] * (1.0 - lam_init))
        o_ref[...] = jnp.concatenate(outs, axis=-1).astype(BF16)


def _sattn(wq, k_new, v_new, cache_k2, cache_v2, page_table, lamv, g_subln, n_heads, lam_init):
    n_seq, n_rows, aw = wq.shape
    n_new = k_new.shape[1]
    hd = aw // n_heads
    page = cache_k2.shape[1] // n_heads
    n_pages = page_table.shape[1]
    pstep = _pick(n_pages, (8, 4, 2, 1))
    past = n_pages * page
    assert n_heads == TILE_SUBLANES and hd == 128

    def page_spec(p):
        return pl.BlockSpec((None, page * n_heads, hd), lambda b, j, pt: (pt[b, j * pstep + p], 0, 0))

    seq_spec = lambda rows: pl.BlockSpec((None, rows, aw), lambda b, j, pt: (b, 0, 0))
    return pl.pallas_call(
        functools.partial(_sattn_kernel, n_pages_step=pstep, page=page, past=past, n_new=n_new,
                          n_heads=n_heads, lam_init=lam_init),
        grid_spec=pltpu.PrefetchScalarGridSpec(
            num_scalar_prefetch=1,
            grid=(n_seq, n_pages // pstep),
            in_specs=[
                pl.BlockSpec(lamv.shape, lambda b, j, pt: (0, 0)),
                pl.BlockSpec((1, hd), lambda b, j, pt: (0, 0)),
                seq_spec(n_rows), seq_spec(n_new), seq_spec(n_new),
            ] + [page_spec(p) for p in range(pstep)] * 2,
            out_specs=seq_spec(n_new),
            scratch_shapes=[
                pltpu.VMEM((n_rows, 1), F32),
                pltpu.VMEM((n_rows, 1), F32),
                pltpu.VMEM((n_rows, aw), F32),
            ],
        ),
        out_shape=jax.ShapeDtypeStruct((n_seq, n_new, aw), BF16),
        compiler_params=_cparams(2),
        name="sattn",
    )(page_table, lamv, g_subln.reshape(1, hd), wq, k_new, v_new,
      *([cache_k2] * pstep), *([cache_v2] * pstep))


def _gelu_tanh(x):
    return x * (0.5 * (1.0 + jnp.tanh(math.sqrt(2.0 / math.pi) * (x + 0.044715 * (x * x * x)))))


def _softplus(x):
    return jnp.maximum(x, 0.0) + jnp.log1p(jnp.exp(-jnp.abs(x)))


def _rglru_kernel(u_ref, gt_ref, c0_ref, h0_ref, cw_ref, cb_ref, wai_ref, ba_ref, bi_ref, lam_ref, gn_ref,
                  rec_ref, ht_ref, cout_ref, ubuf, hcar, *, tl, n_blocks):
    t = pl.program_id(1)
    halo = CONV_WIDTH - 1
    base = 8

    @pl.when(t == 0)
    def _():
        ubuf[base - halo:base, :] = c0_ref[...]
        hcar[...] = h0_ref[...]

    ubuf[base:base + tl, :] = u_ref[...]
    cw = cw_ref[...]
    xc = cb_ref[...] + cw[0:1] * ubuf[base - halo:base - halo + tl, :]
    for jj in range(1, CONV_WIDTH):
        xc = xc + cw[jj:jj + 1] * ubuf[base - halo + jj:base - halo + jj + tl, :]
    tail = ubuf[base + tl - halo:base + tl, :]
    ubuf[base - halo:base, :] = tail
    cout_ref[...] = tail

    bw = xc.shape[1] // n_blocks
    za, zi = [], []
    for n in range(n_blocks):
        z = _dot(xc[:, n * bw:(n + 1) * bw].astype(BF16), wai_ref[n])
        za.append(z[:, :bw])
        zi.append(z[:, bw:])
    r = jax.nn.sigmoid(jnp.concatenate(za, axis=-1) + ba_ref[...])
    i = jax.nn.sigmoid(jnp.concatenate(zi, axis=-1) + bi_ref[...])
    log_a = -RG_C * r * _softplus(-lam_ref[...])
    a = jnp.exp(log_a)
    th = jnp.tanh(log_a)
    b = xc * i * jnp.sqrt(-2.0 * th / (1.0 - th))

    rowi = lax.broadcasted_iota(I32, a.shape, 0)
    sft = 1
    while sft < tl:
        keep = rowi >= sft
        a_prev = jnp.where(keep, pltpu.roll(a, sft, 0), 1.0)
        b_prev = jnp.where(keep, pltpu.roll(b, sft, 0), 0.0)
        b = a * b_prev + b
        a = a * a_prev
        sft *= 2
    hs = a * hcar[...] + b
    h_last = hs[tl - 1:tl, :]
    hcar[...] = h_last
    ht_ref[...] = h_last

    rec = hs * _gelu_tanh(gt_ref[...])
    rec_ref[...] = (_rms(rec, NORM_EPS) * gn_ref[...]).astype(BF16)


def _rglru(u2d, gate2d, conv0, h0, conv_w, conv_b, wai_bf, b_a, b_i, rg_lambda, g_rgnorm, n_batch, seq):
    r, w = u2d.shape
    tl = _pick(seq, (256, 128, 64, 32, 16, 8))
    nt = seq // tl
    n_blocks = wai_bf.shape[0]
    halo = CONV_WIDTH - 1
    row = lambda b, t: (b * nt + t, 0)
    vec = pl.BlockSpec((1, w), lambda b, t: (0, 0))
    rec, ht, cout = pl.pallas_call(
        functools.partial(_rglru_kernel, tl=tl, n_blocks=n_blocks),
        grid=(n_batch, nt),
        in_specs=[
            pl.BlockSpec((tl, w), row),
            pl.BlockSpec((tl, w), row),
            pl.BlockSpec((None, halo, w), lambda b, t: (b, 0, 0)),
            pl.BlockSpec((None, 1, w), lambda b, t: (b, 0, 0)),
            pl.BlockSpec((CONV_WIDTH, w), lambda b, t: (0, 0)),
            vec,
            pl.BlockSpec(wai_bf.shape, lambda b, t: (0, 0, 0)),
            vec, vec, vec, vec,
        ],
        out_specs=[
            pl.BlockSpec((tl, w), row),
            pl.BlockSpec((None, 1, w), lambda b, t: (b, 0, 0)),
            pl.BlockSpec((None, halo, w), lambda b, t: (b, 0, 0)),
        ],
        out_shape=[
            jax.ShapeDtypeStruct((r, w), BF16),
            jax.ShapeDtypeStruct((n_batch, 1, w), F32),
            jax.ShapeDtypeStruct((n_batch, halo, w), F32),
        ],
        scratch_shapes=[pltpu.VMEM((tl + 8, w), F32), pltpu.VMEM((1, w), F32)],
        compiler_params=_cparams(2),
        name="rglru",
    )(u2d, gate2d, conv0, h0.reshape(n_batch, 1, w), conv_w, conv_b.reshape(1, w), wai_bf,
      b_a.reshape(1, w), b_i.reshape(1, w), rg_lambda.reshape(1, w), g_rgnorm.reshape(1, w))
    return rec, ht.reshape(n_batch, w), cout


def _oproj_kernel(att_ref, rec_ref, x_ref, g1_ref, sh_ref, sc_ref, gn_ref, wo_ref, wrt_ref, *rest,
                  aliased, n_tiles):
    x1_ref, h2p_ref, st_ref = rest[2:] if aliased else rest
    aw = att_ref.shape[1]
    i = pl.program_id(0)

    @pl.when(i < n_tiles)
    def _():
        mix = _dot(att_ref[...], wo_ref[:aw, :]) + _dot(rec_ref[...], wo_ref[aw:, :])
        x1 = x_ref[...] + g1_ref[...] * mix
        x1_ref[...] = x1
        h2 = (_rms(x1, NORM_EPS) * gn_ref[...]) * (1.0 + sc_ref[...]) + sh_ref[...]
        _rows_to_tiles(h2p_ref, _pack_halves(h2))
        h_hi, h_lo = _split(h2)
        w_hi, w_lo = _split(wrt_ref[...])
        logits_t = _dot_nt(w_hi, h_hi) + (_dot_nt(w_hi, h_lo) + _dot_nt(w_lo, h_hi))
        st_ref[...] = jax.nn.sigmoid(logits_t)

    @pl.when(i >= n_tiles)
    def _():
        h2p_ref[...] = jnp.zeros(h2p_ref.shape, U32)
        st_ref[...] = jnp.zeros(st_ref.shape, F32)


def _oproj(att, rec, x2d, g1, shift, scale, g_norm2, wo_bf, wr_t, rows_per_batch, row_offset, total_rows,
           h2p_all=None, st_all=None):
    r, d = x2d.shape
    aw = att.shape[1]
    n_exp = wr_t.shape[0]
    tm = _pick(r, (256, 128))
    assert row_offset % tm == 0
    off = row_offset // tm
    lanes = d // 2 // TILE_SUBLANES
    assert lanes == 128
    aliased = h2p_all is not None
    n_tiles = r // tm
    n_fill = 0 if aliased else (total_rows - r) // tm
    assert aliased or (row_offset == 0 and (total_rows - r) % tm == 0)
    row = lambda i: (jnp.minimum(i, n_tiles - 1), 0)
    mspec = _mod_spec(rows_per_batch, tm, d, n_tiles)
    in_specs = [
        pl.BlockSpec((tm, aw), row),
        pl.BlockSpec((tm, d - aw), row),
        pl.BlockSpec((tm, d), row),
        mspec, mspec, mspec,
        pl.BlockSpec((1, d), lambda i: (0, 0)),
        pl.BlockSpec((d, d), lambda i: (0, 0)),
        pl.BlockSpec((n_exp, d), lambda i: (0, 0)),
    ]
    args = [att, rec, x2d, _mod_array(g1, rows_per_batch, tm), _mod_array(shift, rows_per_batch, tm),
            _mod_array(scale, rows_per_batch, tm), g_norm2.reshape(1, d), wo_bf, wr_t]
    io_alias = {}
    if aliased:
        in_specs += [pl.BlockSpec(memory_space=pl.ANY), pl.BlockSpec(memory_space=pl.ANY)]
        io_alias = {len(args): 1, len(args) + 1: 2}
        args += [h2p_all, st_all]
    return pl.pallas_call(
        functools.partial(_oproj_kernel, aliased=aliased, n_tiles=n_tiles),
        grid=(n_tiles + n_fill,),
        in_specs=in_specs,
        out_specs=[
            pl.BlockSpec((tm, d), row),
            pl.BlockSpec((tm * TILE_SUBLANES, lanes), lambda i: (i + off, 0)),
            pl.BlockSpec((n_exp, tm), lambda i: (0, i + off)),
        ],
        out_shape=[
            jax.ShapeDtypeStruct((r, d), F32),
            jax.ShapeDtypeStruct((total_rows * TILE_SUBLANES, lanes), U32),
            jax.ShapeDtypeStruct((n_exp, total_rows), F32),
        ],
        input_output_aliases=io_alias,
        compiler_params=_cparams(1),
        name="oproj",
    )(*args)


def _route_kernel(st_ref, rb_ref, idx_ref, wts_ref, rank_ref, cnt_ref, carry):
    i = pl.program_id(0)
    n_exp, tr = st_ref.shape
    gsz = n_exp // N_GROUPS

    @pl.when(i == 0)
    def _():
        carry[...] = jnp.zeros(carry.shape, F32)

    s = st_ref[...]
    biased = s + rb_ref[...]
    g = biased.reshape(N_GROUPS, gsz, tr)
    within = lax.broadcasted_iota(I32, g.shape, 1)
    m1 = jnp.max(g, axis=1, keepdims=True)
    first = jnp.min(jnp.where(g == m1, within, gsz), axis=1, keepdims=True)
    m2 = jnp.max(jnp.where(within == first, -jnp.inf, g), axis=1, keepdims=True)
    gscore = (m1 + m2).reshape(N_GROUPS, tr)

    gidx = lax.broadcasted_iota(I32, gscore.shape, 0)
    gsel = jnp.zeros(gscore.shape, F32)
    for _ in range(TOPK_GROUPS):
        mg = jnp.max(gscore, axis=0, keepdims=True)
        fg = jnp.min(jnp.where(gscore == mg, gidx, N_GROUPS), axis=0, keepdims=True)
        hit = gidx == fg
        gsel = jnp.where(hit, 1.0, gsel)
        gscore = jnp.where(hit, -jnp.inf, gscore)
    masked = jnp.where(gsel.reshape(N_GROUPS, 1, tr) > 0.5, g, -jnp.inf).reshape(n_exp, tr)

    eidx = lax.broadcasted_iota(I32, (n_exp, tr), 0)
    idxs, ws = [], []
    chosen = jnp.zeros((n_exp, tr), jnp.bool_)
    for _ in range(TOP_K):
        mv = jnp.max(masked, axis=0, keepdims=True)
        fe = jnp.min(jnp.where(masked == mv, eidx, n_exp), axis=0, keepdims=True)
        hit = eidx == fe
        idxs.append(fe)
        ws.append(jnp.sum(jnp.where(hit, s, 0.0), axis=0, keepdims=True))
        chosen = chosen | hit
        masked = jnp.where(hit, -jnp.inf, masked)
    idx = jnp.concatenate(idxs, axis=0)
    w = jnp.concatenate(ws, axis=0)
    idx_ref[...] = idx
    wts_ref[...] = w / jnp.sum(w, axis=0, keepdims=True) * ROUTED_SCALE

    cmat = jnp.where(chosen, 1.0, 0.0)
    before = lax.broadcasted_iota(I32, (tr, tr), 0) < lax.broadcasted_iota(I32, (tr, tr), 1)
    prior = _dot(cmat.astype(BF16), jnp.where(before, 1.0, 0.0).astype(BF16)) + carry[...]
    ranks = [jnp.sum(jnp.where(eidx == idxs[k], prior, 0.0), axis=0, keepdims=True) for k in range(TOP_K)]
    rank_ref[...] = jnp.concatenate(ranks, axis=0).astype(I32)
    carry[...] = carry[...] + jnp.sum(cmat, axis=1, keepdims=True)
    cnt_ref[...] = carry[...].astype(I32)


def _route(st_all, router_bias):
    n_exp, t_all = st_all.shape
    tr = _pick(t_all, (640, 512, 256, 128))
    col = lambda i: (0, i)
    o8 = lambda dt: jax.ShapeDtypeStruct((TOP_K, t_all), dt)
    return pl.pallas_call(
        _route_kernel,
        grid=(t_all // tr,),
        in_specs=[pl.BlockSpec((n_exp, tr), col), pl.BlockSpec((n_exp, 1), lambda i: (0, 0))],
        out_specs=[pl.BlockSpec((TOP_K, tr), col)] * 3 + [pl.BlockSpec((n_exp, 1), lambda i: (0, 0))],
        out_shape=[o8(I32), o8(F32), o8(I32), jax.ShapeDtypeStruct((n_exp, 1), I32)],
        scratch_shapes=[pltpu.VMEM((n_exp, 1), F32)],
        compiler_params=_cparams(1),
        name="route",
    )(st_all, router_bias.reshape(n_exp, 1))


def _pos_kernel(idx_ref, rank_ref, start_ref, pos_ref):
    n_exp = start_ref.shape[0]
    tr = idx_ref.shape[1]
    eidx = lax.broadcasted_iota(I32, (n_exp, tr), 0)
    start = start_ref[...]
    rows = [jnp.sum(jnp.where(eidx == idx_ref[k:k + 1, :], start, 0.0), axis=0, keepdims=True)
            for k in range(TOP_K)]
    pos_ref[...] = jnp.concatenate(rows, axis=0).astype(I32) + rank_ref[...]


def _pos(idx, rank, start_rows):
    n_exp = start_rows.shape[0]
    t_all = idx.shape[1]
    tr = _pick(t_all, (640, 512, 256, 128))
    col = pl.BlockSpec((TOP_K, tr), lambda i: (0, i))
    return pl.pallas_call(
        _pos_kernel,
        grid=(t_all // tr,),
        in_specs=[col, col, pl.BlockSpec((n_exp, 1), lambda i: (0, 0))],
        out_specs=col,
        out_shape=jax.ShapeDtypeStruct((TOP_K, t_all), I32),
        compiler_params=_cparams(1),
        name="pos",
    )(idx, rank, start_rows.astype(F32).reshape(n_exp, 1))


def _token_copy(src, src_tok, dst, dst_tok, sem):
    rows = lambda t: pl.ds(pl.multiple_of(t * TILE_SUBLANES, TILE_SUBLANES), TILE_SUBLANES)
    return pltpu.make_async_copy(src.at[rows(src_tok)], dst.at[rows(dst_tok)], sem)


def _dispatch_kernel(tnv_ref, pos_ref, x_ref, xs_ref, zeros, sem, *, td, n_tok_tiles, tm):
    i = pl.program_id(0)

    @pl.when(i == 0)
    def _():
        zeros[...] = jnp.zeros(zeros.shape, U32)

    @pl.when(i < n_tok_tiles)
    def _():
        def issue(t, carry):
            for k in range(TOP_K):
                _token_copy(x_ref, t, xs_ref, pos_ref[0, k * td + t], sem).start()
            return carry

        lax.fori_loop(0, td, issue, 0)

        def drain(n, carry):
            _token_copy(x_ref, 0, xs_ref, 0, sem).wait()
            return carry

        lax.fori_loop(0, td * TOP_K, drain, 0)

    @pl.when(i >= n_tok_tiles)
    def _():
        tile = i - n_tok_tiles
        nv = tnv_ref[tile]
        base = tile * tm

        @pl.when(nv == 0)
        def _():
            span = pl.ds(pl.multiple_of(base * TILE_SUBLANES, tm * TILE_SUBLANES), tm * TILE_SUBLANES)
            cp = pltpu.make_async_copy(zeros, xs_ref.at[span], sem)
            cp.start()
            cp.wait()

        @pl.when(jnp.logical_and(nv > 0, nv < tm))
        def _():
            def fill(r, carry):
                _token_copy(zeros, 0, xs_ref, base + r, sem).start()
                return carry

            lax.fori_loop(nv, tm, fill, 0)

            def drain(r, carry):
                _token_copy(zeros, 0, xs_ref, 0, sem).wait()
                return carry

            lax.fori_loop(nv, tm, drain, 0)


def _tile_pos(pos, tile):
    k, t = pos.shape
    return pos.reshape(k, t // tile, tile).transpose(1, 0, 2).reshape(t // tile, 1, k * tile)


def _dispatch(h2p_all, pos, tile_nv):
    rows, lanes = h2p_all.shape
    t_all = rows // TILE_SUBLANES
    tm = EXPERT_TILE_ROWS
    n_tiles = tile_nv.shape[0]
    td = _pick(t_all, (256, 128, 64, 32, 16, 8))
    n_tok_tiles = t_all // td
    tok = lambda i: jnp.minimum(i, n_tok_tiles - 1)
    return pl.pallas_call(
        functools.partial(_dispatch_kernel, td=td, n_tok_tiles=n_tok_tiles, tm=tm),
        grid_spec=pltpu.PrefetchScalarGridSpec(
            num_scalar_prefetch=1,
            grid=(n_tok_tiles + n_tiles,),
            in_specs=[
                pl.BlockSpec((None, 1, TOP_K * td), lambda i, tnv: (tok(i), 0, 0), memory_space=pltpu.SMEM),
                pl.BlockSpec((td * TILE_SUBLANES, lanes), lambda i, tnv: (tok(i), 0)),
            ],
            out_specs=pl.BlockSpec(memory_space=pl.ANY),
            scratch_shapes=[pltpu.VMEM((tm * TILE_SUBLANES, lanes), U32), pltpu.SemaphoreType.DMA(())],
        ),
        out_shape=jax.ShapeDtypeStruct((n_tiles * tm * TILE_SUBLANES, lanes), U32),
        compiler_params=_cparams(1),
        name="dispatch",
    )(tile_nv, _tile_pos(pos, td), h2p_all)


def _swiglu_packed(xp, wg, wu, wd):
    xa, xb = _unpack_halves(xp)
    half = xp.shape[1]
    g = _dot(xa, wg[:half, :]) + _dot(xb, wg[half:, :])
    u = _dot(xa, wu[:half, :]) + _dot(xb, wu[half:, :])
    return _dot((_silu(g) * u).astype(BF16), wd[...])


def _experts_kernel(te_ref, tnv_ref, tord_ref, tnext_ref, xs_ref, wg_hbm, wu_hbm, wd_hbm, ys_ref,
                    wg_f, wu_f, wd_f, wg_b, wu_b, wd_b, sems):
    i = pl.program_id(0)
    nv = tnv_ref[i]
    expert = te_ref[i]
    new_expert = jnp.logical_or(i == 0, expert != te_ref[jnp.maximum(i - 1, 0)])
    slot = tord_ref[i] % 2
    tm = ys_ref.shape[0] // TILE_SUBLANES

    def weight_copies(e, sl):
        return [pltpu.make_async_copy(hbm.at[e], buf.at[sl], sems.at[n, sl])
                for n, (hbm, buf) in enumerate(((wg_hbm, wg_f), (wu_hbm, wu_f), (wd_hbm, wd_f)))]

    @pl.when(jnp.logical_and(nv > 0, new_expert))
    def _():
        @pl.when(i == 0)
        def _():
            for cp in weight_copies(expert, slot):
                cp.start()

        for cp in weight_copies(expert, slot):
            cp.wait()
        nxt = tnext_ref[i]

        @pl.when(nxt >= 0)
        def _():
            for cp in weight_copies(nxt, 1 - slot):
                cp.start()

        wg_b[...] = wg_f[slot].astype(BF16)
        wu_b[...] = wu_f[slot].astype(BF16)
        wd_b[...] = wd_f[slot].astype(BF16)

    @pl.when(nv > 0)
    def _():
        xp = _rows_from_tiles(xs_ref, tm)
        rowi = lax.broadcasted_iota(I32, xp.shape, 0)
        xp = jnp.where(rowi < nv, xp, jnp.uint32(0))
        _rows_to_tiles(ys_ref, _pack_halves(_swiglu_packed(xp, wg_b, wu_b, wd_b)))

    @pl.when(nv == 0)
    def _():
        ys_ref[...] = jnp.zeros(ys_ref.shape, U32)


def _experts(xs, tile_e, tile_nv, tile_ord, tile_next, w_gate, w_up, w_down):
    rows, lanes = xs.shape
    tm = EXPERT_TILE_ROWS
    n_exp, d, ff = w_gate.shape
    blk = pl.BlockSpec((tm * TILE_SUBLANES, lanes), lambda i, *_: (i, 0))
    hbm = pl.BlockSpec(memory_space=pl.ANY)
    return pl.pallas_call(
        _experts_kernel,
        grid_spec=pltpu.PrefetchScalarGridSpec(
            num_scalar_prefetch=4,
            grid=(rows // (tm * TILE_SUBLANES),),
            in_specs=[blk, hbm, hbm, hbm],
            out_specs=blk,
            scratch_shapes=[
                pltpu.VMEM((2, d, ff), F32), pltpu.VMEM((2, d, ff), F32), pltpu.VMEM((2, ff, d), F32),
                pltpu.VMEM((d, ff), BF16), pltpu.VMEM((d, ff), BF16), pltpu.VMEM((ff, d), BF16),
                pltpu.SemaphoreType.DMA((3, 2)),
            ],
        ),
        out_shape=jax.ShapeDtypeStruct((rows, lanes), U32),
        compiler_params=_cparams(1),
        name="experts",
    )(tile_e, tile_nv, tile_ord, tile_next, xs, w_gate, w_up, w_down)


def _combine_kernel(pos_ref, w_ref, h2p_ref, x1_ref, g2_ref, gf_ref, wsg_ref, wsu_ref, wsd_ref, ys_ref,
                    o_ref, buf, sem, *, tc):
    def issue(t, carry):
        for k in range(TOP_K):
            _token_copy(ys_ref, pos_ref[0, k * tc + t], buf.at[k], t, sem).start()
        return carry

    lax.fori_loop(0, tc, issue, 0)
    shared = _swiglu_packed(_rows_from_tiles(h2p_ref, tc), wsg_ref, wsu_ref, wsd_ref)

    def drain(n, carry):
        _token_copy(ys_ref, 0, buf.at[0], 0, sem).wait()
        return carry

    lax.fori_loop(0, tc * TOP_K, drain, 0)
    w = w_ref[...]
    lo, hi = None, None
    for k in range(TOP_K):
        yk = _rows_from_tiles(buf.at[k], tc)
        wk = w[:, k:k + 1]
        yl = pltpu.unpack_elementwise(yk, index=0, packed_dtype=BF16, unpacked_dtype=F32) * wk
        yh = pltpu.unpack_elementwise(yk, index=1, packed_dtype=BF16, unpacked_dtype=F32) * wk
        lo = yl if lo is None else lo + yl
        hi = yh if hi is None else hi + yh
    routed = jnp.concatenate([lo, hi], axis=-1)
    x2 = x1_ref[...] + g2_ref[...] * (routed + shared)
    o_ref[...] = _rms(x2, NORM_EPS) * gf_ref[...]


def _combine(ys, pos, wts_t, h2p_all, x1, g2, g_final, wsg_bf, wsu_bf, wsd_bf, rows_per_batch, row_offset):
    r, d = x1.shape
    lanes = h2p_all.shape[1]
    tc = _pick(r, (128, 64, 32, 16, 8))
    assert row_offset % tc == 0
    off = row_offset // tc
    ff = wsg_bf.shape[1]
    const = lambda shp: pl.BlockSpec(shp, lambda i: (0,) * len(shp))
    return pl.pallas_call(
        functools.partial(_combine_kernel, tc=tc),
        grid=(r // tc,),
        in_specs=[
            pl.BlockSpec((None, 1, TOP_K * tc), lambda i: (i, 0, 0), memory_space=pltpu.SMEM),
            pl.BlockSpec((tc, TOP_K), lambda i: (i, 0)),
            pl.BlockSpec((tc * TILE_SUBLANES, lanes), lambda i: (i + off, 0)),
            pl.BlockSpec((tc, d), lambda i: (i, 0)),
            _mod_spec(rows_per_batch, tc, d),
            const((1, d)), const((d, ff)), const((d, ff)), const((ff, d)),
            pl.BlockSpec(memory_space=pl.ANY),
        ],
        out_specs=pl.BlockSpec((tc, d), lambda i: (i, 0)),
        out_shape=jax.ShapeDtypeStruct((r, d), F32),
        scratch_shapes=[pltpu.VMEM((TOP_K, tc * TILE_SUBLANES, lanes), U32), pltpu.SemaphoreType.DMA(())],
        compiler_params=_cparams(1),
        name="combine",
    )(_tile_pos(pos, tc), wts_t, h2p_all, x1, _mod_array(g2, rows_per_batch, tc), g_final.reshape(1, d),
      wsg_bf, wsu_bf, wsd_bf, ys)


def kernel(x_prompt, x_sample, cache_k, cache_v, state_h, state_conv, page_table, c_prompt, c_sample,
           w_ada, b_ada, g_norm1, w_in, lambda_q1, lambda_k1, lambda_q2, lambda_k2, g_subln,
           conv_w, conv_b, w_rg_a, b_rg_a, w_rg_i, b_rg_i, rg_lambda, g_rgnorm, w_o, g_norm2,
           w_router, router_bias, w_e_gate, w_e_up, w_e_down, w_s_gate, w_s_up, w_s_down, g_final):
    depth = w_ada.shape[0]
    assert depth == 1, "single-layer step"
    bp, seq, d = x_prompt.shape
    bs, n_new, _ = x_sample.shape
    n_heads = cache_k.shape[3]
    k_row = cache_k.shape[4]
    v_head = cache_v.shape[4]
    aw = n_heads * v_head
    rw = d - aw
    assert k_row == v_head and w_in.shape[2] == 3 * aw + 2 * rw and aw == rw
    qk_half = k_row // 2
    n_exp = w_router.shape[2]
    lam_init = 0.8 - 0.6 * math.exp(-0.3 * 0)
    tp, ts = bp * seq, bs * n_new
    t_all = tp + ts

    w_in_bf = w_in[0].astype(BF16)
    wo_bf = w_o[0].astype(BF16)
    wr_t = w_router[0].T
    wai_bf = jnp.concatenate([w_rg_a[0], w_rg_i[0]], axis=-1).astype(BF16)
    wsg_bf, wsu_bf, wsd_bf = w_s_gate[0].astype(BF16), w_s_up[0].astype(BF16), w_s_down[0].astype(BF16)
    lamv = jnp.stack([lambda_q1[0], lambda_k1[0], lambda_q2[0], lambda_k2[0]])
    slopes = jnp.exp2(-8.0 * jnp.arange(1, n_heads + 1, dtype=F32) / n_heads)

    mod = _ada(jnp.concatenate([c_prompt, c_sample], axis=0), w_ada[0], b_ada[0])
    mod_p = [mod[:bp, i * d:(i + 1) * d] for i in range(6)]
    mod_s = [mod[bp:, i * d:(i + 1) * d] for i in range(6)]

    xp2, xs2 = x_prompt.reshape(tp, d), x_sample.reshape(ts, d)
    qscale = qk_half ** -0.5 * LOG2E
    qp, kp, vp, up, gp, kpb, vpb = _inproj(xp2, mod_p[0], mod_p[1], g_norm1[0], w_in_bf, seq, qscale)
    qs, ks, vs, us, gs, _, _ = _inproj(xs2, mod_s[0], mod_s[1], g_norm1[0], w_in_bf, n_new, qscale)

    att_p = _pattn(qp, kpb, vpb, slopes, lamv, g_subln[0], bp, seq, n_heads, lam_init)

    q5 = qs.reshape(bs, n_new, n_heads, 2, qk_half).transpose(0, 2, 3, 1, 4)
    eye_h = jnp.eye(n_heads, dtype=BF16)
    eye_c = jnp.eye(2, dtype=BF16)
    wq = (q5[:, :, :, :, None, None, :] * eye_h[None, :, None, None, :, None, None]
          * eye_c[None, None, :, None, None, :, None]).reshape(bs, n_heads * 2 * n_new, aw)
    n_pool, page = cache_k.shape[1], cache_k.shape[2]
    att_s = _sattn(wq, ks.reshape(bs, n_new, aw), vs.reshape(bs, n_new, aw),
                   cache_k.reshape(n_pool, page * n_heads, k_row), cache_v.reshape(n_pool, page * n_heads, v_head),
                   page_table, lamv, g_subln[0], n_heads, lam_init).reshape(ts, aw)

    rg_args = (conv_w[0], conv_b[0], wai_bf, b_rg_a[0], b_rg_i[0], rg_lambda[0], g_rgnorm[0])
    rec_p, h_p, conv_p = _rglru(up, gp, jnp.zeros((bp, CONV_WIDTH - 1, rw), F32), jnp.zeros((bp, rw), F32),
                                *rg_args, bp, seq)
    rec_s, h_s, conv_s = _rglru(us, gs, state_conv[0], state_h[0], *rg_args, bs, n_new)

    x1p, h2p_all, st_all = _oproj(att_p, rec_p, xp2, mod_p[2], mod_p[3], mod_p[4], g_norm2[0], wo_bf, wr_t,
                                  seq, 0, t_all)
    x1s, h2p_all, st_all = _oproj(att_s, rec_s, xs2, mod_s[2], mod_s[3], mod_s[4], g_norm2[0], wo_bf, wr_t,
                                  n_new, tp, t_all, h2p_all, st_all)

    idx, wts, rank, counts = _route(st_all, router_bias[0])

    tm = EXPERT_TILE_ROWS
    counts = counts.reshape(n_exp)
    ptiles = (counts + tm - 1) // tm
    pend = jnp.cumsum(ptiles)
    pstart = pend - ptiles
    n_tiles = (t_all * TOP_K) // tm + n_exp
    tile_ids = jnp.arange(n_tiles, dtype=I32)
    tile_e = jnp.minimum(jnp.sum(pend[None, :] <= tile_ids[:, None], axis=1), n_exp - 1).astype(I32)
    tile_nv = jnp.where(tile_ids < pend[-1],
                        jnp.clip(counts[tile_e] - (tile_ids - pstart[tile_e]) * tm, 0, tm), 0).astype(I32)
    tile_ord = (jnp.cumsum(ptiles > 0) - 1)[tile_e].astype(I32)
    next_tile = pend[tile_e]
    tile_next = jnp.where(next_tile < pend[-1], tile_e[jnp.minimum(next_tile, n_tiles - 1)], -1).astype(I32)
    pos = _pos(idx, rank, pstart * tm)

    xs_rows = _dispatch(h2p_all, pos, tile_nv)
    ys = _experts(xs_rows, tile_e, tile_nv, tile_ord, tile_next, w_e_gate[0], w_e_up[0], w_e_down[0])

    wts_t = wts.T
    y_p = _combine(ys, pos[:, :tp], wts_t[:tp], h2p_all, x1p, mod_p[5], g_final, wsg_bf, wsu_bf, wsd_bf, seq, 0)
    y_s = _combine(ys, pos[:, tp:], wts_t[tp:], h2p_all, x1s, mod_s[5], g_final, wsg_bf, wsu_bf, wsd_bf, n_new, tp)

    return (y_p.reshape(bp, seq, d), y_s.reshape(bs, n_new, d),
            kp.reshape(1, bp, seq, n_heads, k_row), vp.reshape(1, bp, seq, n_heads, v_head),
            h_p.reshape(1, bp, rw), conv_p.reshape(1, bp, CONV_WIDTH - 1, rw),
            ks.reshape(1, bs, n_new, n_heads, k_row), vs.reshape(1, bs, n_new, n_heads, v_head),
            h_s.reshape(1, bs, rw), conv_s.reshape(1, bs, CONV_WIDTH - 1, rw))
```

```python
import functools
import math

import jax
import jax.numpy as jnp
import numpy as np
from jax import lax
from jax.experimental import pallas as pl
from jax.experimental.pallas import tpu as pltpu
from jax.experimental.pallas import tpu_sc as plsc

F32 = jnp.float32
BF16 = jnp.bfloat16
I32 = jnp.int32
U32 = jnp.uint32

NORM_EPS = 1e-6
SUBLN_EPS = 1e-5
NEG = -1e30
RG_C = 8.0
ROUTED_SCALE = 2.5
N_GROUPS = 8
TOPK_GROUPS = 4
TOP_K = 8
CONV_WIDTH = 4

V7X_VMEM_LIMIT_BYTES = 56 * 1024 * 1024
EXPERT_TILE_ROWS = 128
EXPERT_WEIGHT_DMA_CHUNKS = 8
V7X_SC_CORES = 2
V7X_SC_SUBCORES = 16
COMBINE_TILE = 128
SC_CHUNK_ROWS = 64


def _cparams(n_axes):
    return pltpu.CompilerParams(
        dimension_semantics=("arbitrary",) * n_axes, vmem_limit_bytes=V7X_VMEM_LIMIT_BYTES
    )


def _pick(n, candidates):
    for c in candidates:
        if n % c == 0:
            return c
    return n


def _dot(a, b):
    return jnp.dot(a, b, preferred_element_type=F32)


def _dot_nt(a, b):
    return lax.dot_general(a, b, (((1,), (1,)), ((), ())), preferred_element_type=F32)


def _split(x):
    hi = x.astype(BF16)
    lo = (x - hi.astype(F32)).astype(BF16)
    return hi, lo


def _rms(x, eps):
    return x * lax.rsqrt(jnp.mean(x * x, axis=-1, keepdims=True) + eps)


def _silu(x):
    return x * jax.nn.sigmoid(x)


def _bf16_terms(x, n):
    terms = []
    for _ in range(n):
        bits = np.float32(x).view(np.uint32)
        bits = (bits + np.uint32(0x7FFF) + ((bits >> np.uint32(16)) & np.uint32(1))) & np.uint32(0xFFFF0000)
        t = float(bits.view(np.float32))
        terms.append(t)
        x -= t
    return tuple(terms)


LOG2E = math.log2(math.e)
LOG2E_BF16_TERMS = _bf16_terms(LOG2E, 3)
TILE_SUBLANES = 8


def _rows_from_tiles(ref, n_rows):
    return jnp.concatenate(
        [ref[pl.ds(sub, n_rows, stride=TILE_SUBLANES), :] for sub in range(TILE_SUBLANES)], axis=-1)


def _rows_to_tiles(ref, x):
    n_rows, width = x.shape
    lanes = width // TILE_SUBLANES
    for sub in range(TILE_SUBLANES):
        ref[pl.ds(sub, n_rows, stride=TILE_SUBLANES), :] = x[:, sub * lanes:(sub + 1) * lanes]


def _unpack_halves(xp):
    lo = pltpu.unpack_elementwise(xp, index=0, packed_dtype=BF16, unpacked_dtype=F32)
    hi = pltpu.unpack_elementwise(xp, index=1, packed_dtype=BF16, unpacked_dtype=F32)
    return lo.astype(BF16), hi.astype(BF16)


def _pack_halves(x):
    n = x.shape[-1] // 2
    return pltpu.pack_elementwise([x[:, :n], x[:, n:]], packed_dtype=BF16)


def _ada_kernel(c_ref, w_ref, b_ref, o_ref):
    a_hi, a_lo = _split(_silu(c_ref[...]))
    w_hi, w_lo = _split(w_ref[...])
    o_ref[...] = _dot(a_hi, w_hi) + (_dot(a_hi, w_lo) + _dot(a_lo, w_hi)) + b_ref[...]


def _ada(c, w, b):
    n, d = c.shape
    d_out = w.shape[1]
    tn = _pick(d_out, (512, 256, 128))
    return pl.pallas_call(
        _ada_kernel,
        grid=(d_out // tn,),
        in_specs=[
            pl.BlockSpec((n, d), lambda j: (0, 0)),
            pl.BlockSpec((d, tn), lambda j: (0, j)),
            pl.BlockSpec((1, tn), lambda j: (0, j)),
        ],
        out_specs=pl.BlockSpec((n, tn), lambda j: (0, j)),
        out_shape=jax.ShapeDtypeStruct((n, d_out), F32),
        compiler_params=_cparams(1),
        name="ada",
    )(c, w, b.reshape(1, d_out))


def _mod_spec(rows_per_batch, tm, d, n_tiles=None):
    clamp = (lambda i: i) if n_tiles is None else (lambda i: jnp.minimum(i, n_tiles - 1))
    if rows_per_batch % tm == 0:
        per = rows_per_batch // tm
        return pl.BlockSpec((None, 1, d), lambda i, *_: (clamp(i) // per, 0, 0))
    return pl.BlockSpec((None, tm, d), lambda i, *_: (clamp(i), 0, 0))


def _mod_array(m, rows_per_batch, tm):
    nb, d = m.shape
    if rows_per_batch % tm == 0:
        return m.reshape(nb, 1, d)
    assert tm % rows_per_batch == 0
    return jnp.repeat(m, rows_per_batch, axis=0).reshape(nb * rows_per_batch // tm, tm, d)


def _inproj_kernel(x_ref, sh_ref, sc_ref, g_ref, w_ref,
                   q_ref, k_ref, v_ref, u_ref, gt_ref, kb_ref, vb_ref, h_scr, *, qscale):
    j = pl.program_id(1)

    @pl.when(j == 0)
    def _():
        y = _rms(x_ref[...], NORM_EPS) * g_ref[...]
        h_scr[...] = (y * (1.0 + sc_ref[...]) + sh_ref[...]).astype(BF16)

    z = _dot(h_scr[...], w_ref[...])

    @pl.when(j == 0)
    def _():
        q_ref[...] = (z * qscale).astype(BF16)

    @pl.when(j == 1)
    def _():
        k_ref[...] = z
        kb_ref[...] = z.astype(BF16)

    @pl.when(j == 2)
    def _():
        v_ref[...] = z
        vb_ref[...] = z.astype(BF16)

    @pl.when(j == 3)
    def _():
        u_ref[...] = z

    @pl.when(j == 4)
    def _():
        gt_ref[...] = z


def _inproj(x2d, shift, scale, g, w_bf, rows_per_batch, qscale):
    r, d = x2d.shape
    wd = w_bf.shape[1] // 5
    tm = _pick(r, (512, 256, 128, 64, 32, 16, 8))
    row = lambda i, j: (i, 0)
    f32o = jax.ShapeDtypeStruct((r, wd), F32)
    bfo = jax.ShapeDtypeStruct((r, wd), BF16)
    return pl.pallas_call(
        functools.partial(_inproj_kernel, qscale=qscale),
        grid=(r // tm, 5),
        in_specs=[
            pl.BlockSpec((tm, d), row),
            _mod_spec(rows_per_batch, tm, d),
            _mod_spec(rows_per_batch, tm, d),
            pl.BlockSpec((1, d), lambda i, j: (0, 0)),
            pl.BlockSpec((d, wd), lambda i, j: (0, j)),
        ],
        out_specs=[pl.BlockSpec((tm, wd), row)] * 7,
        out_shape=[bfo, f32o, f32o, f32o, f32o, bfo, bfo],
        scratch_shapes=[pltpu.VMEM((tm, d), BF16)],
        compiler_params=_cparams(2),
        name="inproj",
    )(x2d, _mod_array(shift, rows_per_batch, tm), _mod_array(scale, rows_per_batch, tm), g.reshape(1, d), w_bf)


def _lam(lamv_ref, lam_init):
    lv = lamv_ref[...]
    s1 = jnp.sum(lv[0:1] * lv[1:2], axis=-1, keepdims=True)
    s2 = jnp.sum(lv[2:3] * lv[3:4], axis=-1, keepdims=True)
    return jnp.exp(s1) - jnp.exp(s2) + lam_init


def _online_update(s, v, m_scr, l_scr, acc_scr, shift=None):
    m_old = m_scr[...]
    s_max = jnp.max(s, axis=-1, keepdims=True)
    m_new = jnp.maximum(m_old, s_max if shift is None else s_max + shift)
    alpha = jnp.exp2(m_old - m_new)
    p = jnp.exp2(s - (m_new if shift is None else m_new - shift))
    l_scr[...] = alpha * l_scr[...] + jnp.sum(p, axis=-1, keepdims=True)
    acc_scr[...] = alpha * acc_scr[...] + _dot(p.astype(BF16), v)
    m_scr[...] = m_new


def _pattn_kernel(slopes_ref, lamv_ref, gs_ref, q_ref, k_ref, v_ref, o_ref, q_scr, ka_scr, *state,
                  tq, rc, lam_init):
    n_chunks = 2 * tq // rc
    m_scrs, l_scrs, acc_scrs = state[:n_chunks], state[n_chunks:2 * n_chunks], state[2 * n_chunks:]
    h = pl.program_id(1)
    qi = pl.program_id(2)
    slope = slopes_ref[h]
    hd = q_ref.shape[1]
    half = hd // 2
    n_terms = len(LOG2E_BF16_TERMS)

    q = q_ref[...]
    lane = lax.broadcasted_iota(I32, q.shape, 1)
    zero = jnp.zeros_like(q)
    qa = jnp.zeros(q.shape, F32)
    for n, term in enumerate(LOG2E_BF16_TERMS):
        qa = jnp.where(jnp.logical_or(lane == n, lane == n + n_terms), term, qa)
    qa = qa.astype(BF16)
    q_scr[0:tq, 0:hd] = jnp.where(lane < half, q, zero)
    q_scr[tq:2 * tq, 0:hd] = jnp.where(lane >= half, q, zero)
    q_scr[0:tq, hd:2 * hd] = qa
    q_scr[tq:2 * tq, hd:2 * hd] = qa
    c = lax.broadcasted_iota(I32, (tq, hd), 0)
    c_lo = c % 256
    ka = jnp.where(lane < n_terms, c_lo.astype(F32) * slope,
                   jnp.where(lane < 2 * n_terms, (c - c_lo).astype(F32) * slope, 0.0))
    ka_scr[...] = ka.astype(BF16)

    for m_scr, l_scr, acc_scr in zip(m_scrs, l_scrs, acc_scrs):
        m_scr[...] = jnp.full(m_scr.shape, NEG, F32)
        l_scr[...] = jnp.zeros(l_scr.shape, F32)
        acc_scr[...] = jnp.zeros(acc_scr.shape, F32)

    def step(j, masked):
        start = pl.multiple_of(j * tq, tq)
        kaug = jnp.concatenate([k_ref[pl.ds(start, tq), :], ka_scr[...]], axis=-1)
        v = v_ref[pl.ds(start, tq), :]
        for ci in range(n_chunks):
            r0 = ci * rc
            q0 = r0 % tq
            ncol = q0 + rc if masked else tq
            rowpos = q0 + lax.broadcasted_iota(I32, (rc, 1), 0)
            s = _dot_nt(q_scr[r0:r0 + rc, :], kaug[:ncol])
            if masked:
                s = jnp.where(lax.broadcasted_iota(I32, (rc, ncol), 1) <= rowpos, s, NEG)
            shift = ((j - qi) * tq - rowpos).astype(F32) * (slope * LOG2E)
            _online_update(s, v[:ncol], m_scrs[ci], l_scrs[ci], acc_scrs[ci], shift=shift)

    def body(j, carry):
        step(j, False)
        return carry

    lax.fori_loop(0, qi, body, 0)
    step(qi, True)

    lam = _lam(lamv_ref, lam_init)
    o = jnp.concatenate([acc[...] / l[...] for acc, l in zip(acc_scrs, l_scrs)], axis=0)
    att = o[:tq] - lam * o[tq:]
    att = _rms(att, SUBLN_EPS) * gs_ref[...] * (1.0 - lam_init)
    o_ref[...] = att.astype(BF16)


def _pattn(qb, kb, vb, slopes, lamv, g_subln, n_batch, seq, n_heads, lam_init):
    r, aw = qb.shape
    hd = aw // n_heads
    tq = _pick(seq, (512, 256, 128))
    rc = min(tq, 256)
    n_chunks = 2 * tq // rc
    nq = seq // tq
    return pl.pallas_call(
        functools.partial(_pattn_kernel, tq=tq, rc=rc, lam_init=lam_init),
        grid_spec=pltpu.PrefetchScalarGridSpec(
            num_scalar_prefetch=1,
            grid=(n_batch, n_heads, nq),
            in_specs=[
                pl.BlockSpec(lamv.shape, lambda b, h, i, *_: (0, 0)),
                pl.BlockSpec((1, hd), lambda b, h, i, *_: (0, 0)),
                pl.BlockSpec((tq, hd), lambda b, h, i, *_: (b * nq + i, h)),
                pl.BlockSpec((seq, hd), lambda b, h, i, *_: (b, h)),
                pl.BlockSpec((seq, hd), lambda b, h, i, *_: (b, h)),
            ],
            out_specs=pl.BlockSpec((tq, hd), lambda b, h, i, *_: (b * nq + i, h)),
            scratch_shapes=[pltpu.VMEM((2 * tq, 2 * hd), BF16), pltpu.VMEM((tq, hd), BF16)]
            + [pltpu.VMEM((rc, 1), F32)] * (2 * n_chunks) + [pltpu.VMEM((rc, hd), F32)] * n_chunks,
        ),
        out_shape=jax.ShapeDtypeStruct((r, aw), BF16),
        compiler_params=_cparams(3),
        name="pattn",
    )(slopes, lamv, g_subln.reshape(1, hd), qb, kb, vb)


def _sattn_kernel(pt_ref, lamv_ref, gs_ref, wq_ref, kn_ref, vn_ref, *rest,
                  n_pages_step, page, past, n_new, n_heads, lam_init):
    k_refs = rest[:n_pages_step]
    v_refs = rest[n_pages_step:2 * n_pages_step]
    o_ref = rest[2 * n_pages_step]
    m_scr, l_scr, acc_scr = rest[2 * n_pages_step + 1:]
    j = pl.program_id(1)
    n_rows = wq_ref.shape[0]
    hd = wq_ref.shape[1] // n_heads
    tk = n_pages_step * page

    r = lax.broadcasted_iota(I32, (n_rows, 1), 0)
    head = r // (2 * n_new)
    qi = r % n_new
    slope = jnp.exp2(-(head + 1).astype(F32)) * LOG2E

    @pl.when(j == 0)
    def _():
        m_scr[...] = jnp.full(m_scr.shape, NEG, F32)
        l_scr[...] = jnp.zeros(l_scr.shape, F32)
        acc_scr[...] = jnp.zeros(acc_scr.shape, F32)

    wq = wq_ref[...]
    kc = jnp.concatenate([_rows_from_tiles(kr, page).astype(BF16) for kr in k_refs], axis=0)
    vc = jnp.concatenate([_rows_from_tiles(vr, page).astype(BF16) for vr in v_refs], axis=0)
    t = j * tk + lax.broadcasted_iota(I32, (1, tk), 1)
    s = _dot_nt(wq, kc) - slope * (past + qi - t).astype(F32)
    _online_update(s, vc, m_scr, l_scr, acc_scr)

    @pl.when(j == pl.num_programs(1) - 1)
    def _():
        pad = jnp.zeros((page - n_new, kn_ref.shape[1]), BF16)
        kn = jnp.concatenate([kn_ref[...].astype(BF16), pad], axis=0)
        vn = jnp.concatenate([vn_ref[...].astype(BF16), pad], axis=0)
        tj = lax.broadcasted_iota(I32, (1, page), 1)
        sn = _dot_nt(wq, kn) - slope * (qi - tj).astype(F32)
        sn = jnp.where(tj <= qi, sn, NEG)
        _online_update(sn, vn, m_scr, l_scr, acc_scr)

        lam = _lam(lamv_ref, lam_init)
        o = acc_scr[...] / l_scr[...]
        outs = []
        for h in range(n_heads):
            blk = o[h * 2 * n_new:(h + 1) * 2 * n_new, h * hd:(h + 1) * hd]
            att = blk[:n_new] - lam * blk[n_new:]
            outs.append(_rms(att, SUBLN_EPS) * gs_ref[...] * (1.0 - lam_init))
        o_ref[...] = jnp.concatenate(outs, axis=-1).astype(BF16)


def _sattn(wq, k_new, v_new, cache_k2, cache_v2, page_table, lamv, g_subln, n_heads, lam_init):
    n_seq, n_rows, aw = wq.shape
    n_new = k_new.shape[1]
    hd = aw // n_heads
    page = cache_k2.shape[1] // n_heads
    n_pages = page_table.shape[1]
    pstep = _pick(n_pages, (8, 4, 2, 1))
    past = n_pages * page
    assert n_heads == TILE_SUBLANES and hd == 128

    def page_spec(p):
        return pl.BlockSpec((None, page * n_heads, hd), lambda b, j, pt: (pt[b, j * pstep + p], 0, 0))

    seq_spec = lambda rows: pl.BlockSpec((None, rows, aw), lambda b, j, pt: (b, 0, 0))
    return pl.pallas_call(
        functools.partial(_sattn_kernel, n_pages_step=pstep, page=page, past=past, n_new=n_new,
                          n_heads=n_heads, lam_init=lam_init),
        grid_spec=pltpu.PrefetchScalarGridSpec(
            num_scalar_prefetch=1,
            grid=(n_seq, n_pages // pstep),
            in_specs=[
                pl.BlockSpec(lamv.shape, lambda b, j, pt: (0, 0)),
                pl.BlockSpec((1, hd), lambda b, j, pt: (0, 0)),
                seq_spec(n_rows), seq_spec(n_new), seq_spec(n_new),
            ] + [page_spec(p) for p in range(pstep)] * 2,
            out_specs=seq_spec(n_new),
            scratch_shapes=[
                pltpu.VMEM((n_rows, 1), F32),
                pltpu.VMEM((n_rows, 1), F32),
                pltpu.VMEM((n_rows, aw), F32),
            ],
        ),
        out_shape=jax.ShapeDtypeStruct((n_seq, n_new, aw), BF16),
        compiler_params=_cparams(2),
        name="sattn",
    )(page_table, lamv, g_subln.reshape(1, hd), wq, k_new, v_new,
      *([cache_k2] * pstep), *([cache_v2] * pstep))


def _gelu_tanh(x):
    return x * (0.5 * (1.0 + jnp.tanh(math.sqrt(2.0 / math.pi) * (x + 0.044715 * (x * x * x)))))


def _softplus(x):
    return jnp.maximum(x, 0.0) + jnp.log1p(jnp.exp(-jnp.abs(x)))


def _rglru_kernel(u_ref, gt_ref, c0_ref, h0_ref, cw_ref, cb_ref, wai_ref, ba_ref, bi_ref, lam_ref, gn_ref,
                  rec_ref, ht_ref, cout_ref, ubuf, hcar, *, tl, n_blocks):
    t = pl.program_id(1)
    halo = CONV_WIDTH - 1
    base = 8

    @pl.when(t == 0)
    def _():
        ubuf[base - halo:base, :] = c0_ref[...]
        hcar[...] = h0_ref[...]

    ubuf[base:base + tl, :] = u_ref[...]
    cw = cw_ref[...]
    xc = cb_ref[...] + cw[0:1] * ubuf[base - halo:base - halo + tl, :]
    for jj in range(1, CONV_WIDTH):
        xc = xc + cw[jj:jj + 1] * ubuf[base - halo + jj:base - halo + jj + tl, :]
    tail = ubuf[base + tl - halo:base + tl, :]
    ubuf[base - halo:base, :] = tail
    cout_ref[...] = tail

    bw = xc.shape[1] // n_blocks
    za, zi = [], []
    for n in range(n_blocks):
        z = _dot(xc[:, n * bw:(n + 1) * bw].astype(BF16), wai_ref[n])
        za.append(z[:, :bw])
        zi.append(z[:, bw:])
    r = jax.nn.sigmoid(jnp.concatenate(za, axis=-1) + ba_ref[...])
    i = jax.nn.sigmoid(jnp.concatenate(zi, axis=-1) + bi_ref[...])
    log_a = -RG_C * r * _softplus(-lam_ref[...])
    a = jnp.exp(log_a)
    th = jnp.tanh(log_a)
    b = xc * i * jnp.sqrt(-2.0 * th / (1.0 - th))

    rowi = lax.broadcasted_iota(I32, a.shape, 0)
    sft = 1
    while sft < tl:
        keep = rowi >= sft
        a_prev = jnp.where(keep, pltpu.roll(a, sft, 0), 1.0)
        b_prev = jnp.where(keep, pltpu.roll(b, sft, 0), 0.0)
        b = a * b_prev + b
        a = a * a_prev
        sft *= 2
    hs = a * hcar[...] + b
    h_last = hs[tl - 1:tl, :]
    hcar[...] = h_last
    ht_ref[...] = h_last

    rec = hs * _gelu_tanh(gt_ref[...])
    rec_ref[...] = (_rms(rec, NORM_EPS) * gn_ref[...]).astype(BF16)


def _rglru(u2d, gate2d, conv0, h0, conv_w, conv_b, wai_bf, b_a, b_i, rg_lambda, g_rgnorm, n_batch, seq):
    r, w = u2d.shape
    tl = _pick(seq, (256, 128, 64, 32, 16, 8))
    nt = seq // tl
    n_blocks = wai_bf.shape[0]
    halo = CONV_WIDTH - 1
    row = lambda b, t: (b * nt + t, 0)
    vec = pl.BlockSpec((1, w), lambda b, t: (0, 0))
    rec, ht, cout = pl.pallas_call(
        functools.partial(_rglru_kernel, tl=tl, n_blocks=n_blocks),
        grid=(n_batch, nt),
        in_specs=[
            pl.BlockSpec((tl, w), row),
            pl.BlockSpec((tl, w), row),
            pl.BlockSpec((None, halo, w), lambda b, t: (b, 0, 0)),
            pl.BlockSpec((None, 1, w), lambda b, t: (b, 0, 0)),
            pl.BlockSpec((CONV_WIDTH, w), lambda b, t: (0, 0)),
            vec,
            pl.BlockSpec(wai_bf.shape, lambda b, t: (0, 0, 0)),
            vec, vec, vec, vec,
        ],
        out_specs=[
            pl.BlockSpec((tl, w), row),
            pl.BlockSpec((None, 1, w), lambda b, t: (b, 0, 0)),
            pl.BlockSpec((None, halo, w), lambda b, t: (b, 0, 0)),
        ],
        out_shape=[
            jax.ShapeDtypeStruct((r, w), BF16),
            jax.ShapeDtypeStruct((n_batch, 1, w), F32),
            jax.ShapeDtypeStruct((n_batch, halo, w), F32),
        ],
        scratch_shapes=[pltpu.VMEM((tl + 8, w), F32), pltpu.VMEM((1, w), F32)],
        compiler_params=_cparams(2),
        name="rglru",
    )(u2d, gate2d, conv0, h0.reshape(n_batch, 1, w), conv_w, conv_b.reshape(1, w), wai_bf,
      b_a.reshape(1, w), b_i.reshape(1, w), rg_lambda.reshape(1, w), g_rgnorm.reshape(1, w))
    return rec, ht.reshape(n_batch, w), cout


def _oproj_kernel(att_ref, rec_ref, x_ref, g1_ref, sh_ref, sc_ref, gn_ref, wo_ref, wrt_ref, *rest,
                  aliased, n_tiles):
    x1_ref, h2p_ref, st_ref = rest[2:] if aliased else rest
    aw = att_ref.shape[1]
    i = pl.program_id(0)

    @pl.when(i < n_tiles)
    def _():
        mix = _dot(att_ref[...], wo_ref[:aw, :]) + _dot(rec_ref[...], wo_ref[aw:, :])
        x1 = x_ref[...] + g1_ref[...] * mix
        x1_ref[...] = x1
        h2 = (_rms(x1, NORM_EPS) * gn_ref[...]) * (1.0 + sc_ref[...]) + sh_ref[...]
        _rows_to_tiles(h2p_ref, _pack_halves(h2))
        h_hi, h_lo = _split(h2)
        w_hi, w_lo = _split(wrt_ref[...])
        logits_t = _dot_nt(w_hi, h_hi) + (_dot_nt(w_hi, h_lo) + _dot_nt(w_lo, h_hi))
        st_ref[...] = jax.nn.sigmoid(logits_t)

    @pl.when(i >= n_tiles)
    def _():
        h2p_ref[...] = jnp.zeros(h2p_ref.shape, U32)
        st_ref[...] = jnp.zeros(st_ref.shape, F32)


def _oproj(att, rec, x2d, g1, shift, scale, g_norm2, wo_bf, wr_t, rows_per_batch, row_offset, total_rows,
           h2p_all=None, st_all=None):
    r, d = x2d.shape
    aw = att.shape[1]
    n_exp = wr_t.shape[0]
    tm = _pick(r, (256, 128))
    assert row_offset % tm == 0
    off = row_offset // tm
    lanes = d // 2 // TILE_SUBLANES
    assert lanes == 128
    aliased = h2p_all is not None
    n_tiles = r // tm
    n_fill = 0 if aliased else (total_rows - r) // tm
    assert aliased or (row_offset == 0 and (total_rows - r) % tm == 0)
    row = lambda i: (jnp.minimum(i, n_tiles - 1), 0)
    mspec = _mod_spec(rows_per_batch, tm, d, n_tiles)
    in_specs = [
        pl.BlockSpec((tm, aw), row),
        pl.BlockSpec((tm, d - aw), row),
        pl.BlockSpec((tm, d), row),
        mspec, mspec, mspec,
        pl.BlockSpec((1, d), lambda i: (0, 0)),
        pl.BlockSpec((d, d), lambda i: (0, 0)),
        pl.BlockSpec((n_exp, d), lambda i: (0, 0)),
    ]
    args = [att, rec, x2d, _mod_array(g1, rows_per_batch, tm), _mod_array(shift, rows_per_batch, tm),
            _mod_array(scale, rows_per_batch, tm), g_norm2.reshape(1, d), wo_bf, wr_t]
    io_alias = {}
    if aliased:
        in_specs += [pl.BlockSpec(memory_space=pl.ANY), pl.BlockSpec(memory_space=pl.ANY)]
        io_alias = {len(args): 1, len(args) + 1: 2}
        args += [h2p_all, st_all]
    return pl.pallas_call(
        functools.partial(_oproj_kernel, aliased=aliased, n_tiles=n_tiles),
        grid=(n_tiles + n_fill,),
        in_specs=in_specs,
        out_specs=[
            pl.BlockSpec((tm, d), row),
            pl.BlockSpec((tm * TILE_SUBLANES, lanes), lambda i: (i + off, 0)),
            pl.BlockSpec((n_exp, tm), lambda i: (0, i + off)),
        ],
        out_shape=[
            jax.ShapeDtypeStruct((r, d), F32),
            jax.ShapeDtypeStruct((total_rows * TILE_SUBLANES, lanes), U32),
            jax.ShapeDtypeStruct((n_exp, total_rows), F32),
        ],
        input_output_aliases=io_alias,
        compiler_params=_cparams(1),
        name="oproj",
    )(*args)


def _route_kernel(st_ref, rb_ref, idx_ref, wts_ref, rank_ref, cnt_ref, carry):
    i = pl.program_id(0)
    n_exp, tr = st_ref.shape
    gsz = n_exp // N_GROUPS

    @pl.when(i == 0)
    def _():
        carry[...] = jnp.zeros(carry.shape, F32)

    s = st_ref[...]
    biased = s + rb_ref[...]
    g = biased.reshape(N_GROUPS, gsz, tr)
    within = lax.broadcasted_iota(I32, g.shape, 1)
    m1 = jnp.max(g, axis=1, keepdims=True)
    first = jnp.min(jnp.where(g == m1, within, gsz), axis=1, keepdims=True)
    m2 = jnp.max(jnp.where(within == first, -jnp.inf, g), axis=1, keepdims=True)
    gscore = (m1 + m2).reshape(N_GROUPS, tr)

    gidx = lax.broadcasted_iota(I32, gscore.shape, 0)
    gsel = jnp.zeros(gscore.shape, F32)
    for _ in range(TOPK_GROUPS):
        mg = jnp.max(gscore, axis=0, keepdims=True)
        fg = jnp.min(jnp.where(gscore == mg, gidx, N_GROUPS), axis=0, keepdims=True)
        hit = gidx == fg
        gsel = jnp.where(hit, 1.0, gsel)
        gscore = jnp.where(hit, -jnp.inf, gscore)
    masked = jnp.where(gsel.reshape(N_GROUPS, 1, tr) > 0.5, g, -jnp.inf).reshape(n_exp, tr)

    eidx = lax.broadcasted_iota(I32, (n_exp, tr), 0)
    idxs, ws = [], []
    chosen = jnp.zeros((n_exp, tr), jnp.bool_)
    for _ in range(TOP_K):
        mv = jnp.max(masked, axis=0, keepdims=True)
        fe = jnp.min(jnp.where(masked == mv, eidx, n_exp), axis=0, keepdims=True)
        hit = eidx == fe
        idxs.append(fe)
        ws.append(jnp.sum(jnp.where(hit, s, 0.0), axis=0, keepdims=True))
        chosen = chosen | hit
        masked = jnp.where(hit, -jnp.inf, masked)
    idx = jnp.concatenate(idxs, axis=0)
    w = jnp.concatenate(ws, axis=0)
    idx_ref[...] = idx
    wts_ref[...] = w / jnp.sum(w, axis=0, keepdims=True) * ROUTED_SCALE

    cmat = jnp.where(chosen, 1.0, 0.0)
    before = lax.broadcasted_iota(I32, (tr, tr), 0) < lax.broadcasted_iota(I32, (tr, tr), 1)
    prior = _dot(cmat.astype(BF16), jnp.where(before, 1.0, 0.0).astype(BF16)) + carry[...]
    ranks = [jnp.sum(jnp.where(eidx == idxs[k], prior, 0.0), axis=0, keepdims=True) for k in range(TOP_K)]
    rank_ref[...] = jnp.concatenate(ranks, axis=0).astype(I32)
    carry[...] = carry[...] + jnp.sum(cmat, axis=1, keepdims=True)
    cnt_ref[...] = carry[...].astype(I32)


def _route(st_all, router_bias):
    n_exp, t_all = st_all.shape
    tr = _pick(t_all, (640, 512, 256, 128))
    col = lambda i: (0, i)
    o8 = lambda dt: jax.ShapeDtypeStruct((TOP_K, t_all), dt)
    return pl.pallas_call(
        _route_kernel,
        grid=(t_all // tr,),
        in_specs=[pl.BlockSpec((n_exp, tr), col), pl.BlockSpec((n_exp, 1), lambda i: (0, 0))],
        out_specs=[pl.BlockSpec((TOP_K, tr), col)] * 3 + [pl.BlockSpec((n_exp, 1), lambda i: (0, 0))],
        out_shape=[o8(I32), o8(F32), o8(I32), jax.ShapeDtypeStruct((n_exp, 1), I32)],
        scratch_shapes=[pltpu.VMEM((n_exp, 1), F32)],
        compiler_params=_cparams(1),
        name="route",
    )(st_all, router_bias.reshape(n_exp, 1))


def _pos_kernel(idx_ref, rank_ref, start_ref, pos_ref):
    n_exp = start_ref.shape[0]
    tr = idx_ref.shape[1]
    eidx = lax.broadcasted_iota(I32, (n_exp, tr), 0)
    start = start_ref[...]
    rows = [jnp.sum(jnp.where(eidx == idx_ref[k:k + 1, :], start, 0.0), axis=0, keepdims=True)
            for k in range(TOP_K)]
    pos_ref[...] = jnp.concatenate(rows, axis=0).astype(I32) + rank_ref[...]


def _pos(idx, rank, start_rows):
    n_exp = start_rows.shape[0]
    t_all = idx.shape[1]
    tr = _pick(t_all, (640, 512, 256, 128))
    col = pl.BlockSpec((TOP_K, tr), lambda i: (0, i))
    return pl.pallas_call(
        _pos_kernel,
        grid=(t_all // tr,),
        in_specs=[col, col, pl.BlockSpec((n_exp, 1), lambda i: (0, 0))],
        out_specs=col,
        out_shape=jax.ShapeDtypeStruct((TOP_K, t_all), I32),
        compiler_params=_cparams(1),
        name="pos",
    )(idx, rank, start_rows.astype(F32).reshape(n_exp, 1))


def _tile_pos(pos, tile):
    k, t = pos.shape
    return pos.reshape(k, t // tile, tile).transpose(1, 0, 2)


def _sc_worker_chunks(n_chunks, fn):
    n_workers = V7X_SC_CORES * V7X_SC_SUBCORES
    worker = lax.axis_index("core") * V7X_SC_SUBCORES + lax.axis_index("subcore")

    @pl.loop(0, -(-n_chunks // n_workers))
    def _(it):
        chunk = it * n_workers + worker

        @pl.when(chunk < n_chunks)
        def _():
            fn(chunk)


def _sc_mesh():
    return plsc.VectorSubcoreMesh(core_axis_name="core", subcore_axis_name="subcore",
                                  num_cores=V7X_SC_CORES, num_subcores=V7X_SC_SUBCORES)


def _sc_scatter_rows(x3, idx3, n_rows):
    n_chunks, n_k, width = idx3.shape
    row = x3.shape[1:]

    @pl.kernel(out_type=jax.ShapeDtypeStruct((n_rows,) + row, x3.dtype), mesh=_sc_mesh(),
               scratch_types=[pltpu.VMEM((width,) + row, x3.dtype), pltpu.VMEM((n_k, width), I32),
                              pltpu.SemaphoreType.DMA],
               name="dispatch_sc")
    def scatter(x_hbm, i_hbm, o_hbm, xbuf, ibuf, sem):
        def one(chunk):
            pltpu.sync_copy(x_hbm.at[pl.ds(chunk * width, width)], xbuf)
            pltpu.sync_copy(i_hbm.at[chunk], ibuf)
            copies = [pltpu.make_async_copy(xbuf, o_hbm.at[ibuf.at[k]], sem) for k in range(n_k)]
            for cp in copies:
                cp.start()
            for cp in copies:
                cp.wait()

        _sc_worker_chunks(n_chunks, one)

    return scatter(x3, idx3)


def _sc_gather_rows(src3, idx2):
    n_chunks, width = idx2.shape
    row = src3.shape[1:]

    @pl.kernel(out_type=jax.ShapeDtypeStruct((n_chunks * width,) + row, src3.dtype), mesh=_sc_mesh(),
               scratch_types=[pltpu.VMEM((width,) + row, src3.dtype), pltpu.VMEM((width,), I32),
                              pltpu.SemaphoreType.DMA],
               name="combine_sc")
    def gather(s_hbm, i_hbm, o_hbm, buf, ibuf, sem):
        def one(chunk):
            pltpu.sync_copy(i_hbm.at[chunk], ibuf)
            pltpu.async_copy(s_hbm.at[ibuf], buf, sem).wait()
            pltpu.sync_copy(buf, o_hbm.at[pl.ds(chunk * width, width)])

        _sc_worker_chunks(n_chunks, one)

    return gather(src3, idx2)


def _swiglu_packed(xp, wg, wu, wd):
    xa, xb = _unpack_halves(xp)
    half = xp.shape[1]
    g = _dot(xa, wg[:half, :]) + _dot(xb, wg[half:, :])
    u = _dot(xa, wu[:half, :]) + _dot(xb, wu[half:, :])
    return _dot((_silu(g) * u).astype(BF16), wd[...])


def _experts_kernel(te_ref, tnv_ref, tord_ref, tnext_ref, xs_ref, wg_hbm, wu_hbm, wd_hbm, ys_ref,
                    wg_f, wu_f, wd_f, wg_b, wu_b, wd_b, sems):
    i = pl.program_id(0)
    nv = tnv_ref[i]
    expert = te_ref[i]
    new_expert = jnp.logical_or(i == 0, expert != te_ref[jnp.maximum(i - 1, 0)])
    slot = tord_ref[i] % 2
    tm = ys_ref.shape[0] // TILE_SUBLANES

    def weight_copies(e, sl):
        copies = []
        for n, (hbm, buf) in enumerate(((wg_hbm, wg_f), (wu_hbm, wu_f), (wd_hbm, wd_f))):
            rows = hbm.shape[1] // EXPERT_WEIGHT_DMA_CHUNKS
            for c in range(EXPERT_WEIGHT_DMA_CHUNKS):
                span = pl.ds(c * rows, rows)
                copies.append(pltpu.make_async_copy(hbm.at[e, span], buf.at[sl, span], sems.at[n, sl]))
        return copies

    @pl.when(jnp.logical_and(nv > 0, new_expert))
    def _():
        @pl.when(i == 0)
        def _():
            for cp in weight_copies(expert, slot):
                cp.start()

        for cp in weight_copies(expert, slot):
            cp.wait()
        nxt = tnext_ref[i]

        @pl.when(nxt >= 0)
        def _():
            for cp in weight_copies(nxt, 1 - slot):
                cp.start()

        wg_b[...] = wg_f[slot].astype(BF16)
        wu_b[...] = wu_f[slot].astype(BF16)
        wd_b[...] = wd_f[slot].astype(BF16)

    @pl.when(nv > 0)
    def _():
        xp = _rows_from_tiles(xs_ref, tm)
        rowi = lax.broadcasted_iota(I32, xp.shape, 0)
        xp = jnp.where(rowi < nv, xp, jnp.uint32(0))
        _rows_to_tiles(ys_ref, _pack_halves(_swiglu_packed(xp, wg_b, wu_b, wd_b)))

    @pl.when(nv == 0)
    def _():
        ys_ref[...] = jnp.zeros(ys_ref.shape, U32)


def _experts(xs, tile_e, tile_nv, tile_ord, tile_next, w_gate, w_up, w_down):
    rows, lanes = xs.shape
    tm = EXPERT_TILE_ROWS
    n_exp, d, ff = w_gate.shape
    blk = pl.BlockSpec((tm * TILE_SUBLANES, lanes), lambda i, *_: (i, 0))
    hbm = pl.BlockSpec(memory_space=pl.ANY)
    return pl.pallas_call(
        _experts_kernel,
        grid_spec=pltpu.PrefetchScalarGridSpec(
            num_scalar_prefetch=4,
            grid=(rows // (tm * TILE_SUBLANES),),
            in_specs=[blk, hbm, hbm, hbm],
            out_specs=blk,
            scratch_shapes=[
                pltpu.VMEM((2, d, ff), F32), pltpu.VMEM((2, d, ff), F32), pltpu.VMEM((2, ff, d), F32),
                pltpu.VMEM((d, ff), BF16), pltpu.VMEM((d, ff), BF16), pltpu.VMEM((ff, d), BF16),
                pltpu.SemaphoreType.DMA((3, 2)),
            ],
        ),
        out_shape=jax.ShapeDtypeStruct((rows, lanes), U32),
        compiler_params=_cparams(1),
        name="experts",
    )(tile_e, tile_nv, tile_ord, tile_next, xs, w_gate, w_up, w_down)


def _combine_kernel(w_ref, yg_ref, h2p_ref, x1_ref, g2_ref, gf_ref, wsg_ref, wsu_ref, wsd_ref, o_ref, *, tc):
    shared = _swiglu_packed(_rows_from_tiles(h2p_ref, tc), wsg_ref, wsu_ref, wsd_ref)
    w = w_ref[...]
    lo, hi = None, None
    for k in range(TOP_K):
        rows = tc * TILE_SUBLANES
        yk = _rows_from_tiles(yg_ref.at[pl.ds(k * rows, rows)], tc)
        wk = w[:, k:k + 1]
        yl = pltpu.unpack_elementwise(yk, index=0, packed_dtype=BF16, unpacked_dtype=F32) * wk
        yh = pltpu.unpack_elementwise(yk, index=1, packed_dtype=BF16, unpacked_dtype=F32) * wk
        lo = yl if lo is None else lo + yl
        hi = yh if hi is None else hi + yh
    routed = jnp.concatenate([lo, hi], axis=-1)
    x2 = x1_ref[...] + g2_ref[...] * (routed + shared)
    o_ref[...] = _rms(x2, NORM_EPS) * gf_ref[...]


def _combine(yg, wts_t, h2p_all, x1, g2, g_final, wsg_bf, wsu_bf, wsd_bf, rows_per_batch, row_offset):
    r, d = x1.shape
    lanes = h2p_all.shape[1]
    tc = COMBINE_TILE
    assert r % tc == 0 and row_offset % tc == 0
    off = row_offset // tc
    ff = wsg_bf.shape[1]
    const = lambda shp: pl.BlockSpec(shp, lambda i: (0,) * len(shp))
    return pl.pallas_call(
        functools.partial(_combine_kernel, tc=tc),
        grid=(r // tc,),
        in_specs=[
            pl.BlockSpec((tc, TOP_K), lambda i: (i, 0)),
            pl.BlockSpec((TOP_K * tc * TILE_SUBLANES, lanes), lambda i: (i + off, 0)),
            pl.BlockSpec((tc * TILE_SUBLANES, lanes), lambda i: (i + off, 0)),
            pl.BlockSpec((tc, d), lambda i: (i, 0)),
            _mod_spec(rows_per_batch, tc, d),
            const((1, d)), const((d, ff)), const((d, ff)), const((ff, d)),
        ],
        out_specs=pl.BlockSpec((tc, d), lambda i: (i, 0)),
        out_shape=jax.ShapeDtypeStruct((r, d), F32),
        compiler_params=_cparams(1),
        name="combine",
    )(wts_t, yg, h2p_all, x1, _mod_array(g2, rows_per_batch, tc), g_final.reshape(1, d),
      wsg_bf, wsu_bf, wsd_bf)


def kernel(x_prompt, x_sample, cache_k, cache_v, state_h, state_conv, page_table, c_prompt, c_sample,
           w_ada, b_ada, g_norm1, w_in, lambda_q1, lambda_k1, lambda_q2, lambda_k2, g_subln,
           conv_w, conv_b, w_rg_a, b_rg_a, w_rg_i, b_rg_i, rg_lambda, g_rgnorm, w_o, g_norm2,
           w_router, router_bias, w_e_gate, w_e_up, w_e_down, w_s_gate, w_s_up, w_s_down, g_final):
    depth = w_ada.shape[0]
    assert depth == 1, "single-layer step"
    bp, seq, d = x_prompt.shape
    bs, n_new, _ = x_sample.shape
    n_heads = cache_k.shape[3]
    k_row = cache_k.shape[4]
    v_head = cache_v.shape[4]
    aw = n_heads * v_head
    rw = d - aw
    assert k_row == v_head and w_in.shape[2] == 3 * aw + 2 * rw and aw == rw
    qk_half = k_row // 2
    n_exp = w_router.shape[2]
    lam_init = 0.8 - 0.6 * math.exp(-0.3 * 0)
    tp, ts = bp * seq, bs * n_new
    t_all = tp + ts

    w_in_bf = w_in[0].astype(BF16)
    wo_bf = w_o[0].astype(BF16)
    wr_t = w_router[0].T
    wai_bf = jnp.concatenate([w_rg_a[0], w_rg_i[0]], axis=-1).astype(BF16)
    wsg_bf, wsu_bf, wsd_bf = w_s_gate[0].astype(BF16), w_s_up[0].astype(BF16), w_s_down[0].astype(BF16)
    lamv = jnp.stack([lambda_q1[0], lambda_k1[0], lambda_q2[0], lambda_k2[0]])
    slopes = jnp.exp2(-8.0 * jnp.arange(1, n_heads + 1, dtype=F32) / n_heads)

    mod = _ada(jnp.concatenate([c_prompt, c_sample], axis=0), w_ada[0], b_ada[0])
    mod_p = [mod[:bp, i * d:(i + 1) * d] for i in range(6)]
    mod_s = [mod[bp:, i * d:(i + 1) * d] for i in range(6)]

    xp2, xs2 = x_prompt.reshape(tp, d), x_sample.reshape(ts, d)
    qscale = qk_half ** -0.5 * LOG2E
    qp, kp, vp, up, gp, kpb, vpb = _inproj(xp2, mod_p[0], mod_p[1], g_norm1[0], w_in_bf, seq, qscale)
    qs, ks, vs, us, gs, _, _ = _inproj(xs2, mod_s[0], mod_s[1], g_norm1[0], w_in_bf, n_new, qscale)

    att_p = _pattn(qp, kpb, vpb, slopes, lamv, g_subln[0], bp, seq, n_heads, lam_init)

    q5 = qs.reshape(bs, n_new, n_heads, 2, qk_half).transpose(0, 2, 3, 1, 4)
    eye_h = jnp.eye(n_heads, dtype=BF16)
    eye_c = jnp.eye(2, dtype=BF16)
    wq = (q5[:, :, :, :, None, None, :] * eye_h[None, :, None, None, :, None, None]
          * eye_c[None, None, :, None, None, :, None]).reshape(bs, n_heads * 2 * n_new, aw)
    n_pool, page = cache_k.shape[1], cache_k.shape[2]
    att_s = _sattn(wq, ks.reshape(bs, n_new, aw), vs.reshape(bs, n_new, aw),
                   cache_k.reshape(n_pool, page * n_heads, k_row), cache_v.reshape(n_pool, page * n_heads, v_head),
                   page_table, lamv, g_subln[0], n_heads, lam_init).reshape(ts, aw)

    rg_args = (conv_w[0], conv_b[0], wai_bf, b_rg_a[0], b_rg_i[0], rg_lambda[0], g_rgnorm[0])
    rec_p, h_p, conv_p = _rglru(up, gp, jnp.zeros((bp, CONV_WIDTH - 1, rw), F32), jnp.zeros((bp, rw), F32),
                                *rg_args, bp, seq)
    rec_s, h_s, conv_s = _rglru(us, gs, state_conv[0], state_h[0], *rg_args, bs, n_new)

    x1p, h2p_all, st_all = _oproj(att_p, rec_p, xp2, mod_p[2], mod_p[3], mod_p[4], g_norm2[0], wo_bf, wr_t,
                                  seq, 0, t_all)
    x1s, h2p_all, st_all = _oproj(att_s, rec_s, xs2, mod_s[2], mod_s[3], mod_s[4], g_norm2[0], wo_bf, wr_t,
                                  n_new, tp, t_all, h2p_all, st_all)

    idx, wts, rank, counts = _route(st_all, router_bias[0])

    tm = EXPERT_TILE_ROWS
    counts = counts.reshape(n_exp)
    ptiles = (counts + tm - 1) // tm
    pend = jnp.cumsum(ptiles)
    pstart = pend - ptiles
    n_tiles = (t_all * TOP_K) // tm + n_exp
    tile_ids = jnp.arange(n_tiles, dtype=I32)
    tile_e = jnp.minimum(jnp.sum(pend[None, :] <= tile_ids[:, None], axis=1), n_exp - 1).astype(I32)
    tile_nv = jnp.where(tile_ids < pend[-1],
                        jnp.clip(counts[tile_e] - (tile_ids - pstart[tile_e]) * tm, 0, tm), 0).astype(I32)
    tile_ord = (jnp.cumsum(ptiles > 0) - 1)[tile_e].astype(I32)
    next_tile = pend[tile_e]
    tile_next = jnp.where(next_tile < pend[-1], tile_e[jnp.minimum(next_tile, n_tiles - 1)], -1).astype(I32)
    pos = _pos(idx, rank, pstart * tm)

    lanes = h2p_all.shape[1]
    tile3 = lambda a2: a2.reshape(-1, TILE_SUBLANES, lanes)
    xs_rows = _sc_scatter_rows(tile3(h2p_all), _tile_pos(pos, SC_CHUNK_ROWS), n_tiles * tm)
    ys = _experts(xs_rows.reshape(-1, lanes), tile_e, tile_nv, tile_ord, tile_next,
                  w_e_gate[0], w_e_up[0], w_e_down[0])
    yg = _sc_gather_rows(tile3(ys), _tile_pos(pos, COMBINE_TILE).reshape(-1, SC_CHUNK_ROWS)).reshape(-1, lanes)

    wts_t = wts.T
    y_p = _combine(yg, wts_t[:tp], h2p_all, x1p, mod_p[5], g_final, wsg_bf, wsu_bf, wsd_bf, seq, 0)
    y_s = _combine(yg, wts_t[tp:], h2p_all, x1s, mod_s[5], g_final, wsg_bf, wsu_bf, wsd_bf, n_new, tp)

    return (y_p.reshape(bp, seq, d), y_s.reshape(bs, n_new, d),
            kp.reshape(1, bp, seq, n_heads, k_row), vp.reshape(1, bp, seq, n_heads, v_head),
            h_p.reshape(1, bp, rw), conv_p.reshape(1, bp, CONV_WIDTH - 1, rw),
            ks.reshape(1, bs, n_new, n_heads, k_row), vs.reshape(1, bs, n_new, n_heads, v_head),
            h_s.reshape(1, bs, rw), conv_s.reshape(1, bs, CONV_WIDTH - 1, rw))
```

```python
import functools
import math

import jax
import jax.numpy as jnp
import numpy as np
from jax import lax
from jax.experimental import pallas as pl
from jax.experimental.pallas import tpu as pltpu
from jax.experimental.pallas import tpu_sc as plsc

F32 = jnp.float32
BF16 = jnp.bfloat16
I32 = jnp.int32
U32 = jnp.uint32

NORM_EPS = 1e-6
SUBLN_EPS = 1e-5
NEG = -1e30
RG_C = 8.0
ROUTED_SCALE = 2.5
N_GROUPS = 8
TOPK_GROUPS = 4
TOP_K = 8
CONV_WIDTH = 4

V7X_VMEM_LIMIT_BYTES = 56 * 1024 * 1024
EXPERT_TILE_ROWS = 128
EXPERT_WEIGHT_DMA_CHUNKS = 8
V7X_SC_CORES = 2
V7X_SC_SUBCORES = 16
COMBINE_TILE = 128
SC_CHUNK_ROWS = 64


def _cparams(n_axes):
    return pltpu.CompilerParams(
        dimension_semantics=("arbitrary",) * n_axes, vmem_limit_bytes=V7X_VMEM_LIMIT_BYTES
    )


def _pick(n, candidates):
    for c in candidates:
        if n % c == 0:
            return c
    return n


def _dot(a, b):
    return jnp.dot(a, b, preferred_element_type=F32)


def _dot_nt(a, b):
    return lax.dot_general(a, b, (((1,), (1,)), ((), ())), preferred_element_type=F32)


def _split(x):
    hi = x.astype(BF16)
    lo = (x - hi.astype(F32)).astype(BF16)
    return hi, lo


def _rms(x, eps):
    return x * lax.rsqrt(jnp.mean(x * x, axis=-1, keepdims=True) + eps)


def _silu(x):
    return x * jax.nn.sigmoid(x)


def _bf16_terms(x, n):
    terms = []
    for _ in range(n):
        bits = np.float32(x).view(np.uint32)
        bits = (bits + np.uint32(0x7FFF) + ((bits >> np.uint32(16)) & np.uint32(1))) & np.uint32(0xFFFF0000)
        t = float(bits.view(np.float32))
        terms.append(t)
        x -= t
    return tuple(terms)


LOG2E = math.log2(math.e)
LOG2E_BF16_TERMS = _bf16_terms(LOG2E, 3)
TILE_SUBLANES = 8


def _rows_from_tiles(ref, n_rows):
    return jnp.concatenate(
        [ref[pl.ds(sub, n_rows, stride=TILE_SUBLANES), :] for sub in range(TILE_SUBLANES)], axis=-1)


def _rows_to_tiles(ref, x):
    n_rows, width = x.shape
    lanes = width // TILE_SUBLANES
    for sub in range(TILE_SUBLANES):
        ref[pl.ds(sub, n_rows, stride=TILE_SUBLANES), :] = x[:, sub * lanes:(sub + 1) * lanes]


def _unpack_halves(xp):
    lo = pltpu.unpack_elementwise(xp, index=0, packed_dtype=BF16, unpacked_dtype=F32)
    hi = pltpu.unpack_elementwise(xp, index=1, packed_dtype=BF16, unpacked_dtype=F32)
    return lo.astype(BF16), hi.astype(BF16)


def _pack_halves(x):
    n = x.shape[-1] // 2
    return pltpu.pack_elementwise([x[:, :n], x[:, n:]], packed_dtype=BF16)


def _ada_kernel(c_ref, w_ref, b_ref, o_ref):
    a_hi, a_lo = _split(_silu(c_ref[...]))
    w_hi, w_lo = _split(w_ref[...])
    o_ref[...] = _dot(a_hi, w_hi) + (_dot(a_hi, w_lo) + _dot(a_lo, w_hi)) + b_ref[...]


def _ada(c, w, b):
    n, d = c.shape
    d_out = w.shape[1]
    tn = _pick(d_out, (512, 256, 128))
    return pl.pallas_call(
        _ada_kernel,
        grid=(d_out // tn,),
        in_specs=[
            pl.BlockSpec((n, d), lambda j: (0, 0)),
            pl.BlockSpec((d, tn), lambda j: (0, j)),
            pl.BlockSpec((1, tn), lambda j: (0, j)),
        ],
        out_specs=pl.BlockSpec((n, tn), lambda j: (0, j)),
        out_shape=jax.ShapeDtypeStruct((n, d_out), F32),
        compiler_params=_cparams(1),
        name="ada",
    )(c, w, b.reshape(1, d_out))


def _mod_spec(rows_per_batch, tm, d, n_tiles=None):
    clamp = (lambda i: i) if n_tiles is None else (lambda i: jnp.minimum(i, n_tiles - 1))
    if rows_per_batch % tm == 0:
        per = rows_per_batch // tm
        return pl.BlockSpec((None, 1, d), lambda i, *_: (clamp(i) // per, 0, 0))
    return pl.BlockSpec((None, tm, d), lambda i, *_: (clamp(i), 0, 0))


def _mod_array(m, rows_per_batch, tm):
    nb, d = m.shape
    if rows_per_batch % tm == 0:
        return m.reshape(nb, 1, d)
    assert tm % rows_per_batch == 0
    return jnp.repeat(m, rows_per_batch, axis=0).reshape(nb * rows_per_batch // tm, tm, d)


def _inproj_kernel(x_ref, sh_ref, sc_ref, g_ref, w_ref,
                   q_ref, k_ref, v_ref, u_ref, gt_ref, kb_ref, vb_ref, h_scr, *, qscale):
    j = pl.program_id(1)

    @pl.when(j == 0)
    def _():
        y = _rms(x_ref[...], NORM_EPS) * g_ref[...]
        h_scr[...] = (y * (1.0 + sc_ref[...]) + sh_ref[...]).astype(BF16)

    z = _dot(h_scr[...], w_ref[...])

    @pl.when(j == 0)
    def _():
        q_ref[...] = (z * qscale).astype(BF16)

    @pl.when(j == 1)
    def _():
        k_ref[...] = z
        kb_ref[...] = z.astype(BF16)

    @pl.when(j == 2)
    def _():
        v_ref[...] = z
        vb_ref[...] = z.astype(BF16)

    @pl.when(j == 3)
    def _():
        u_ref[...] = z

    @pl.when(j == 4)
    def _():
        gt_ref[...] = z


def _inproj(x2d, shift, scale, g, w_bf, rows_per_batch, qscale):
    r, d = x2d.shape
    wd = w_bf.shape[1] // 5
    tm = _pick(r, (512, 256, 128, 64, 32, 16, 8))
    row = lambda i, j: (i, 0)
    f32o = jax.ShapeDtypeStruct((r, wd), F32)
    bfo = jax.ShapeDtypeStruct((r, wd), BF16)
    return pl.pallas_call(
        functools.partial(_inproj_kernel, qscale=qscale),
        grid=(r // tm, 5),
        in_specs=[
            pl.BlockSpec((tm, d), row),
            _mod_spec(rows_per_batch, tm, d),
            _mod_spec(rows_per_batch, tm, d),
            pl.BlockSpec((1, d), lambda i, j: (0, 0)),
            pl.BlockSpec((d, wd), lambda i, j: (0, j)),
        ],
        out_specs=[pl.BlockSpec((tm, wd), row)] * 7,
        out_shape=[bfo, f32o, f32o, f32o, f32o, bfo, bfo],
        scratch_shapes=[pltpu.VMEM((tm, d), BF16)],
        compiler_params=_cparams(2),
        name="inproj",
    )(x2d, _mod_array(shift, rows_per_batch, tm), _mod_array(scale, rows_per_batch, tm), g.reshape(1, d), w_bf)


def _lam(lamv_ref, lam_init):
    lv = lamv_ref[...]
    s1 = jnp.sum(lv[0:1] * lv[1:2], axis=-1, keepdims=True)
    s2 = jnp.sum(lv[2:3] * lv[3:4], axis=-1, keepdims=True)
    return jnp.exp(s1) - jnp.exp(s2) + lam_init


def _online_update(s, v, m_scr, l_scr, acc_scr, shift=None):
    m_old = m_scr[...]
    s_max = jnp.max(s, axis=-1, keepdims=True)
    m_new = jnp.maximum(m_old, s_max if shift is None else s_max + shift)
    alpha = jnp.exp2(m_old - m_new)
    p = jnp.exp2(s - (m_new if shift is None else m_new - shift))
    l_scr[...] = alpha * l_scr[...] + jnp.sum(p, axis=-1, keepdims=True)
    acc_scr[...] = alpha * acc_scr[...] + _dot(p.astype(BF16), v)
    m_scr[...] = m_new


def _pattn_kernel(slopes_ref, lamv_ref, gs_ref, q_ref, k_ref, v_ref, o_ref, q_scr, ka_scr, *state,
                  tq, rc, lam_init):
    n_chunks = 2 * tq // rc
    m_scrs, l_scrs, acc_scrs = state[:n_chunks], state[n_chunks:2 * n_chunks], state[2 * n_chunks:]
    h = pl.program_id(1)
    qi = pl.program_id(2)
    slope = slopes_ref[h]
    hd = q_ref.shape[1]
    half = hd // 2
    n_terms = len(LOG2E_BF16_TERMS)

    q = q_ref[...]
    lane = lax.broadcasted_iota(I32, q.shape, 1)
    zero = jnp.zeros_like(q)
    qa = jnp.zeros(q.shape, F32)
    for n, term in enumerate(LOG2E_BF16_TERMS):
        qa = jnp.where(jnp.logical_or(lane == n, lane == n + n_terms), term, qa)
    qa = qa.astype(BF16)
    q_scr[0:tq, 0:hd] = jnp.where(lane < half, q, zero)
    q_scr[tq:2 * tq, 0:hd] = jnp.where(lane >= half, q, zero)
    q_scr[0:tq, hd:2 * hd] = qa
    q_scr[tq:2 * tq, hd:2 * hd] = qa
    c = lax.broadcasted_iota(I32, (tq, hd), 0)
    c_lo = c % 256
    ka = jnp.where(lane < n_terms, c_lo.astype(F32) * slope,
                   jnp.where(lane < 2 * n_terms, (c - c_lo).astype(F32) * slope, 0.0))
    ka_scr[...] = ka.astype(BF16)

    for m_scr, l_scr, acc_scr in zip(m_scrs, l_scrs, acc_scrs):
        m_scr[...] = jnp.full(m_scr.shape, NEG, F32)
        l_scr[...] = jnp.zeros(l_scr.shape, F32)
        acc_scr[...] = jnp.zeros(acc_scr.shape, F32)

    def step(j, masked):
        start = pl.multiple_of(j * tq, tq)
        kaug = jnp.concatenate([k_ref[pl.ds(start, tq), :], ka_scr[...]], axis=-1)
        v = v_ref[pl.ds(start, tq), :]
        for ci in range(n_chunks):
            r0 = ci * rc
            q0 = r0 % tq
            ncol = q0 + rc if masked else tq
            rowpos = q0 + lax.broadcasted_iota(I32, (rc, 1), 0)
            s = _dot_nt(q_scr[r0:r0 + rc, :], kaug[:ncol])
            if masked:
                s = jnp.where(lax.broadcasted_iota(I32, (rc, ncol), 1) <= rowpos, s, NEG)
            shift = ((j - qi) * tq - rowpos).astype(F32) * (slope * LOG2E)
            _online_update(s, v[:ncol], m_scrs[ci], l_scrs[ci], acc_scrs[ci], shift=shift)

    def body(j, carry):
        step(j, False)
        return carry

    lax.fori_loop(0, qi, body, 0)
    step(qi, True)

    lam = _lam(lamv_ref, lam_init)
    o = jnp.concatenate([acc[...] / l[...] for acc, l in zip(acc_scrs, l_scrs)], axis=0)
    att = o[:tq] - lam * o[tq:]
    att = _rms(att, SUBLN_EPS) * gs_ref[...] * (1.0 - lam_init)
    o_ref[...] = att.astype(BF16)


def _pattn(qb, kb, vb, slopes, lamv, g_subln, n_batch, seq, n_heads, lam_init):
    r, aw = qb.shape
    hd = aw // n_heads
    tq = _pick(seq, (2048, 1024, 512, 256, 128))
    rc = min(tq, 256)
    n_chunks = 2 * tq // rc
    nq = seq // tq
    return pl.pallas_call(
        functools.partial(_pattn_kernel, tq=tq, rc=rc, lam_init=lam_init),
        grid_spec=pltpu.PrefetchScalarGridSpec(
            num_scalar_prefetch=1,
            grid=(n_batch, n_heads, nq),
            in_specs=[
                pl.BlockSpec(lamv.shape, lambda b, h, i, *_: (0, 0)),
                pl.BlockSpec((1, hd), lambda b, h, i, *_: (0, 0)),
                pl.BlockSpec((tq, hd), lambda b, h, i, *_: (b * nq + i, h)),
                pl.BlockSpec((seq, hd), lambda b, h, i, *_: (b, h)),
                pl.BlockSpec((seq, hd), lambda b, h, i, *_: (b, h)),
            ],
            out_specs=pl.BlockSpec((tq, hd), lambda b, h, i, *_: (b * nq + i, h)),
            scratch_shapes=[pltpu.VMEM((2 * tq, 2 * hd), BF16), pltpu.VMEM((tq, hd), BF16)]
            + [pltpu.VMEM((rc, 1), F32)] * (2 * n_chunks) + [pltpu.VMEM((rc, hd), F32)] * n_chunks,
        ),
        out_shape=jax.ShapeDtypeStruct((r, aw), BF16),
        compiler_params=_cparams(3),
        name="pattn",
    )(slopes, lamv, g_subln.reshape(1, hd), qb, kb, vb)


def _sattn_kernel(pt_ref, lamv_ref, gs_ref, wq_ref, kn_ref, vn_ref, *rest,
                  n_pages_step, page, past, n_new, n_heads, lam_init):
    k_refs = rest[:n_pages_step]
    v_refs = rest[n_pages_step:2 * n_pages_step]
    o_ref = rest[2 * n_pages_step]
    m_scr, l_scr, acc_scr = rest[2 * n_pages_step + 1:]
    j = pl.program_id(1)
    n_rows = wq_ref.shape[0]
    hd = wq_ref.shape[1] // n_heads
    tk = n_pages_step * page

    r = lax.broadcasted_iota(I32, (n_rows, 1), 0)
    head = r // (2 * n_new)
    qi = r % n_new
    slope = jnp.exp2(-(head + 1).astype(F32)) * LOG2E

    @pl.when(j == 0)
    def _():
        m_scr[...] = jnp.full(m_scr.shape, NEG, F32)
        l_scr[...] = jnp.zeros(l_scr.shape, F32)
        acc_scr[...] = jnp.zeros(acc_scr.shape, F32)

    wq = wq_ref[...]
    kc = jnp.concatenate([_rows_from_tiles(kr, page).astype(BF16) for kr in k_refs], axis=0)
    vc = jnp.concatenate([_rows_from_tiles(vr, page).astype(BF16) for vr in v_refs], axis=0)
    t = j * tk + lax.broadcasted_iota(I32, (1, tk), 1)
    s = _dot_nt(wq, kc) - slope * (past + qi - t).astype(F32)
    _online_update(s, vc, m_scr, l_scr, acc_scr)

    @pl.when(j == pl.num_programs(1) - 1)
    def _():
        pad = jnp.zeros((page - n_new, kn_ref.shape[1]), BF16)
        kn = jnp.concatenate([kn_ref[...].astype(BF16), pad], axis=0)
        vn = jnp.concatenate([vn_ref[...].astype(BF16), pad], axis=0)
        tj = lax.broadcasted_iota(I32, (1, page), 1)
        sn = _dot_nt(wq, kn) - slope * (qi - tj).astype(F32)
        sn = jnp.where(tj <= qi, sn, NEG)
        _online_update(sn, vn, m_scr, l_scr, acc_scr)

        lam = _lam(lamv_ref, lam_init)
        o = acc_scr[...] / l_scr[...]
        outs = []
        for h in range(n_heads):
            blk = o[h * 2 * n_new:(h + 1) * 2 * n_new, h * hd:(h + 1) * hd]
            att = blk[:n_new] - lam * blk[n_new:]
            outs.append(_rms(att, SUBLN_EPS) * gs_ref[...] * (1.0 - lam_init))
        o_ref[...] = jnp.concatenate(outs, axis=-1).astype(BF16)


def _sattn(wq, k_new, v_new, cache_k2, cache_v2, page_table, lamv, g_subln, n_heads, lam_init):
    n_seq, n_rows, aw = wq.shape
    n_new = k_new.shape[1]
    hd = aw // n_heads
    page = cache_k2.shape[1] // n_heads
    n_pages = page_table.shape[1]
    pstep = _pick(n_pages, (8, 4, 2, 1))
    past = n_pages * page
    assert n_heads == TILE_SUBLANES and hd == 128

    def page_spec(p):
        return pl.BlockSpec((None, page * n_heads, hd), lambda b, j, pt: (pt[b, j * pstep + p], 0, 0))

    seq_spec = lambda rows: pl.BlockSpec((None, rows, aw), lambda b, j, pt: (b, 0, 0))
    return pl.pallas_call(
        functools.partial(_sattn_kernel, n_pages_step=pstep, page=page, past=past, n_new=n_new,
                          n_heads=n_heads, lam_init=lam_init),
        grid_spec=pltpu.PrefetchScalarGridSpec(
            num_scalar_prefetch=1,
            grid=(n_seq, n_pages // pstep),
            in_specs=[
                pl.BlockSpec(lamv.shape, lambda b, j, pt: (0, 0)),
                pl.BlockSpec((1, hd), lambda b, j, pt: (0, 0)),
                seq_spec(n_rows), seq_spec(n_new), seq_spec(n_new),
            ] + [page_spec(p) for p in range(pstep)] * 2,
            out_specs=seq_spec(n_new),
            scratch_shapes=[
                pltpu.VMEM((n_rows, 1), F32),
                pltpu.VMEM((n_rows, 1), F32),
                pltpu.VMEM((n_rows, aw), F32),
            ],
        ),
        out_shape=jax.ShapeDtypeStruct((n_seq, n_new, aw), BF16),
        compiler_params=_cparams(2),
        name="sattn",
    )(page_table, lamv, g_subln.reshape(1, hd), wq, k_new, v_new,
      *([cache_k2] * pstep), *([cache_v2] * pstep))


def _gelu_tanh(x):
    return x * (0.5 * (1.0 + jnp.tanh(math.sqrt(2.0 / math.pi) * (x + 0.044715 * (x * x * x)))))


def _softplus(x):
    return jnp.maximum(x, 0.0) + jnp.log1p(jnp.exp(-jnp.abs(x)))


def _rglru_kernel(u_ref, gt_ref, c0_ref, h0_ref, cw_ref, cb_ref, wai_ref, ba_ref, bi_ref, lam_ref, gn_ref,
                  rec_ref, ht_ref, cout_ref, ubuf, hcar, *, tl, n_blocks):
    t = pl.program_id(1)
    halo = CONV_WIDTH - 1
    base = 8

    @pl.when(t == 0)
    def _():
        ubuf[base - halo:base, :] = c0_ref[...]
        hcar[...] = h0_ref[...]

    ubuf[base:base + tl, :] = u_ref[...]
    cw = cw_ref[...]
    xc = cb_ref[...] + cw[0:1] * ubuf[base - halo:base - halo + tl, :]
    for jj in range(1, CONV_WIDTH):
        xc = xc + cw[jj:jj + 1] * ubuf[base - halo + jj:base - halo + jj + tl, :]
    tail = ubuf[base + tl - halo:base + tl, :]
    ubuf[base - halo:base, :] = tail
    cout_ref[...] = tail

    bw = xc.shape[1] // n_blocks
    za, zi = [], []
    for n in range(n_blocks):
        z = _dot(xc[:, n * bw:(n + 1) * bw].astype(BF16), wai_ref[n])
        za.append(z[:, :bw])
        zi.append(z[:, bw:])
    r = jax.nn.sigmoid(jnp.concatenate(za, axis=-1) + ba_ref[...])
    i = jax.nn.sigmoid(jnp.concatenate(zi, axis=-1) + bi_ref[...])
    log_a = -RG_C * r * _softplus(-lam_ref[...])
    a = jnp.exp(log_a)
    th = jnp.tanh(log_a)
    b = xc * i * jnp.sqrt(-2.0 * th / (1.0 - th))

    rowi = lax.broadcasted_iota(I32, a.shape, 0)
    sft = 1
    while sft < tl:
        keep = rowi >= sft
        a_prev = jnp.where(keep, pltpu.roll(a, sft, 0), 1.0)
        b_prev = jnp.where(keep, pltpu.roll(b, sft, 0), 0.0)
        b = a * b_prev + b
        a = a * a_prev
        sft *= 2
    hs = a * hcar[...] + b
    h_last = hs[tl - 1:tl, :]
    hcar[...] = h_last
    ht_ref[...] = h_last

    rec = hs * _gelu_tanh(gt_ref[...])
    rec_ref[...] = (_rms(rec, NORM_EPS) * gn_ref[...]).astype(BF16)


def _rglru(u2d, gate2d, conv0, h0, conv_w, conv_b, wai_bf, b_a, b_i, rg_lambda, g_rgnorm, n_batch, seq):
    r, w = u2d.shape
    tl = _pick(seq, (256, 128, 64, 32, 16, 8))
    nt = seq // tl
    n_blocks = wai_bf.shape[0]
    halo = CONV_WIDTH - 1
    row = lambda b, t: (b * nt + t, 0)
    vec = pl.BlockSpec((1, w), lambda b, t: (0, 0))
    rec, ht, cout = pl.pallas_call(
        functools.partial(_rglru_kernel, tl=tl, n_blocks=n_blocks),
        grid=(n_batch, nt),
        in_specs=[
            pl.BlockSpec((tl, w), row),
            pl.BlockSpec((tl, w), row),
            pl.BlockSpec((None, halo, w), lambda b, t: (b, 0, 0)),
            pl.BlockSpec((None, 1, w), lambda b, t: (b, 0, 0)),
            pl.BlockSpec((CONV_WIDTH, w), lambda b, t: (0, 0)),
            vec,
            pl.BlockSpec(wai_bf.shape, lambda b, t: (0, 0, 0)),
            vec, vec, vec, vec,
        ],
        out_specs=[
            pl.BlockSpec((tl, w), row),
            pl.BlockSpec((None, 1, w), lambda b, t: (b, 0, 0)),
            pl.BlockSpec((None, halo, w), lambda b, t: (b, 0, 0)),
        ],
        out_shape=[
            jax.ShapeDtypeStruct((r, w), BF16),
            jax.ShapeDtypeStruct((n_batch, 1, w), F32),
            jax.ShapeDtypeStruct((n_batch, halo, w), F32),
        ],
        scratch_shapes=[pltpu.VMEM((tl + 8, w), F32), pltpu.VMEM((1, w), F32)],
        compiler_params=_cparams(2),
        name="rglru",
    )(u2d, gate2d, conv0, h0.reshape(n_batch, 1, w), conv_w, conv_b.reshape(1, w), wai_bf,
      b_a.reshape(1, w), b_i.reshape(1, w), rg_lambda.reshape(1, w), g_rgnorm.reshape(1, w))
    return rec, ht.reshape(n_batch, w), cout


def _oproj_kernel(att_ref, rec_ref, x_ref, g1_ref, sh_ref, sc_ref, gn_ref, wo_ref, wrt_ref, *rest,
                  aliased, n_tiles):
    x1_ref, h2p_ref, st_ref = rest[2:] if aliased else rest
    aw = att_ref.shape[1]
    i = pl.program_id(0)

    @pl.when(i < n_tiles)
    def _():
        mix = _dot(att_ref[...], wo_ref[:aw, :]) + _dot(rec_ref[...], wo_ref[aw:, :])
        x1 = x_ref[...] + g1_ref[...] * mix
        x1_ref[...] = x1
        h2 = (_rms(x1, NORM_EPS) * gn_ref[...]) * (1.0 + sc_ref[...]) + sh_ref[...]
        _rows_to_tiles(h2p_ref, _pack_halves(h2))
        h_hi, h_lo = _split(h2)
        w_hi, w_lo = _split(wrt_ref[...])
        logits_t = _dot_nt(w_hi, h_hi) + (_dot_nt(w_hi, h_lo) + _dot_nt(w_lo, h_hi))
        st_ref[...] = jax.nn.sigmoid(logits_t)

    @pl.when(i >= n_tiles)
    def _():
        h2p_ref[...] = jnp.zeros(h2p_ref.shape, U32)
        st_ref[...] = jnp.zeros(st_ref.shape, F32)


def _oproj(att, rec, x2d, g1, shift, scale, g_norm2, wo_bf, wr_t, rows_per_batch, row_offset, total_rows,
           h2p_all=None, st_all=None):
    r, d = x2d.shape
    aw = att.shape[1]
    n_exp = wr_t.shape[0]
    tm = _pick(r, (256, 128))
    assert row_offset % tm == 0
    off = row_offset // tm
    lanes = d // 2 // TILE_SUBLANES
    assert lanes == 128
    aliased = h2p_all is not None
    n_tiles = r // tm
    n_fill = 0 if aliased else (total_rows - r) // tm
    assert aliased or (row_offset == 0 and (total_rows - r) % tm == 0)
    row = lambda i: (jnp.minimum(i, n_tiles - 1), 0)
    mspec = _mod_spec(rows_per_batch, tm, d, n_tiles)
    in_specs = [
        pl.BlockSpec((tm, aw), row),
        pl.BlockSpec((tm, d - aw), row),
        pl.BlockSpec((tm, d), row),
        mspec, mspec, mspec,
        pl.BlockSpec((1, d), lambda i: (0, 0)),
        pl.BlockSpec((d, d), lambda i: (0, 0)),
        pl.BlockSpec((n_exp, d), lambda i: (0, 0)),
    ]
    args = [att, rec, x2d, _mod_array(g1, rows_per_batch, tm), _mod_array(shift, rows_per_batch, tm),
            _mod_array(scale, rows_per_batch, tm), g_norm2.reshape(1, d), wo_bf, wr_t]
    io_alias = {}
    if aliased:
        in_specs += [pl.BlockSpec(memory_space=pl.ANY), pl.BlockSpec(memory_space=pl.ANY)]
        io_alias = {len(args): 1, len(args) + 1: 2}
        args += [h2p_all, st_all]
    return pl.pallas_call(
        functools.partial(_oproj_kernel, aliased=aliased, n_tiles=n_tiles),
        grid=(n_tiles + n_fill,),
        in_specs=in_specs,
        out_specs=[
            pl.BlockSpec((tm, d), row),
            pl.BlockSpec((tm * TILE_SUBLANES, lanes), lambda i: (i + off, 0)),
            pl.BlockSpec((n_exp, tm), lambda i: (0, i + off)),
        ],
        out_shape=[
            jax.ShapeDtypeStruct((r, d), F32),
            jax.ShapeDtypeStruct((total_rows * TILE_SUBLANES, lanes), U32),
            jax.ShapeDtypeStruct((n_exp, total_rows), F32),
        ],
        input_output_aliases=io_alias,
        compiler_params=_cparams(1),
        name="oproj",
    )(*args)


def _route_kernel(st_ref, rb_ref, idx_ref, wts_ref, rank_ref, cnt_ref, carry):
    i = pl.program_id(0)
    n_exp, tr = st_ref.shape
    gsz = n_exp // N_GROUPS

    @pl.when(i == 0)
    def _():
        carry[...] = jnp.zeros(carry.shape, F32)

    s = st_ref[...]
    biased = s + rb_ref[...]
    g = biased.reshape(N_GROUPS, gsz, tr)
    within = lax.broadcasted_iota(I32, g.shape, 1)
    m1 = jnp.max(g, axis=1, keepdims=True)
    first = jnp.min(jnp.where(g == m1, within, gsz), axis=1, keepdims=True)
    m2 = jnp.max(jnp.where(within == first, -jnp.inf, g), axis=1, keepdims=True)
    gscore = (m1 + m2).reshape(N_GROUPS, tr)

    gidx = lax.broadcasted_iota(I32, gscore.shape, 0)
    gsel = jnp.zeros(gscore.shape, F32)
    for _ in range(TOPK_GROUPS):
        mg = jnp.max(gscore, axis=0, keepdims=True)
        fg = jnp.min(jnp.where(gscore == mg, gidx, N_GROUPS), axis=0, keepdims=True)
        hit = gidx == fg
        gsel = jnp.where(hit, 1.0, gsel)
        gscore = jnp.where(hit, -jnp.inf, gscore)
    masked = jnp.where(gsel.reshape(N_GROUPS, 1, tr) > 0.5, g, -jnp.inf).reshape(n_exp, tr)

    eidx = lax.broadcasted_iota(I32, (n_exp, tr), 0)
    idxs, ws = [], []
    chosen = jnp.zeros((n_exp, tr), jnp.bool_)
    for _ in range(TOP_K):
        mv = jnp.max(masked, axis=0, keepdims=True)
        fe = jnp.min(jnp.where(masked == mv, eidx, n_exp), axis=0, keepdims=True)
        hit = eidx == fe
        idxs.append(fe)
        ws.append(jnp.sum(jnp.where(hit, s, 0.0), axis=0, keepdims=True))
        chosen = chosen | hit
        masked = jnp.where(hit, -jnp.inf, masked)
    idx = jnp.concatenate(idxs, axis=0)
    w = jnp.concatenate(ws, axis=0)
    idx_ref[...] = idx
    wts_ref[...] = w / jnp.sum(w, axis=0, keepdims=True) * ROUTED_SCALE

    cmat = jnp.where(chosen, 1.0, 0.0)
    before = lax.broadcasted_iota(I32, (tr, tr), 0) < lax.broadcasted_iota(I32, (tr, tr), 1)
    prior = _dot(cmat.astype(BF16), jnp.where(before, 1.0, 0.0).astype(BF16)) + carry[...]
    ranks = [jnp.sum(jnp.where(eidx == idxs[k], prior, 0.0), axis=0, keepdims=True) for k in range(TOP_K)]
    rank_ref[...] = jnp.concatenate(ranks, axis=0).astype(I32)
    carry[...] = carry[...] + jnp.sum(cmat, axis=1, keepdims=True)
    cnt_ref[...] = carry[...].astype(I32)


def _route(st_all, router_bias):
    n_exp, t_all = st_all.shape
    tr = _pick(t_all, (640, 512, 256, 128))
    col = lambda i: (0, i)
    o8 = lambda dt: jax.ShapeDtypeStruct((TOP_K, t_all), dt)
    return pl.pallas_call(
        _route_kernel,
        grid=(t_all // tr,),
        in_specs=[pl.BlockSpec((n_exp, tr), col), pl.BlockSpec((n_exp, 1), lambda i: (0, 0))],
        out_specs=[pl.BlockSpec((TOP_K, tr), col)] * 3 + [pl.BlockSpec((n_exp, 1), lambda i: (0, 0))],
        out_shape=[o8(I32), o8(F32), o8(I32), jax.ShapeDtypeStruct((n_exp, 1), I32)],
        scratch_shapes=[pltpu.VMEM((n_exp, 1), F32)],
        compiler_params=_cparams(1),
        name="route",
    )(st_all, router_bias.reshape(n_exp, 1))


def _pos_kernel(idx_ref, rank_ref, start_ref, pos_ref):
    n_exp = start_ref.shape[0]
    tr = idx_ref.shape[1]
    eidx = lax.broadcasted_iota(I32, (n_exp, tr), 0)
    start = start_ref[...]
    rows = [jnp.sum(jnp.where(eidx == idx_ref[k:k + 1, :], start, 0.0), axis=0, keepdims=True)
            for k in range(TOP_K)]
    pos_ref[...] = jnp.concatenate(rows, axis=0).astype(I32) + rank_ref[...]


def _pos(idx, rank, start_rows):
    n_exp = start_rows.shape[0]
    t_all = idx.shape[1]
    tr = _pick(t_all, (640, 512, 256, 128))
    col = pl.BlockSpec((TOP_K, tr), lambda i: (0, i))
    return pl.pallas_call(
        _pos_kernel,
        grid=(t_all // tr,),
        in_specs=[col, col, pl.BlockSpec((n_exp, 1), lambda i: (0, 0))],
        out_specs=col,
        out_shape=jax.ShapeDtypeStruct((TOP_K, t_all), I32),
        compiler_params=_cparams(1),
        name="pos",
    )(idx, rank, start_rows.astype(F32).reshape(n_exp, 1))


def _tile_pos(pos, tile):
    k, t = pos.shape
    return pos.reshape(k, t // tile, tile).transpose(1, 0, 2)


def _sc_worker_chunks(n_chunks, fn):
    n_workers = V7X_SC_CORES * V7X_SC_SUBCORES
    worker = lax.axis_index("core") * V7X_SC_SUBCORES + lax.axis_index("subcore")

    @pl.loop(0, -(-n_chunks // n_workers))
    def _(it):
        chunk = it * n_workers + worker

        @pl.when(chunk < n_chunks)
        def _():
            fn(chunk)


def _sc_mesh():
    return plsc.VectorSubcoreMesh(core_axis_name="core", subcore_axis_name="subcore",
                                  num_cores=V7X_SC_CORES, num_subcores=V7X_SC_SUBCORES)


def _sc_scatter_rows(x3, idx3, n_rows):
    n_chunks, n_k, width = idx3.shape
    row = x3.shape[1:]

    @pl.kernel(out_type=jax.ShapeDtypeStruct((n_rows,) + row, x3.dtype), mesh=_sc_mesh(),
               scratch_types=[pltpu.VMEM((width,) + row, x3.dtype), pltpu.VMEM((n_k, width), I32),
                              pltpu.SemaphoreType.DMA],
               name="dispatch_sc")
    def scatter(x_hbm, i_hbm, o_hbm, xbuf, ibuf, sem):
        def one(chunk):
            pltpu.sync_copy(x_hbm.at[pl.ds(chunk * width, width)], xbuf)
            pltpu.sync_copy(i_hbm.at[chunk], ibuf)
            copies = [pltpu.make_async_copy(xbuf, o_hbm.at[ibuf.at[k]], sem) for k in range(n_k)]
            for cp in copies:
                cp.start()
            for cp in copies:
                cp.wait()

        _sc_worker_chunks(n_chunks, one)

    return scatter(x3, idx3)


def _sc_gather_rows(src3, idx2):
    n_chunks, width = idx2.shape
    row = src3.shape[1:]

    @pl.kernel(out_type=jax.ShapeDtypeStruct((n_chunks * width,) + row, src3.dtype), mesh=_sc_mesh(),
               scratch_types=[pltpu.VMEM((width,) + row, src3.dtype), pltpu.VMEM((width,), I32),
                              pltpu.SemaphoreType.DMA],
               name="combine_sc")
    def gather(s_hbm, i_hbm, o_hbm, buf, ibuf, sem):
        def one(chunk):
            pltpu.sync_copy(i_hbm.at[chunk], ibuf)
            pltpu.async_copy(s_hbm.at[ibuf], buf, sem).wait()
            pltpu.sync_copy(buf, o_hbm.at[pl.ds(chunk * width, width)])

        _sc_worker_chunks(n_chunks, one)

    return gather(src3, idx2)


def _swiglu_packed(xp, wg, wu, wd):
    xa, xb = _unpack_halves(xp)
    half = xp.shape[1]
    g = _dot(xa, wg[:half, :]) + _dot(xb, wg[half:, :])
    u = _dot(xa, wu[:half, :]) + _dot(xb, wu[half:, :])
    return _dot((_silu(g) * u).astype(BF16), wd[...])


def _experts_kernel(te_ref, tnv_ref, tord_ref, tnext_ref, tlo_ref, thi_ref, xs_ref, wg_hbm, wu_hbm, wd_hbm, ys_ref,
                    wg_f, wu_f, wd_f, wg_b, wu_b, wd_b, sems):
    i = pl.program_id(0)
    nv = tnv_ref[i]
    expert = te_ref[i]
    new_expert = jnp.logical_or(i == 0, expert != te_ref[jnp.maximum(i - 1, 0)])
    slot = tord_ref[i] % 2
    tm = ys_ref.shape[0] // TILE_SUBLANES

    def weight_copies(e, sl):
        copies = []
        for n, (hbm, buf) in enumerate(((wg_hbm, wg_f), (wu_hbm, wu_f), (wd_hbm, wd_f))):
            rows = hbm.shape[1] // EXPERT_WEIGHT_DMA_CHUNKS
            for c in range(EXPERT_WEIGHT_DMA_CHUNKS):
                span = pl.ds(c * rows, rows)
                copies.append(pltpu.make_async_copy(hbm.at[e, span], buf.at[sl, span], sems.at[n, sl]))
        return copies

    @pl.when(jnp.logical_and(nv > 0, new_expert))
    def _():
        @pl.when(i == 0)
        def _():
            for cp in weight_copies(expert, slot):
                cp.start()

        for cp in weight_copies(expert, slot):
            cp.wait()

    nxt = tnext_ref[i]

    @pl.when(jnp.logical_and(nv > 0, nxt >= 0))
    def _():
        lo, hi = tlo_ref[i], thi_ref[i]
        for c, cp in enumerate(weight_copies(nxt, 1 - slot)):
            @pl.when(jnp.logical_and(lo <= c, c < hi))
            def _():
                cp.start()

    @pl.when(jnp.logical_and(nv > 0, new_expert))
    def _():
        wg_b[...] = wg_f[slot].astype(BF16)
        wu_b[...] = wu_f[slot].astype(BF16)
        wd_b[...] = wd_f[slot].astype(BF16)

    @pl.when(nv > 0)
    def _():
        xp = _rows_from_tiles(xs_ref, tm)
        rowi = lax.broadcasted_iota(I32, xp.shape, 0)
        xp = jnp.where(rowi < nv, xp, jnp.uint32(0))
        _rows_to_tiles(ys_ref, _pack_halves(_swiglu_packed(xp, wg_b, wu_b, wd_b)))

    @pl.when(nv == 0)
    def _():
        ys_ref[...] = jnp.zeros(ys_ref.shape, U32)


def _experts(xs, tile_tables, w_gate, w_up, w_down):
    rows, lanes = xs.shape
    tm = EXPERT_TILE_ROWS
    n_exp, d, ff = w_gate.shape
    blk = pl.BlockSpec((tm * TILE_SUBLANES, lanes), lambda i, *_: (i, 0))
    hbm = pl.BlockSpec(memory_space=pl.ANY)
    return pl.pallas_call(
        _experts_kernel,
        grid_spec=pltpu.PrefetchScalarGridSpec(
            num_scalar_prefetch=len(tile_tables),
            grid=(rows // (tm * TILE_SUBLANES),),
            in_specs=[blk, hbm, hbm, hbm],
            out_specs=blk,
            scratch_shapes=[
                pltpu.VMEM((2, d, ff), F32), pltpu.VMEM((2, d, ff), F32), pltpu.VMEM((2, ff, d), F32),
                pltpu.VMEM((d, ff), BF16), pltpu.VMEM((d, ff), BF16), pltpu.VMEM((ff, d), BF16),
                pltpu.SemaphoreType.DMA((3, 2)),
            ],
        ),
        out_shape=jax.ShapeDtypeStruct((rows, lanes), U32),
        compiler_params=_cparams(1),
        name="experts",
    )(*tile_tables, xs, w_gate, w_up, w_down)


def _combine_kernel(w_ref, yg_ref, h2p_ref, x1_ref, g2_ref, gf_ref, wsg_ref, wsu_ref, wsd_ref, o_ref, *, tc):
    shared = _swiglu_packed(_rows_from_tiles(h2p_ref, tc), wsg_ref, wsu_ref, wsd_ref)
    w = w_ref[...]
    lo, hi = None, None
    for k in range(TOP_K):
        rows = tc * TILE_SUBLANES
        yk = _rows_from_tiles(yg_ref.at[pl.ds(k * rows, rows)], tc)
        wk = w[:, k:k + 1]
        yl = pltpu.unpack_elementwise(yk, index=0, packed_dtype=BF16, unpacked_dtype=F32) * wk
        yh = pltpu.unpack_elementwise(yk, index=1, packed_dtype=BF16, unpacked_dtype=F32) * wk
        lo = yl if lo is None else lo + yl
        hi = yh if hi is None else hi + yh
    routed = jnp.concatenate([lo, hi], axis=-1)
    x2 = x1_ref[...] + g2_ref[...] * (routed + shared)
    o_ref[...] = _rms(x2, NORM_EPS) * gf_ref[...]


def _combine(yg, wts_t, h2p_all, x1, g2, g_final, wsg_bf, wsu_bf, wsd_bf, rows_per_batch, row_offset):
    r, d = x1.shape
    lanes = h2p_all.shape[1]
    tc = COMBINE_TILE
    assert r % tc == 0 and row_offset % tc == 0
    off = row_offset // tc
    ff = wsg_bf.shape[1]
    const = lambda shp: pl.BlockSpec(shp, lambda i: (0,) * len(shp))
    return pl.pallas_call(
        functools.partial(_combine_kernel, tc=tc),
        grid=(r // tc,),
        in_specs=[
            pl.BlockSpec((tc, TOP_K), lambda i: (i, 0)),
            pl.BlockSpec((TOP_K * tc * TILE_SUBLANES, lanes), lambda i: (i + off, 0)),
            pl.BlockSpec((tc * TILE_SUBLANES, lanes), lambda i: (i + off, 0)),
            pl.BlockSpec((tc, d), lambda i: (i, 0)),
            _mod_spec(rows_per_batch, tc, d),
            const((1, d)), const((d, ff)), const((d, ff)), const((ff, d)),
        ],
        out_specs=pl.BlockSpec((tc, d), lambda i: (i, 0)),
        out_shape=jax.ShapeDtypeStruct((r, d), F32),
        compiler_params=_cparams(1),
        name="combine",
    )(wts_t, yg, h2p_all, x1, _mod_array(g2, rows_per_batch, tc), g_final.reshape(1, d),
      wsg_bf, wsu_bf, wsd_bf)


def kernel(x_prompt, x_sample, cache_k, cache_v, state_h, state_conv, page_table, c_prompt, c_sample,
           w_ada, b_ada, g_norm1, w_in, lambda_q1, lambda_k1, lambda_q2, lambda_k2, g_subln,
           conv_w, conv_b, w_rg_a, b_rg_a, w_rg_i, b_rg_i, rg_lambda, g_rgnorm, w_o, g_norm2,
           w_router, router_bias, w_e_gate, w_e_up, w_e_down, w_s_gate, w_s_up, w_s_down, g_final):
    depth = w_ada.shape[0]
    assert depth == 1, "single-layer step"
    bp, seq, d = x_prompt.shape
    bs, n_new, _ = x_sample.shape
    n_heads = cache_k.shape[3]
    k_row = cache_k.shape[4]
    v_head = cache_v.shape[4]
    aw = n_heads * v_head
    rw = d - aw
    assert k_row == v_head and w_in.shape[2] == 3 * aw + 2 * rw and aw == rw
    qk_half = k_row // 2
    n_exp = w_router.shape[2]
    lam_init = 0.8 - 0.6 * math.exp(-0.3 * 0)
    tp, ts = bp * seq, bs * n_new
    t_all = tp + ts

    w_in_bf = w_in[0].astype(BF16)
    wo_bf = w_o[0].astype(BF16)
    wr_t = w_router[0].T
    wai_bf = jnp.concatenate([w_rg_a[0], w_rg_i[0]], axis=-1).astype(BF16)
    wsg_bf, wsu_bf, wsd_bf = w_s_gate[0].astype(BF16), w_s_up[0].astype(BF16), w_s_down[0].astype(BF16)
    lamv = jnp.stack([lambda_q1[0], lambda_k1[0], lambda_q2[0], lambda_k2[0]])
    slopes = jnp.exp2(-8.0 * jnp.arange(1, n_heads + 1, dtype=F32) / n_heads)

    mod = _ada(jnp.concatenate([c_prompt, c_sample], axis=0), w_ada[0], b_ada[0])
    mod_p = [mod[:bp, i * d:(i + 1) * d] for i in range(6)]
    mod_s = [mod[bp:, i * d:(i + 1) * d] for i in range(6)]

    xp2, xs2 = x_prompt.reshape(tp, d), x_sample.reshape(ts, d)
    qscale = qk_half ** -0.5 * LOG2E
    qp, kp, vp, up, gp, kpb, vpb = _inproj(xp2, mod_p[0], mod_p[1], g_norm1[0], w_in_bf, seq, qscale)
    qs, ks, vs, us, gs, _, _ = _inproj(xs2, mod_s[0], mod_s[1], g_norm1[0], w_in_bf, n_new, qscale)

    att_p = _pattn(qp, kpb, vpb, slopes, lamv, g_subln[0], bp, seq, n_heads, lam_init)

    q5 = qs.reshape(bs, n_new, n_heads, 2, qk_half).transpose(0, 2, 3, 1, 4)
    eye_h = jnp.eye(n_heads, dtype=BF16)
    eye_c = jnp.eye(2, dtype=BF16)
    wq = (q5[:, :, :, :, None, None, :] * eye_h[None, :, None, None, :, None, None]
          * eye_c[None, None, :, None, None, :, None]).reshape(bs, n_heads * 2 * n_new, aw)
    n_pool, page = cache_k.shape[1], cache_k.shape[2]
    att_s = _sattn(wq, ks.reshape(bs, n_new, aw), vs.reshape(bs, n_new, aw),
                   cache_k.reshape(n_pool, page * n_heads, k_row), cache_v.reshape(n_pool, page * n_heads, v_head),
                   page_table, lamv, g_subln[0], n_heads, lam_init).reshape(ts, aw)

    rg_args = (conv_w[0], conv_b[0], wai_bf, b_rg_a[0], b_rg_i[0], rg_lambda[0], g_rgnorm[0])
    rec_p, h_p, conv_p = _rglru(up, gp, jnp.zeros((bp, CONV_WIDTH - 1, rw), F32), jnp.zeros((bp, rw), F32),
                                *rg_args, bp, seq)
    rec_s, h_s, conv_s = _rglru(us, gs, state_conv[0], state_h[0], *rg_args, bs, n_new)

    x1p, h2p_all, st_all = _oproj(att_p, rec_p, xp2, mod_p[2], mod_p[3], mod_p[4], g_norm2[0], wo_bf, wr_t,
                                  seq, 0, t_all)
    x1s, h2p_all, st_all = _oproj(att_s, rec_s, xs2, mod_s[2], mod_s[3], mod_s[4], g_norm2[0], wo_bf, wr_t,
                                  n_new, tp, t_all, h2p_all, st_all)

    idx, wts, rank, counts = _route(st_all, router_bias[0])

    tm = EXPERT_TILE_ROWS
    counts = counts.reshape(n_exp)
    ptiles = (counts + tm - 1) // tm
    pend = jnp.cumsum(ptiles)
    pstart = pend - ptiles
    n_tiles = (t_all * TOP_K) // tm + n_exp
    tile_ids = jnp.arange(n_tiles, dtype=I32)
    tile_e = jnp.minimum(jnp.sum(pend[None, :] <= tile_ids[:, None], axis=1), n_exp - 1).astype(I32)
    tile_nv = jnp.where(tile_ids < pend[-1],
                        jnp.clip(counts[tile_e] - (tile_ids - pstart[tile_e]) * tm, 0, tm), 0).astype(I32)
    tile_ord = (jnp.cumsum(ptiles > 0) - 1)[tile_e].astype(I32)
    next_tile = pend[tile_e]
    tile_next = jnp.where(next_tile < pend[-1], tile_e[jnp.minimum(next_tile, n_tiles - 1)], -1).astype(I32)
    n_copies = 3 * EXPERT_WEIGHT_DMA_CHUNKS
    tile_j, tile_n = tile_ids - pstart[tile_e], jnp.maximum(ptiles[tile_e], 1)
    tile_lo = ((n_copies * tile_j + tile_n - 1) // tile_n).astype(I32)
    tile_hi = ((n_copies * (tile_j + 1) + tile_n - 1) // tile_n).astype(I32)
    pos = _pos(idx, rank, pstart * tm)

    lanes = h2p_all.shape[1]
    tile3 = lambda a2: a2.reshape(-1, TILE_SUBLANES, lanes)
    xs_rows = _sc_scatter_rows(tile3(h2p_all), _tile_pos(pos, SC_CHUNK_ROWS), n_tiles * tm)
    ys = _experts(xs_rows.reshape(-1, lanes), (tile_e, tile_nv, tile_ord, tile_next, tile_lo, tile_hi),
                  w_e_gate[0], w_e_up[0], w_e_down[0])
    yg = _sc_gather_rows(tile3(ys), _tile_pos(pos, COMBINE_TILE).reshape(-1, SC_CHUNK_ROWS)).reshape(-1, lanes)

    wts_t = wts.T
    y_p = _combine(yg, wts_t[:tp], h2p_all, x1p, mod_p[5], g_final, wsg_bf, wsu_bf, wsd_bf, seq, 0)
    y_s = _combine(yg, wts_t[tp:], h2p_all, x1s, mod_s[5], g_final, wsg_bf, wsu_bf, wsd_bf, n_new, tp)

    return (y_p.reshape(bp, seq, d), y_s.reshape(bs, n_new, d),
            kp.reshape(1, bp, seq, n_heads, k_row), vp.reshape(1, bp, seq, n_heads, v_head),
            h_p.reshape(1, bp, rw), conv_p.reshape(1, bp, CONV_WIDTH - 1, rw),
            ks.reshape(1, bs, n_new, n_heads, k_row), vs.reshape(1, bs, n_new, n_heads, v_head),
            h_s.reshape(1, bs, rw), conv_s.reshape(1, bs, CONV_WIDTH - 1, rw))
```

```python
import functools
import math

import jax
import jax.numpy as jnp
import numpy as np
from jax import lax
from jax.experimental import pallas as pl
from jax.experimental.pallas import tpu as pltpu
from jax.experimental.pallas import tpu_sc as plsc

F32 = jnp.float32
BF16 = jnp.bfloat16
I32 = jnp.int32
U32 = jnp.uint32

NORM_EPS = 1e-6
SUBLN_EPS = 1e-5
NEG = -1e30
RG_C = 8.0
ROUTED_SCALE = 2.5
N_GROUPS = 8
TOPK_GROUPS = 4
TOP_K = 8
CONV_WIDTH = 4

V7X_VMEM_LIMIT_BYTES = 56 * 1024 * 1024
EXPERT_TILE_ROWS = 128
EXPERT_WEIGHT_DMA_CHUNKS = 8
WEIGHT_PREFETCH_DMA_PRIORITY = 1
V7X_SC_CORES = 2
V7X_SC_SUBCORES = 16
COMBINE_TILE = 128
SC_CHUNK_ROWS = 64


def _cparams(n_axes):
    return pltpu.CompilerParams(
        dimension_semantics=("arbitrary",) * n_axes, vmem_limit_bytes=V7X_VMEM_LIMIT_BYTES
    )


def _pick(n, candidates):
    for c in candidates:
        if n % c == 0:
            return c
    return n


def _dot(a, b):
    return jnp.dot(a, b, preferred_element_type=F32)


def _dot_nt(a, b):
    return lax.dot_general(a, b, (((1,), (1,)), ((), ())), preferred_element_type=F32)


def _split(x):
    hi = x.astype(BF16)
    lo = (x - hi.astype(F32)).astype(BF16)
    return hi, lo


def _rms(x, eps):
    return x * lax.rsqrt(jnp.mean(x * x, axis=-1, keepdims=True) + eps)


def _silu(x):
    return x * jax.nn.sigmoid(x)


def _bf16_terms(x, n):
    terms = []
    for _ in range(n):
        bits = np.float32(x).view(np.uint32)
        bits = (bits + np.uint32(0x7FFF) + ((bits >> np.uint32(16)) & np.uint32(1))) & np.uint32(0xFFFF0000)
        t = float(bits.view(np.float32))
        terms.append(t)
        x -= t
    return tuple(terms)


LOG2E = math.log2(math.e)
LOG2E_BF16_TERMS = _bf16_terms(LOG2E, 3)
TILE_SUBLANES = 8


def _rows_from_tiles(ref, n_rows):
    return jnp.concatenate(
        [ref[pl.ds(sub, n_rows, stride=TILE_SUBLANES), :] for sub in range(TILE_SUBLANES)], axis=-1)


def _rows_to_tiles(ref, x):
    n_rows, width = x.shape
    lanes = width // TILE_SUBLANES
    for sub in range(TILE_SUBLANES):
        ref[pl.ds(sub, n_rows, stride=TILE_SUBLANES), :] = x[:, sub * lanes:(sub + 1) * lanes]


def _unpack_halves(xp):
    lo = pltpu.unpack_elementwise(xp, index=0, packed_dtype=BF16, unpacked_dtype=F32)
    hi = pltpu.unpack_elementwise(xp, index=1, packed_dtype=BF16, unpacked_dtype=F32)
    return lo.astype(BF16), hi.astype(BF16)


def _pack_halves(x):
    n = x.shape[-1] // 2
    return pltpu.pack_elementwise([x[:, :n], x[:, n:]], packed_dtype=BF16)


def _ada_kernel(c_ref, w_ref, b_ref, o_ref):
    a_hi, a_lo = _split(_silu(c_ref[...]))
    w_hi, w_lo = _split(w_ref[...])
    o_ref[...] = _dot(a_hi, w_hi) + (_dot(a_hi, w_lo) + _dot(a_lo, w_hi)) + b_ref[...]


def _ada(c, w, b):
    n, d = c.shape
    d_out = w.shape[1]
    tn = _pick(d_out, (512, 256, 128))
    return pl.pallas_call(
        _ada_kernel,
        grid=(d_out // tn,),
        in_specs=[
            pl.BlockSpec((n, d), lambda j: (0, 0)),
            pl.BlockSpec((d, tn), lambda j: (0, j)),
            pl.BlockSpec((1, tn), lambda j: (0, j)),
        ],
        out_specs=pl.BlockSpec((n, tn), lambda j: (0, j)),
        out_shape=jax.ShapeDtypeStruct((n, d_out), F32),
        compiler_params=_cparams(1),
        name="ada",
    )(c, w, b.reshape(1, d_out))


def _mod_spec(rows_per_batch, tm, d, n_tiles=None):
    clamp = (lambda i: i) if n_tiles is None else (lambda i: jnp.minimum(i, n_tiles - 1))
    if rows_per_batch % tm == 0:
        per = rows_per_batch // tm
        return pl.BlockSpec((None, 1, d), lambda i, *_: (clamp(i) // per, 0, 0))
    return pl.BlockSpec((None, tm, d), lambda i, *_: (clamp(i), 0, 0))


def _mod_array(m, rows_per_batch, tm):
    nb, d = m.shape
    if rows_per_batch % tm == 0:
        return m.reshape(nb, 1, d)
    assert tm % rows_per_batch == 0
    return jnp.repeat(m, rows_per_batch, axis=0).reshape(nb * rows_per_batch // tm, tm, d)


def _inproj_kernel(x_ref, sh_ref, sc_ref, g_ref, w_ref,
                   q_ref, k_ref, v_ref, u_ref, gt_ref, kb_ref, vb_ref, h_scr, *, qscale):
    j = pl.program_id(1)

    @pl.when(j == 0)
    def _():
        y = _rms(x_ref[...], NORM_EPS) * g_ref[...]
        h_scr[...] = (y * (1.0 + sc_ref[...]) + sh_ref[...]).astype(BF16)

    z = _dot(h_scr[...], w_ref[...])

    @pl.when(j == 0)
    def _():
        q_ref[...] = (z * qscale).astype(BF16)

    @pl.when(j == 1)
    def _():
        k_ref[...] = z
        kb_ref[...] = z.astype(BF16)

    @pl.when(j == 2)
    def _():
        v_ref[...] = z
        vb_ref[...] = z.astype(BF16)

    @pl.when(j == 3)
    def _():
        u_ref[...] = z

    @pl.when(j == 4)
    def _():
        gt_ref[...] = z


def _inproj(x2d, shift, scale, g, w_bf, rows_per_batch, qscale):
    r, d = x2d.shape
    wd = w_bf.shape[1] // 5
    tm = _pick(r, (512, 256, 128, 64, 32, 16, 8))
    row = lambda i, j: (i, 0)
    f32o = jax.ShapeDtypeStruct((r, wd), F32)
    bfo = jax.ShapeDtypeStruct((r, wd), BF16)
    return pl.pallas_call(
        functools.partial(_inproj_kernel, qscale=qscale),
        grid=(r // tm, 5),
        in_specs=[
            pl.BlockSpec((tm, d), row),
            _mod_spec(rows_per_batch, tm, d),
            _mod_spec(rows_per_batch, tm, d),
            pl.BlockSpec((1, d), lambda i, j: (0, 0)),
            pl.BlockSpec((d, wd), lambda i, j: (0, j)),
        ],
        out_specs=[pl.BlockSpec((tm, wd), row)] * 7,
        out_shape=[bfo, f32o, f32o, f32o, f32o, bfo, bfo],
        scratch_shapes=[pltpu.VMEM((tm, d), BF16)],
        compiler_params=_cparams(2),
        name="inproj",
    )(x2d, _mod_array(shift, rows_per_batch, tm), _mod_array(scale, rows_per_batch, tm), g.reshape(1, d), w_bf)


def _lam(lamv_ref, lam_init):
    lv = lamv_ref[...]
    s1 = jnp.sum(lv[0:1] * lv[1:2], axis=-1, keepdims=True)
    s2 = jnp.sum(lv[2:3] * lv[3:4], axis=-1, keepdims=True)
    return jnp.exp(s1) - jnp.exp(s2) + lam_init


def _online_update(s, v, m_scr, l_scr, acc_scr, shift=None):
    m_old = m_scr[...]
    s_max = jnp.max(s, axis=-1, keepdims=True)
    m_new = jnp.maximum(m_old, s_max if shift is None else s_max + shift)
    alpha = jnp.exp2(m_old - m_new)
    p = jnp.exp2(s - (m_new if shift is None else m_new - shift))
    l_scr[...] = alpha * l_scr[...] + jnp.sum(p, axis=-1, keepdims=True)
    acc_scr[...] = alpha * acc_scr[...] + _dot(p.astype(BF16), v)
    m_scr[...] = m_new


def _pattn_kernel(slopes_ref, lamv_ref, gs_ref, q_ref, k_ref, v_ref, o_ref, q_scr, ka_scr, *state,
                  tq, rc, lam_init):
    n_chunks = 2 * tq // rc
    m_scrs, l_scrs, acc_scrs = state[:n_chunks], state[n_chunks:2 * n_chunks], state[2 * n_chunks:]
    h = pl.program_id(1)
    qi = pl.program_id(2)
    slope = slopes_ref[h]
    hd = q_ref.shape[1]
    half = hd // 2
    n_terms = len(LOG2E_BF16_TERMS)

    q = q_ref[...]
    lane = lax.broadcasted_iota(I32, q.shape, 1)
    zero = jnp.zeros_like(q)
    qa = jnp.zeros(q.shape, F32)
    for n, term in enumerate(LOG2E_BF16_TERMS):
        qa = jnp.where(jnp.logical_or(lane == n, lane == n + n_terms), term, qa)
    qa = qa.astype(BF16)
    q_scr[0:tq, 0:hd] = jnp.where(lane < half, q, zero)
    q_scr[tq:2 * tq, 0:hd] = jnp.where(lane >= half, q, zero)
    q_scr[0:tq, hd:2 * hd] = qa
    q_scr[tq:2 * tq, hd:2 * hd] = qa
    c = lax.broadcasted_iota(I32, (tq, hd), 0)
    c_lo = c % 256
    ka = jnp.where(lane < n_terms, c_lo.astype(F32) * slope,
                   jnp.where(lane < 2 * n_terms, (c - c_lo).astype(F32) * slope, 0.0))
    ka_scr[...] = ka.astype(BF16)

    for m_scr, l_scr, acc_scr in zip(m_scrs, l_scrs, acc_scrs):
        m_scr[...] = jnp.full(m_scr.shape, NEG, F32)
        l_scr[...] = jnp.zeros(l_scr.shape, F32)
        acc_scr[...] = jnp.zeros(acc_scr.shape, F32)

    def step(j, masked):
        start = pl.multiple_of(j * tq, tq)
        kaug = jnp.concatenate([k_ref[pl.ds(start, tq), :], ka_scr[...]], axis=-1)
        v = v_ref[pl.ds(start, tq), :]
        for ci in range(n_chunks):
            r0 = ci * rc
            q0 = r0 % tq
            ncol = q0 + rc if masked else tq
            rowpos = q0 + lax.broadcasted_iota(I32, (rc, 1), 0)
            s = _dot_nt(q_scr[r0:r0 + rc, :], kaug[:ncol])
            if masked:
                s = jnp.where(lax.broadcasted_iota(I32, (rc, ncol), 1) <= rowpos, s, NEG)
            shift = ((j - qi) * tq - rowpos).astype(F32) * (slope * LOG2E)
            _online_update(s, v[:ncol], m_scrs[ci], l_scrs[ci], acc_scrs[ci], shift=shift)

    def body(j, carry):
        step(j, False)
        return carry

    lax.fori_loop(0, qi, body, 0)
    step(qi, True)

    lam = _lam(lamv_ref, lam_init)
    o = jnp.concatenate([acc[...] / l[...] for acc, l in zip(acc_scrs, l_scrs)], axis=0)
    att = o[:tq] - lam * o[tq:]
    att = _rms(att, SUBLN_EPS) * gs_ref[...] * (1.0 - lam_init)
    o_ref[...] = att.astype(BF16)


def _pattn(qb, kb, vb, slopes, lamv, g_subln, n_batch, seq, n_heads, lam_init):
    r, aw = qb.shape
    hd = aw // n_heads
    tq = _pick(seq, (2048, 1024, 512, 256, 128))
    rc = min(tq, 256)
    n_chunks = 2 * tq // rc
    nq = seq // tq
    return pl.pallas_call(
        functools.partial(_pattn_kernel, tq=tq, rc=rc, lam_init=lam_init),
        grid_spec=pltpu.PrefetchScalarGridSpec(
            num_scalar_prefetch=1,
            grid=(n_batch, n_heads, nq),
            in_specs=[
                pl.BlockSpec(lamv.shape, lambda b, h, i, *_: (0, 0)),
                pl.BlockSpec((1, hd), lambda b, h, i, *_: (0, 0)),
                pl.BlockSpec((tq, hd), lambda b, h, i, *_: (b * nq + i, h)),
                pl.BlockSpec((seq, hd), lambda b, h, i, *_: (b, h)),
                pl.BlockSpec((seq, hd), lambda b, h, i, *_: (b, h)),
            ],
            out_specs=pl.BlockSpec((tq, hd), lambda b, h, i, *_: (b * nq + i, h)),
            scratch_shapes=[pltpu.VMEM((2 * tq, 2 * hd), BF16), pltpu.VMEM((tq, hd), BF16)]
            + [pltpu.VMEM((rc, 1), F32)] * (2 * n_chunks) + [pltpu.VMEM((rc, hd), F32)] * n_chunks,
        ),
        out_shape=jax.ShapeDtypeStruct((r, aw), BF16),
        compiler_params=_cparams(3),
        name="pattn",
    )(slopes, lamv, g_subln.reshape(1, hd), qb, kb, vb)


def _sattn_kernel(pt_ref, lamv_ref, gs_ref, wq_ref, kn_ref, vn_ref, *rest,
                  n_pages_step, page, past, n_new, n_heads, lam_init):
    k_refs = rest[:n_pages_step]
    v_refs = rest[n_pages_step:2 * n_pages_step]
    o_ref = rest[2 * n_pages_step]
    m_scr, l_scr, acc_scr = rest[2 * n_pages_step + 1:]
    j = pl.program_id(1)
    n_rows = wq_ref.shape[0]
    hd = wq_ref.shape[1] // n_heads
    tk = n_pages_step * page

    r = lax.broadcasted_iota(I32, (n_rows, 1), 0)
    head = r // (2 * n_new)
    qi = r % n_new
    slope = jnp.exp2(-(head + 1).astype(F32)) * LOG2E

    @pl.when(j == 0)
    def _():
        m_scr[...] = jnp.full(m_scr.shape, NEG, F32)
        l_scr[...] = jnp.zeros(l_scr.shape, F32)
        acc_scr[...] = jnp.zeros(acc_scr.shape, F32)

    wq = wq_ref[...]
    kc = jnp.concatenate([_rows_from_tiles(kr, page).astype(BF16) for kr in k_refs], axis=0)
    vc = jnp.concatenate([_rows_from_tiles(vr, page).astype(BF16) for vr in v_refs], axis=0)
    t = j * tk + lax.broadcasted_iota(I32, (1, tk), 1)
    s = _dot_nt(wq, kc) - slope * (past + qi - t).astype(F32)
    _online_update(s, vc, m_scr, l_scr, acc_scr)

    @pl.when(j == pl.num_programs(1) - 1)
    def _():
        pad = jnp.zeros((page - n_new, kn_ref.shape[1]), BF16)
        kn = jnp.concatenate([kn_ref[...].astype(BF16), pad], axis=0)
        vn = jnp.concatenate([vn_ref[...].astype(BF16), pad], axis=0)
        tj = lax.broadcasted_iota(I32, (1, page), 1)
        sn = _dot_nt(wq, kn) - slope * (qi - tj).astype(F32)
        sn = jnp.where(tj <= qi, sn, NEG)
        _online_update(sn, vn, m_scr, l_scr, acc_scr)

        lam = _lam(lamv_ref, lam_init)
        o = acc_scr[...] / l_scr[...]
        outs = []
        for h in range(n_heads):
            blk = o[h * 2 * n_new:(h + 1) * 2 * n_new, h * hd:(h + 1) * hd]
            att = blk[:n_new] - lam * blk[n_new:]
            outs.append(_rms(att, SUBLN_EPS) * gs_ref[...] * (1.0 - lam_init))
        o_ref[...] = jnp.concatenate(outs, axis=-1).astype(BF16)


def _sattn(wq, k_new, v_new, cache_k2, cache_v2, page_table, lamv, g_subln, n_heads, lam_init):
    n_seq, n_rows, aw = wq.shape
    n_new = k_new.shape[1]
    hd = aw // n_heads
    page = cache_k2.shape[1] // n_heads
    n_pages = page_table.shape[1]
    pstep = _pick(n_pages, (8, 4, 2, 1))
    past = n_pages * page
    assert n_heads == TILE_SUBLANES and hd == 128

    def page_spec(p):
        return pl.BlockSpec((None, page * n_heads, hd), lambda b, j, pt: (pt[b, j * pstep + p], 0, 0))

    seq_spec = lambda rows: pl.BlockSpec((None, rows, aw), lambda b, j, pt: (b, 0, 0))
    return pl.pallas_call(
        functools.partial(_sattn_kernel, n_pages_step=pstep, page=page, past=past, n_new=n_new,
                          n_heads=n_heads, lam_init=lam_init),
        grid_spec=pltpu.PrefetchScalarGridSpec(
            num_scalar_prefetch=1,
            grid=(n_seq, n_pages // pstep),
            in_specs=[
                pl.BlockSpec(lamv.shape, lambda b, j, pt: (0, 0)),
                pl.BlockSpec((1, hd), lambda b, j, pt: (0, 0)),
                seq_spec(n_rows), seq_spec(n_new), seq_spec(n_new),
            ] + [page_spec(p) for p in range(pstep)] * 2,
            out_specs=seq_spec(n_new),
            scratch_shapes=[
                pltpu.VMEM((n_rows, 1), F32),
                pltpu.VMEM((n_rows, 1), F32),
                pltpu.VMEM((n_rows, aw), F32),
            ],
        ),
        out_shape=jax.ShapeDtypeStruct((n_seq, n_new, aw), BF16),
        compiler_params=_cparams(2),
        name="sattn",
    )(page_table, lamv, g_subln.reshape(1, hd), wq, k_new, v_new,
      *([cache_k2] * pstep), *([cache_v2] * pstep))


def _gelu_tanh(x):
    return x * (0.5 * (1.0 + jnp.tanh(math.sqrt(2.0 / math.pi) * (x + 0.044715 * (x * x * x)))))


def _softplus(x):
    return jnp.maximum(x, 0.0) + jnp.log1p(jnp.exp(-jnp.abs(x)))


def _rglru_kernel(u_ref, gt_ref, c0_ref, h0_ref, cw_ref, cb_ref, wai_ref, ba_ref, bi_ref, lam_ref, gn_ref,
                  rec_ref, ht_ref, cout_ref, ubuf, hcar, *, tl, n_blocks):
    t = pl.program_id(1)
    halo = CONV_WIDTH - 1
    base = 8

    @pl.when(t == 0)
    def _():
        ubuf[base - halo:base, :] = c0_ref[...]
        hcar[...] = h0_ref[...]

    ubuf[base:base + tl, :] = u_ref[...]
    cw = cw_ref[...]
    xc = cb_ref[...] + cw[0:1] * ubuf[base - halo:base - halo + tl, :]
    for jj in range(1, CONV_WIDTH):
        xc = xc + cw[jj:jj + 1] * ubuf[base - halo + jj:base - halo + jj + tl, :]
    tail = ubuf[base + tl - halo:base + tl, :]
    ubuf[base - halo:base, :] = tail
    cout_ref[...] = tail

    bw = xc.shape[1] // n_blocks
    za, zi = [], []
    for n in range(n_blocks):
        z = _dot(xc[:, n * bw:(n + 1) * bw].astype(BF16), wai_ref[n])
        za.append(z[:, :bw])
        zi.append(z[:, bw:])
    r = jax.nn.sigmoid(jnp.concatenate(za, axis=-1) + ba_ref[...])
    i = jax.nn.sigmoid(jnp.concatenate(zi, axis=-1) + bi_ref[...])
    log_a = -RG_C * r * _softplus(-lam_ref[...])
    a = jnp.exp(log_a)
    th = jnp.tanh(log_a)
    b = xc * i * jnp.sqrt(-2.0 * th / (1.0 - th))

    rowi = lax.broadcasted_iota(I32, a.shape, 0)
    sft = 1
    while sft < tl:
        keep = rowi >= sft
        a_prev = jnp.where(keep, pltpu.roll(a, sft, 0), 1.0)
        b_prev = jnp.where(keep, pltpu.roll(b, sft, 0), 0.0)
        b = a * b_prev + b
        a = a * a_prev
        sft *= 2
    hs = a * hcar[...] + b
    h_last = hs[tl - 1:tl, :]
    hcar[...] = h_last
    ht_ref[...] = h_last

    rec = hs * _gelu_tanh(gt_ref[...])
    rec_ref[...] = (_rms(rec, NORM_EPS) * gn_ref[...]).astype(BF16)


def _rglru(u2d, gate2d, conv0, h0, conv_w, conv_b, wai_bf, b_a, b_i, rg_lambda, g_rgnorm, n_batch, seq):
    r, w = u2d.shape
    tl = _pick(seq, (256, 128, 64, 32, 16, 8))
    nt = seq // tl
    n_blocks = wai_bf.shape[0]
    halo = CONV_WIDTH - 1
    row = lambda b, t: (b * nt + t, 0)
    vec = pl.BlockSpec((1, w), lambda b, t: (0, 0))
    rec, ht, cout = pl.pallas_call(
        functools.partial(_rglru_kernel, tl=tl, n_blocks=n_blocks),
        grid=(n_batch, nt),
        in_specs=[
            pl.BlockSpec((tl, w), row),
            pl.BlockSpec((tl, w), row),
            pl.BlockSpec((None, halo, w), lambda b, t: (b, 0, 0)),
            pl.BlockSpec((None, 1, w), lambda b, t: (b, 0, 0)),
            pl.BlockSpec((CONV_WIDTH, w), lambda b, t: (0, 0)),
            vec,
            pl.BlockSpec(wai_bf.shape, lambda b, t: (0, 0, 0)),
            vec, vec, vec, vec,
        ],
        out_specs=[
            pl.BlockSpec((tl, w), row),
            pl.BlockSpec((None, 1, w), lambda b, t: (b, 0, 0)),
            pl.BlockSpec((None, halo, w), lambda b, t: (b, 0, 0)),
        ],
        out_shape=[
            jax.ShapeDtypeStruct((r, w), BF16),
            jax.ShapeDtypeStruct((n_batch, 1, w), F32),
            jax.ShapeDtypeStruct((n_batch, halo, w), F32),
        ],
        scratch_shapes=[pltpu.VMEM((tl + 8, w), F32), pltpu.VMEM((1, w), F32)],
        compiler_params=_cparams(2),
        name="rglru",
    )(u2d, gate2d, conv0, h0.reshape(n_batch, 1, w), conv_w, conv_b.reshape(1, w), wai_bf,
      b_a.reshape(1, w), b_i.reshape(1, w), rg_lambda.reshape(1, w), g_rgnorm.reshape(1, w))
    return rec, ht.reshape(n_batch, w), cout


def _oproj_kernel(att_ref, rec_ref, x_ref, g1_ref, sh_ref, sc_ref, gn_ref, wo_ref, wrt_ref, *rest,
                  aliased, n_tiles):
    x1_ref, h2p_ref, st_ref = rest[2:] if aliased else rest
    aw = att_ref.shape[1]
    i = pl.program_id(0)

    @pl.when(i < n_tiles)
    def _():
        mix = _dot(att_ref[...], wo_ref[:aw, :]) + _dot(rec_ref[...], wo_ref[aw:, :])
        x1 = x_ref[...] + g1_ref[...] * mix
        x1_ref[...] = x1
        h2 = (_rms(x1, NORM_EPS) * gn_ref[...]) * (1.0 + sc_ref[...]) + sh_ref[...]
        _rows_to_tiles(h2p_ref, _pack_halves(h2))
        h_hi, h_lo = _split(h2)
        w_hi, w_lo = _split(wrt_ref[...])
        logits_t = _dot_nt(w_hi, h_hi) + (_dot_nt(w_hi, h_lo) + _dot_nt(w_lo, h_hi))
        st_ref[...] = jax.nn.sigmoid(logits_t)

    @pl.when(i >= n_tiles)
    def _():
        h2p_ref[...] = jnp.zeros(h2p_ref.shape, U32)
        st_ref[...] = jnp.zeros(st_ref.shape, F32)


def _oproj(att, rec, x2d, g1, shift, scale, g_norm2, wo_bf, wr_t, rows_per_batch, row_offset, total_rows,
           h2p_all=None, st_all=None):
    r, d = x2d.shape
    aw = att.shape[1]
    n_exp = wr_t.shape[0]
    tm = _pick(r, (256, 128))
    assert row_offset % tm == 0
    off = row_offset // tm
    lanes = d // 2 // TILE_SUBLANES
    assert lanes == 128
    aliased = h2p_all is not None
    n_tiles = r // tm
    n_fill = 0 if aliased else (total_rows - r) // tm
    assert aliased or (row_offset == 0 and (total_rows - r) % tm == 0)
    row = lambda i: (jnp.minimum(i, n_tiles - 1), 0)
    mspec = _mod_spec(rows_per_batch, tm, d, n_tiles)
    in_specs = [
        pl.BlockSpec((tm, aw), row),
        pl.BlockSpec((tm, d - aw), row),
        pl.BlockSpec((tm, d), row),
        mspec, mspec, mspec,
        pl.BlockSpec((1, d), lambda i: (0, 0)),
        pl.BlockSpec((d, d), lambda i: (0, 0)),
        pl.BlockSpec((n_exp, d), lambda i: (0, 0)),
    ]
    args = [att, rec, x2d, _mod_array(g1, rows_per_batch, tm), _mod_array(shift, rows_per_batch, tm),
            _mod_array(scale, rows_per_batch, tm), g_norm2.reshape(1, d), wo_bf, wr_t]
    io_alias = {}
    if aliased:
        in_specs += [pl.BlockSpec(memory_space=pl.ANY), pl.BlockSpec(memory_space=pl.ANY)]
        io_alias = {len(args): 1, len(args) + 1: 2}
        args += [h2p_all, st_all]
    return pl.pallas_call(
        functools.partial(_oproj_kernel, aliased=aliased, n_tiles=n_tiles),
        grid=(n_tiles + n_fill,),
        in_specs=in_specs,
        out_specs=[
            pl.BlockSpec((tm, d), row),
            pl.BlockSpec((tm * TILE_SUBLANES, lanes), lambda i: (i + off, 0)),
            pl.BlockSpec((n_exp, tm), lambda i: (0, i + off)),
        ],
        out_shape=[
            jax.ShapeDtypeStruct((r, d), F32),
            jax.ShapeDtypeStruct((total_rows * TILE_SUBLANES, lanes), U32),
            jax.ShapeDtypeStruct((n_exp, total_rows), F32),
        ],
        input_output_aliases=io_alias,
        compiler_params=_cparams(1),
        name="oproj",
    )(*args)


def _route_kernel(st_ref, rb_ref, idx_ref, wts_ref, rank_ref, cnt_ref, carry):
    i = pl.program_id(0)
    n_exp, tr = st_ref.shape
    gsz = n_exp // N_GROUPS

    @pl.when(i == 0)
    def _():
        carry[...] = jnp.zeros(carry.shape, F32)

    s = st_ref[...]
    biased = s + rb_ref[...]
    g = biased.reshape(N_GROUPS, gsz, tr)
    within = lax.broadcasted_iota(I32, g.shape, 1)
    m1 = jnp.max(g, axis=1, keepdims=True)
    first = jnp.min(jnp.where(g == m1, within, gsz), axis=1, keepdims=True)
    m2 = jnp.max(jnp.where(within == first, -jnp.inf, g), axis=1, keepdims=True)
    gscore = (m1 + m2).reshape(N_GROUPS, tr)

    gidx = lax.broadcasted_iota(I32, gscore.shape, 0)
    gsel = jnp.zeros(gscore.shape, F32)
    for _ in range(TOPK_GROUPS):
        mg = jnp.max(gscore, axis=0, keepdims=True)
        fg = jnp.min(jnp.where(gscore == mg, gidx, N_GROUPS), axis=0, keepdims=True)
        hit = gidx == fg
        gsel = jnp.where(hit, 1.0, gsel)
        gscore = jnp.where(hit, -jnp.inf, gscore)
    masked = jnp.where(gsel.reshape(N_GROUPS, 1, tr) > 0.5, g, -jnp.inf).reshape(n_exp, tr)

    eidx = lax.broadcasted_iota(I32, (n_exp, tr), 0)
    idxs, ws = [], []
    chosen = jnp.zeros((n_exp, tr), jnp.bool_)
    for _ in range(TOP_K):
        mv = jnp.max(masked, axis=0, keepdims=True)
        fe = jnp.min(jnp.where(masked == mv, eidx, n_exp), axis=0, keepdims=True)
        hit = eidx == fe
        idxs.append(fe)
        ws.append(jnp.sum(jnp.where(hit, s, 0.0), axis=0, keepdims=True))
        chosen = chosen | hit
        masked = jnp.where(hit, -jnp.inf, masked)
    idx = jnp.concatenate(idxs, axis=0)
    w = jnp.concatenate(ws, axis=0)
    idx_ref[...] = idx
    wts_ref[...] = w / jnp.sum(w, axis=0, keepdims=True) * ROUTED_SCALE

    cmat = jnp.where(chosen, 1.0, 0.0)
    before = lax.broadcasted_iota(I32, (tr, tr), 0) < lax.broadcasted_iota(I32, (tr, tr), 1)
    prior = _dot(cmat.astype(BF16), jnp.where(before, 1.0, 0.0).astype(BF16)) + carry[...]
    ranks = [jnp.sum(jnp.where(eidx == idxs[k], prior, 0.0), axis=0, keepdims=True) for k in range(TOP_K)]
    rank_ref[...] = jnp.concatenate(ranks, axis=0).astype(I32)
    carry[...] = carry[...] + jnp.sum(cmat, axis=1, keepdims=True)
    cnt_ref[...] = carry[...].astype(I32)


def _route(st_all, router_bias):
    n_exp, t_all = st_all.shape
    tr = _pick(t_all, (640, 512, 256, 128))
    col = lambda i: (0, i)
    o8 = lambda dt: jax.ShapeDtypeStruct((TOP_K, t_all), dt)
    return pl.pallas_call(
        _route_kernel,
        grid=(t_all // tr,),
        in_specs=[pl.BlockSpec((n_exp, tr), col), pl.BlockSpec((n_exp, 1), lambda i: (0, 0))],
        out_specs=[pl.BlockSpec((TOP_K, tr), col)] * 3 + [pl.BlockSpec((n_exp, 1), lambda i: (0, 0))],
        out_shape=[o8(I32), o8(F32), o8(I32), jax.ShapeDtypeStruct((n_exp, 1), I32)],
        scratch_shapes=[pltpu.VMEM((n_exp, 1), F32)],
        compiler_params=_cparams(1),
        name="route",
    )(st_all, router_bias.reshape(n_exp, 1))


def _pos_kernel(idx_ref, rank_ref, start_ref, pos_ref):
    n_exp = start_ref.shape[0]
    tr = idx_ref.shape[1]
    eidx = lax.broadcasted_iota(I32, (n_exp, tr), 0)
    start = start_ref[...]
    rows = [jnp.sum(jnp.where(eidx == idx_ref[k:k + 1, :], start, 0.0), axis=0, keepdims=True)
            for k in range(TOP_K)]
    pos_ref[...] = jnp.concatenate(rows, axis=0).astype(I32) + rank_ref[...]


def _pos(idx, rank, start_rows):
    n_exp = start_rows.shape[0]
    t_all = idx.shape[1]
    tr = _pick(t_all, (640, 512, 256, 128))
    col = pl.BlockSpec((TOP_K, tr), lambda i: (0, i))
    return pl.pallas_call(
        _pos_kernel,
        grid=(t_all // tr,),
        in_specs=[col, col, pl.BlockSpec((n_exp, 1), lambda i: (0, 0))],
        out_specs=col,
        out_shape=jax.ShapeDtypeStruct((TOP_K, t_all), I32),
        compiler_params=_cparams(1),
        name="pos",
    )(idx, rank, start_rows.astype(F32).reshape(n_exp, 1))


def _tile_pos(pos, tile):
    k, t = pos.shape
    return pos.reshape(k, t // tile, tile).transpose(1, 0, 2)


def _sc_worker_chunks(n_chunks, fn):
    n_workers = V7X_SC_CORES * V7X_SC_SUBCORES
    worker = lax.axis_index("core") * V7X_SC_SUBCORES + lax.axis_index("subcore")

    @pl.loop(0, -(-n_chunks // n_workers))
    def _(it):
        chunk = it * n_workers + worker

        @pl.when(chunk < n_chunks)
        def _():
            fn(chunk)


def _sc_mesh():
    return plsc.VectorSubcoreMesh(core_axis_name="core", subcore_axis_name="subcore",
                                  num_cores=V7X_SC_CORES, num_subcores=V7X_SC_SUBCORES)


def _sc_scatter_rows(x3, idx3, n_rows):
    n_chunks, n_k, width = idx3.shape
    row = x3.shape[1:]

    @pl.kernel(out_type=jax.ShapeDtypeStruct((n_rows,) + row, x3.dtype), mesh=_sc_mesh(),
               scratch_types=[pltpu.VMEM((width,) + row, x3.dtype), pltpu.VMEM((n_k, width), I32),
                              pltpu.SemaphoreType.DMA],
               name="dispatch_sc")
    def scatter(x_hbm, i_hbm, o_hbm, xbuf, ibuf, sem):
        def one(chunk):
            pltpu.sync_copy(x_hbm.at[pl.ds(chunk * width, width)], xbuf)
            pltpu.sync_copy(i_hbm.at[chunk], ibuf)
            copies = [pltpu.make_async_copy(xbuf, o_hbm.at[ibuf.at[k]], sem) for k in range(n_k)]
            for cp in copies:
                cp.start()
            for cp in copies:
                cp.wait()

        _sc_worker_chunks(n_chunks, one)

    return scatter(x3, idx3)


def _sc_gather_rows(src3, idx2):
    n_chunks, width = idx2.shape
    row = src3.shape[1:]

    @pl.kernel(out_type=jax.ShapeDtypeStruct((n_chunks * width,) + row, src3.dtype), mesh=_sc_mesh(),
               scratch_types=[pltpu.VMEM((width,) + row, src3.dtype), pltpu.VMEM((width,), I32),
                              pltpu.SemaphoreType.DMA],
               name="combine_sc")
    def gather(s_hbm, i_hbm, o_hbm, buf, ibuf, sem):
        def one(chunk):
            pltpu.sync_copy(i_hbm.at[chunk], ibuf)
            pltpu.async_copy(s_hbm.at[ibuf], buf, sem).wait()
            pltpu.sync_copy(buf, o_hbm.at[pl.ds(chunk * width, width)])

        _sc_worker_chunks(n_chunks, one)

    return gather(src3, idx2)


def _swiglu_packed(xp, wg, wu, wd):
    xa, xb = _unpack_halves(xp)
    half = xp.shape[1]
    g = _dot(xa, wg[:half, :]) + _dot(xb, wg[half:, :])
    u = _dot(xa, wu[:half, :]) + _dot(xb, wu[half:, :])
    return _dot((_silu(g) * u).astype(BF16), wd[...])


def _experts_kernel(te_ref, tnv_ref, tord_ref, tnext_ref, tlo_ref, thi_ref, xs_ref, wg_hbm, wu_hbm, wd_hbm, ys_ref,
                    wg_f, wu_f, wd_f, wg_b, wu_b, wd_b, sems):
    i = pl.program_id(0)
    nv = tnv_ref[i]
    expert = te_ref[i]
    new_expert = jnp.logical_or(i == 0, expert != te_ref[jnp.maximum(i - 1, 0)])
    slot = tord_ref[i] % 2
    tm = ys_ref.shape[0] // TILE_SUBLANES

    def weight_copies(e, sl):
        copies = []
        for n, (hbm, buf) in enumerate(((wg_hbm, wg_f), (wu_hbm, wu_f), (wd_hbm, wd_f))):
            rows = hbm.shape[1] // EXPERT_WEIGHT_DMA_CHUNKS
            for c in range(EXPERT_WEIGHT_DMA_CHUNKS):
                span = pl.ds(c * rows, rows)
                copies.append(pltpu.make_async_copy(hbm.at[e, span], buf.at[sl, span], sems.at[n, sl]))
        return copies

    @pl.when(jnp.logical_and(nv > 0, new_expert))
    def _():
        @pl.when(i == 0)
        def _():
            for cp in weight_copies(expert, slot):
                cp.start()

        for cp in weight_copies(expert, slot):
            cp.wait()

    nxt = tnext_ref[i]

    @pl.when(jnp.logical_and(nv > 0, nxt >= 0))
    def _():
        lo, hi = tlo_ref[i], thi_ref[i]
        for c, cp in enumerate(weight_copies(nxt, 1 - slot)):
            @pl.when(jnp.logical_and(lo <= c, c < hi))
            def _():
                cp.start(priority=WEIGHT_PREFETCH_DMA_PRIORITY)

    @pl.when(jnp.logical_and(nv > 0, new_expert))
    def _():
        wg_b[...] = wg_f[slot].astype(BF16)
        wu_b[...] = wu_f[slot].astype(BF16)
        wd_b[...] = wd_f[slot].astype(BF16)

    @pl.when(nv > 0)
    def _():
        xp = _rows_from_tiles(xs_ref, tm)
        rowi = lax.broadcasted_iota(I32, xp.shape, 0)
        xp = jnp.where(rowi < nv, xp, jnp.uint32(0))
        _rows_to_tiles(ys_ref, _pack_halves(_swiglu_packed(xp, wg_b, wu_b, wd_b)))

    @pl.when(nv == 0)
    def _():
        ys_ref[...] = jnp.zeros(ys_ref.shape, U32)


def _experts(xs, tile_tables, w_gate, w_up, w_down):
    rows, lanes = xs.shape
    tm = EXPERT_TILE_ROWS
    n_exp, d, ff = w_gate.shape
    blk = pl.BlockSpec((tm * TILE_SUBLANES, lanes), lambda i, *_: (i, 0))
    hbm = pl.BlockSpec(memory_space=pl.ANY)
    return pl.pallas_call(
        _experts_kernel,
        grid_spec=pltpu.PrefetchScalarGridSpec(
            num_scalar_prefetch=len(tile_tables),
            grid=(rows // (tm * TILE_SUBLANES),),
            in_specs=[blk, hbm, hbm, hbm],
            out_specs=blk,
            scratch_shapes=[
                pltpu.VMEM((2, d, ff), F32), pltpu.VMEM((2, d, ff), F32), pltpu.VMEM((2, ff, d), F32),
                pltpu.VMEM((d, ff), BF16), pltpu.VMEM((d, ff), BF16), pltpu.VMEM((ff, d), BF16),
                pltpu.SemaphoreType.DMA((3, 2)),
            ],
        ),
        out_shape=jax.ShapeDtypeStruct((rows, lanes), U32),
        compiler_params=_cparams(1),
        name="experts",
    )(*tile_tables, xs, w_gate, w_up, w_down)


def _combine_kernel(w_ref, yg_ref, h2p_ref, x1_ref, g2_ref, gf_ref, wsg_ref, wsu_ref, wsd_ref, o_ref, *, tc):
    shared = _swiglu_packed(_rows_from_tiles(h2p_ref, tc), wsg_ref, wsu_ref, wsd_ref)
    w = w_ref[...]
    lo, hi = None, None
    for k in range(TOP_K):
        rows = tc * TILE_SUBLANES
        yk = _rows_from_tiles(yg_ref.at[pl.ds(k * rows, rows)], tc)
        wk = w[:, k:k + 1]
        yl = pltpu.unpack_elementwise(yk, index=0, packed_dtype=BF16, unpacked_dtype=F32) * wk
        yh = pltpu.unpack_elementwise(yk, index=1, packed_dtype=BF16, unpacked_dtype=F32) * wk
        lo = yl if lo is None else lo + yl
        hi = yh if hi is None else hi + yh
    routed = jnp.concatenate([lo, hi], axis=-1)
    x2 = x1_ref[...] + g2_ref[...] * (routed + shared)
    o_ref[...] = _rms(x2, NORM_EPS) * gf_ref[...]


def _combine(yg, wts_t, h2p_all, x1, g2, g_final, wsg_bf, wsu_bf, wsd_bf, rows_per_batch, row_offset):
    r, d = x1.shape
    lanes = h2p_all.shape[1]
    tc = COMBINE_TILE
    assert r % tc == 0 and row_offset % tc == 0
    off = row_offset // tc
    ff = wsg_bf.shape[1]
    const = lambda shp: pl.BlockSpec(shp, lambda i: (0,) * len(shp))
    return pl.pallas_call(
        functools.partial(_combine_kernel, tc=tc),
        grid=(r // tc,),
        in_specs=[
            pl.BlockSpec((tc, TOP_K), lambda i: (i, 0)),
            pl.BlockSpec((TOP_K * tc * TILE_SUBLANES, lanes), lambda i: (i + off, 0)),
            pl.BlockSpec((tc * TILE_SUBLANES, lanes), lambda i: (i + off, 0)),
            pl.BlockSpec((tc, d), lambda i: (i, 0)),
            _mod_spec(rows_per_batch, tc, d),
            const((1, d)), const((d, ff)), const((d, ff)), const((ff, d)),
        ],
        out_specs=pl.BlockSpec((tc, d), lambda i: (i, 0)),
        out_shape=jax.ShapeDtypeStruct((r, d), F32),
        compiler_params=_cparams(1),
        name="combine",
    )(wts_t, yg, h2p_all, x1, _mod_array(g2, rows_per_batch, tc), g_final.reshape(1, d),
      wsg_bf, wsu_bf, wsd_bf)


def kernel(x_prompt, x_sample, cache_k, cache_v, state_h, state_conv, page_table, c_prompt, c_sample,
           w_ada, b_ada, g_norm1, w_in, lambda_q1, lambda_k1, lambda_q2, lambda_k2, g_subln,
           conv_w, conv_b, w_rg_a, b_rg_a, w_rg_i, b_rg_i, rg_lambda, g_rgnorm, w_o, g_norm2,
           w_router, router_bias, w_e_gate, w_e_up, w_e_down, w_s_gate, w_s_up, w_s_down, g_final):
    depth = w_ada.shape[0]
    assert depth == 1, "single-layer step"
    bp, seq, d = x_prompt.shape
    bs, n_new, _ = x_sample.shape
    n_heads = cache_k.shape[3]
    k_row = cache_k.shape[4]
    v_head = cache_v.shape[4]
    aw = n_heads * v_head
    rw = d - aw
    assert k_row == v_head and w_in.shape[2] == 3 * aw + 2 * rw and aw == rw
    qk_half = k_row // 2
    n_exp = w_router.shape[2]
    lam_init = 0.8 - 0.6 * math.exp(-0.3 * 0)
    tp, ts = bp * seq, bs * n_new
    t_all = tp + ts

    w_in_bf = w_in[0].astype(BF16)
    wo_bf = w_o[0].astype(BF16)
    wr_t = w_router[0].T
    wai_bf = jnp.concatenate([w_rg_a[0], w_rg_i[0]], axis=-1).astype(BF16)
    wsg_bf, wsu_bf, wsd_bf = w_s_gate[0].astype(BF16), w_s_up[0].astype(BF16), w_s_down[0].astype(BF16)
    lamv = jnp.stack([lambda_q1[0], lambda_k1[0], lambda_q2[0], lambda_k2[0]])
    slopes = jnp.exp2(-8.0 * jnp.arange(1, n_heads + 1, dtype=F32) / n_heads)

    mod = _ada(jnp.concatenate([c_prompt, c_sample], axis=0), w_ada[0], b_ada[0])
    mod_p = [mod[:bp, i * d:(i + 1) * d] for i in range(6)]
    mod_s = [mod[bp:, i * d:(i + 1) * d] for i in range(6)]

    xp2, xs2 = x_prompt.reshape(tp, d), x_sample.reshape(ts, d)
    qscale = qk_half ** -0.5 * LOG2E
    qp, kp, vp, up, gp, kpb, vpb = _inproj(xp2, mod_p[0], mod_p[1], g_norm1[0], w_in_bf, seq, qscale)
    qs, ks, vs, us, gs, _, _ = _inproj(xs2, mod_s[0], mod_s[1], g_norm1[0], w_in_bf, n_new, qscale)

    att_p = _pattn(qp, kpb, vpb, slopes, lamv, g_subln[0], bp, seq, n_heads, lam_init)

    q5 = qs.reshape(bs, n_new, n_heads, 2, qk_half).transpose(0, 2, 3, 1, 4)
    eye_h = jnp.eye(n_heads, dtype=BF16)
    eye_c = jnp.eye(2, dtype=BF16)
    wq = (q5[:, :, :, :, None, None, :] * eye_h[None, :, None, None, :, None, None]
          * eye_c[None, None, :, None, None, :, None]).reshape(bs, n_heads * 2 * n_new, aw)
    n_pool, page = cache_k.shape[1], cache_k.shape[2]
    att_s = _sattn(wq, ks.reshape(bs, n_new, aw), vs.reshape(bs, n_new, aw),
                   cache_k.reshape(n_pool, page * n_heads, k_row), cache_v.reshape(n_pool, page * n_heads, v_head),
                   page_table, lamv, g_subln[0], n_heads, lam_init).reshape(ts, aw)

    rg_args = (conv_w[0], conv_b[0], wai_bf, b_rg_a[0], b_rg_i[0], rg_lambda[0], g_rgnorm[0])
    rec_p, h_p, conv_p = _rglru(up, gp, jnp.zeros((bp, CONV_WIDTH - 1, rw), F32), jnp.zeros((bp, rw), F32),
                                *rg_args, bp, seq)
    rec_s, h_s, conv_s = _rglru(us, gs, state_conv[0], state_h[0], *rg_args, bs, n_new)

    x1p, h2p_all, st_all = _oproj(att_p, rec_p, xp2, mod_p[2], mod_p[3], mod_p[4], g_norm2[0], wo_bf, wr_t,
                                  seq, 0, t_all)
    x1s, h2p_all, st_all = _oproj(att_s, rec_s, xs2, mod_s[2], mod_s[3], mod_s[4], g_norm2[0], wo_bf, wr_t,
                                  n_new, tp, t_all, h2p_all, st_all)

    idx, wts, rank, counts = _route(st_all, router_bias[0])

    tm = EXPERT_TILE_ROWS
    counts = counts.reshape(n_exp)
    ptiles = (counts + tm - 1) // tm
    pend = jnp.cumsum(ptiles)
    pstart = pend - ptiles
    n_tiles = (t_all * TOP_K) // tm + n_exp
    tile_ids = jnp.arange(n_tiles, dtype=I32)
    tile_e = jnp.minimum(jnp.sum(pend[None, :] <= tile_ids[:, None], axis=1), n_exp - 1).astype(I32)
    tile_nv = jnp.where(tile_ids < pend[-1],
                        jnp.clip(counts[tile_e] - (tile_ids - pstart[tile_e]) * tm, 0, tm), 0).astype(I32)
    tile_ord = (jnp.cumsum(ptiles > 0) - 1)[tile_e].astype(I32)
    next_tile = pend[tile_e]
    tile_next = jnp.where(next_tile < pend[-1], tile_e[jnp.minimum(next_tile, n_tiles - 1)], -1).astype(I32)
    n_copies = 3 * EXPERT_WEIGHT_DMA_CHUNKS
    tile_j, tile_n = tile_ids - pstart[tile_e], jnp.maximum(ptiles[tile_e] - 1, 1)
    tile_lo = ((n_copies * tile_j + tile_n - 1) // tile_n).astype(I32)
    tile_hi = ((n_copies * (tile_j + 1) + tile_n - 1) // tile_n).astype(I32)
    pos = _pos(idx, rank, pstart * tm)

    lanes = h2p_all.shape[1]
    tile3 = lambda a2: a2.reshape(-1, TILE_SUBLANES, lanes)
    xs_rows = _sc_scatter_rows(tile3(h2p_all), _tile_pos(pos, SC_CHUNK_ROWS), n_tiles * tm)
    ys = _experts(xs_rows.reshape(-1, lanes), (tile_e, tile_nv, tile_ord, tile_next, tile_lo, tile_hi),
                  w_e_gate[0], w_e_up[0], w_e_down[0])
    yg = _sc_gather_rows(tile3(ys), _tile_pos(pos, COMBINE_TILE).reshape(-1, SC_CHUNK_ROWS)).reshape(-1, lanes)

    wts_t = wts.T
    y_p = _combine(yg, wts_t[:tp], h2p_all, x1p, mod_p[5], g_final, wsg_bf, wsu_bf, wsd_bf, seq, 0)
    y_s = _combine(yg, wts_t[tp:], h2p_all, x1s, mod_s[5], g_final, wsg_bf, wsu_bf, wsd_bf, n_new, tp)

    return (y_p.reshape(bp, seq, d), y_s.reshape(bs, n_new, d),
            kp.reshape(1, bp, seq, n_heads, k_row), vp.reshape(1, bp, seq, n_heads, v_head),
            h_p.reshape(1, bp, rw), conv_p.reshape(1, bp, CONV_WIDTH - 1, rw),
            ks.reshape(1, bs, n_new, n_heads, k_row), vs.reshape(1, bs, n_new, n_heads, v_head),
            h_s.reshape(1, bs, rw), conv_s.reshape(1, bs, CONV_WIDTH - 1, rw))
```

```python
import functools
import math

import jax
import jax.numpy as jnp
import numpy as np
from jax import lax
from jax.experimental import pallas as pl
from jax.experimental.pallas import tpu as pltpu
from jax.experimental.pallas import tpu_sc as plsc

F32 = jnp.float32
BF16 = jnp.bfloat16
I32 = jnp.int32
U32 = jnp.uint32

NORM_EPS = 1e-6
SUBLN_EPS = 1e-5
NEG = -1e30
RG_C = 8.0
ROUTED_SCALE = 2.5
N_GROUPS = 8
TOPK_GROUPS = 4
TOP_K = 8
CONV_WIDTH = 4

V7X_VMEM_LIMIT_BYTES = 56 * 1024 * 1024
EXPERT_TILE_ROWS = 256
EXPERT_WEIGHT_DMA_CHUNKS = 8
WEIGHT_PREFETCH_DMA_PRIORITY = 1
V7X_SC_CORES = 2
V7X_SC_SUBCORES = 16
COMBINE_TILE = 128
SC_CHUNK_ROWS = 64


def _cparams(n_axes):
    return pltpu.CompilerParams(
        dimension_semantics=("arbitrary",) * n_axes, vmem_limit_bytes=V7X_VMEM_LIMIT_BYTES
    )


def _pick(n, candidates):
    for c in candidates:
        if n % c == 0:
            return c
    return n


def _dot(a, b):
    return jnp.dot(a, b, preferred_element_type=F32)


def _dot_nt(a, b):
    return lax.dot_general(a, b, (((1,), (1,)), ((), ())), preferred_element_type=F32)


def _split(x):
    hi = x.astype(BF16)
    lo = (x - hi.astype(F32)).astype(BF16)
    return hi, lo


def _rms(x, eps):
    return x * lax.rsqrt(jnp.mean(x * x, axis=-1, keepdims=True) + eps)


def _silu(x):
    return x * jax.nn.sigmoid(x)


def _bf16_terms(x, n):
    terms = []
    for _ in range(n):
        bits = np.float32(x).view(np.uint32)
        bits = (bits + np.uint32(0x7FFF) + ((bits >> np.uint32(16)) & np.uint32(1))) & np.uint32(0xFFFF0000)
        t = float(bits.view(np.float32))
        terms.append(t)
        x -= t
    return tuple(terms)


LOG2E = math.log2(math.e)
LOG2E_BF16_TERMS = _bf16_terms(LOG2E, 3)
TILE_SUBLANES = 8


def _rows_from_tiles(ref, n_rows):
    return jnp.concatenate(
        [ref[pl.ds(sub, n_rows, stride=TILE_SUBLANES), :] for sub in range(TILE_SUBLANES)], axis=-1)


def _rows_to_tiles(ref, x):
    n_rows, width = x.shape
    lanes = width // TILE_SUBLANES
    for sub in range(TILE_SUBLANES):
        ref[pl.ds(sub, n_rows, stride=TILE_SUBLANES), :] = x[:, sub * lanes:(sub + 1) * lanes]


def _unpack_halves(xp):
    lo = pltpu.unpack_elementwise(xp, index=0, packed_dtype=BF16, unpacked_dtype=F32)
    hi = pltpu.unpack_elementwise(xp, index=1, packed_dtype=BF16, unpacked_dtype=F32)
    return lo.astype(BF16), hi.astype(BF16)


def _pack_halves(x):
    n = x.shape[-1] // 2
    return pltpu.pack_elementwise([x[:, :n], x[:, n:]], packed_dtype=BF16)


def _ada_kernel(c_ref, w_ref, b_ref, o_ref):
    a_hi, a_lo = _split(_silu(c_ref[...]))
    w_hi, w_lo = _split(w_ref[...])
    o_ref[...] = _dot(a_hi, w_hi) + (_dot(a_hi, w_lo) + _dot(a_lo, w_hi)) + b_ref[...]


def _ada(c, w, b):
    n, d = c.shape
    d_out = w.shape[1]
    tn = _pick(d_out, (512, 256, 128))
    return pl.pallas_call(
        _ada_kernel,
        grid=(d_out // tn,),
        in_specs=[
            pl.BlockSpec((n, d), lambda j: (0, 0)),
            pl.BlockSpec((d, tn), lambda j: (0, j)),
            pl.BlockSpec((1, tn), lambda j: (0, j)),
        ],
        out_specs=pl.BlockSpec((n, tn), lambda j: (0, j)),
        out_shape=jax.ShapeDtypeStruct((n, d_out), F32),
        compiler_params=_cparams(1),
        name="ada",
    )(c, w, b.reshape(1, d_out))


def _mod_spec(rows_per_batch, tm, d, n_tiles=None):
    clamp = (lambda i: i) if n_tiles is None else (lambda i: jnp.minimum(i, n_tiles - 1))
    if rows_per_batch % tm == 0:
        per = rows_per_batch // tm
        return pl.BlockSpec((None, 1, d), lambda i, *_: (clamp(i) // per, 0, 0))
    return pl.BlockSpec((None, tm, d), lambda i, *_: (clamp(i), 0, 0))


def _mod_array(m, rows_per_batch, tm):
    nb, d = m.shape
    if rows_per_batch % tm == 0:
        return m.reshape(nb, 1, d)
    assert tm % rows_per_batch == 0
    return jnp.repeat(m, rows_per_batch, axis=0).reshape(nb * rows_per_batch // tm, tm, d)


def _inproj_kernel(x_ref, sh_ref, sc_ref, g_ref, w_ref,
                   q_ref, k_ref, v_ref, u_ref, gt_ref, kb_ref, vb_ref, h_scr, *, qscale):
    j = pl.program_id(1)

    @pl.when(j == 0)
    def _():
        y = _rms(x_ref[...], NORM_EPS) * g_ref[...]
        h_scr[...] = (y * (1.0 + sc_ref[...]) + sh_ref[...]).astype(BF16)

    z = _dot(h_scr[...], w_ref[...])

    @pl.when(j == 0)
    def _():
        q_ref[...] = (z * qscale).astype(BF16)

    @pl.when(j == 1)
    def _():
        _rows_to_tiles(k_ref, z)
        kb_ref[...] = z.astype(BF16)

    @pl.when(j == 2)
    def _():
        _rows_to_tiles(v_ref, z)
        vb_ref[...] = z.astype(BF16)

    @pl.when(j == 3)
    def _():
        u_ref[...] = z

    @pl.when(j == 4)
    def _():
        gt_ref[...] = z


def _inproj(x2d, shift, scale, g, w_bf, rows_per_batch, qscale):
    r, d = x2d.shape
    wd = w_bf.shape[1] // 5
    tm = _pick(r, (512, 256, 128, 64, 32, 16, 8))
    row = lambda i, j: (i, 0)
    f32o = jax.ShapeDtypeStruct((r, wd), F32)
    kvo = jax.ShapeDtypeStruct((r * TILE_SUBLANES, wd // TILE_SUBLANES), F32)
    bfo = jax.ShapeDtypeStruct((r, wd), BF16)
    return pl.pallas_call(
        functools.partial(_inproj_kernel, qscale=qscale),
        grid=(r // tm, 5),
        in_specs=[
            pl.BlockSpec((tm, d), row),
            _mod_spec(rows_per_batch, tm, d),
            _mod_spec(rows_per_batch, tm, d),
            pl.BlockSpec((1, d), lambda i, j: (0, 0)),
            pl.BlockSpec((d, wd), lambda i, j: (0, j)),
        ],
        out_specs=[pl.BlockSpec((tm, wd), row), pl.BlockSpec((tm * TILE_SUBLANES, wd // TILE_SUBLANES), row),
                   pl.BlockSpec((tm * TILE_SUBLANES, wd // TILE_SUBLANES), row)] + [pl.BlockSpec((tm, wd), row)] * 4,
        out_shape=[bfo, kvo, kvo, f32o, f32o, bfo, bfo],
        scratch_shapes=[pltpu.VMEM((tm, d), BF16)],
        compiler_params=_cparams(2),
        name="inproj",
    )(x2d, _mod_array(shift, rows_per_batch, tm), _mod_array(scale, rows_per_batch, tm), g.reshape(1, d), w_bf)


def _lam(lamv_ref, lam_init):
    lv = lamv_ref[...]
    s1 = jnp.sum(lv[0:1] * lv[1:2], axis=-1, keepdims=True)
    s2 = jnp.sum(lv[2:3] * lv[3:4], axis=-1, keepdims=True)
    return jnp.exp(s1) - jnp.exp(s2) + lam_init


def _online_update(s, v, m_scr, l_scr, acc_scr, shift=None):
    m_old = m_scr[...]
    s_max = jnp.max(s, axis=-1, keepdims=True)
    m_new = jnp.maximum(m_old, s_max if shift is None else s_max + shift)
    alpha = jnp.exp2(m_old - m_new)
    p = jnp.exp2(s - (m_new if shift is None else m_new - shift))
    l_scr[...] = alpha * l_scr[...] + jnp.sum(p, axis=-1, keepdims=True)
    acc_scr[...] = alpha * acc_scr[...] + _dot(p.astype(BF16), v)
    m_scr[...] = m_new


def _pattn_kernel(slopes_ref, lamv_ref, gs_ref, q_ref, k_ref, v_ref, o_ref, q_scr, ka_scr, *state,
                  tq, rc, lam_init):
    n_chunks = 2 * tq // rc
    m_scrs, l_scrs, acc_scrs = state[:n_chunks], state[n_chunks:2 * n_chunks], state[2 * n_chunks:]
    h = pl.program_id(1)
    qi = pl.program_id(2)
    slope = slopes_ref[h]
    hd = q_ref.shape[1]
    half = hd // 2
    n_terms = len(LOG2E_BF16_TERMS)

    q = q_ref[...]
    lane = lax.broadcasted_iota(I32, q.shape, 1)
    zero = jnp.zeros_like(q)
    qa = jnp.zeros(q.shape, F32)
    for n, term in enumerate(LOG2E_BF16_TERMS):
        qa = jnp.where(jnp.logical_or(lane == n, lane == n + n_terms), term, qa)
    qa = qa.astype(BF16)
    q_scr[0:tq, 0:hd] = jnp.where(lane < half, q, zero)
    q_scr[tq:2 * tq, 0:hd] = jnp.where(lane >= half, q, zero)
    q_scr[0:tq, hd:2 * hd] = qa
    q_scr[tq:2 * tq, hd:2 * hd] = qa
    c = lax.broadcasted_iota(I32, (tq, hd), 0)
    c_lo = c % 256
    ka = jnp.where(lane < n_terms, c_lo.astype(F32) * slope,
                   jnp.where(lane < 2 * n_terms, (c - c_lo).astype(F32) * slope, 0.0))
    ka_scr[...] = ka.astype(BF16)

    for m_scr, l_scr, acc_scr in zip(m_scrs, l_scrs, acc_scrs):
        m_scr[...] = jnp.full(m_scr.shape, NEG, F32)
        l_scr[...] = jnp.zeros(l_scr.shape, F32)
        acc_scr[...] = jnp.zeros(acc_scr.shape, F32)

    def step(j, masked):
        start = pl.multiple_of(j * tq, tq)
        kaug = jnp.concatenate([k_ref[pl.ds(start, tq), :], ka_scr[...]], axis=-1)
        v = v_ref[pl.ds(start, tq), :]
        for ci in range(n_chunks):
            r0 = ci * rc
            q0 = r0 % tq
            ncol = q0 + rc if masked else tq
            rowpos = q0 + lax.broadcasted_iota(I32, (rc, 1), 0)
            s = _dot_nt(q_scr[r0:r0 + rc, :], kaug[:ncol])
            if masked:
                local = lax.broadcasted_iota(I32, (rc, rc), 1) <= lax.broadcasted_iota(I32, (rc, rc), 0)
                diag = jnp.where(local, s[:, q0:], NEG)
                s = diag if q0 == 0 else jnp.concatenate([s[:, :q0], diag], axis=1)
            shift = ((j - qi) * tq - rowpos).astype(F32) * (slope * LOG2E)
            _online_update(s, v[:ncol], m_scrs[ci], l_scrs[ci], acc_scrs[ci], shift=shift)

    def body(j, carry):
        step(j, False)
        return carry

    lax.fori_loop(0, qi, body, 0)
    step(qi, True)

    lam = _lam(lamv_ref, lam_init)
    o = jnp.concatenate([acc[...] / l[...] for acc, l in zip(acc_scrs, l_scrs)], axis=0)
    att = o[:tq] - lam * o[tq:]
    att = _rms(att, SUBLN_EPS) * gs_ref[...] * (1.0 - lam_init)
    o_ref[...] = att.astype(BF16)


def _pattn(qb, kb, vb, slopes, lamv, g_subln, n_batch, seq, n_heads, lam_init):
    r, aw = qb.shape
    hd = aw // n_heads
    tq = _pick(seq, (2048, 1024, 512, 256, 128))
    rc = min(tq, 256)
    n_chunks = 2 * tq // rc
    nq = seq // tq
    return pl.pallas_call(
        functools.partial(_pattn_kernel, tq=tq, rc=rc, lam_init=lam_init),
        grid_spec=pltpu.PrefetchScalarGridSpec(
            num_scalar_prefetch=1,
            grid=(n_batch, n_heads, nq),
            in_specs=[
                pl.BlockSpec(lamv.shape, lambda b, h, i, *_: (0, 0)),
                pl.BlockSpec((1, hd), lambda b, h, i, *_: (0, 0)),
                pl.BlockSpec((tq, hd), lambda b, h, i, *_: (b * nq + i, h)),
                pl.BlockSpec((seq, hd), lambda b, h, i, *_: (b, h)),
                pl.BlockSpec((seq, hd), lambda b, h, i, *_: (b, h)),
            ],
            out_specs=pl.BlockSpec((tq, hd), lambda b, h, i, *_: (b * nq + i, h)),
            scratch_shapes=[pltpu.VMEM((2 * tq, 2 * hd), BF16), pltpu.VMEM((tq, hd), BF16)]
            + [pltpu.VMEM((rc, 1), F32)] * (2 * n_chunks) + [pltpu.VMEM((rc, hd), F32)] * n_chunks,
        ),
        out_shape=jax.ShapeDtypeStruct((r, aw), BF16),
        compiler_params=_cparams(3),
        name="pattn",
    )(slopes, lamv, g_subln.reshape(1, hd), qb, kb, vb)


def _sattn_kernel(pt_ref, lamv_ref, gs_ref, wq_ref, kn_ref, vn_ref, *rest,
                  n_pages_step, page, past, n_new, n_heads, lam_init):
    k_refs = rest[:n_pages_step]
    v_refs = rest[n_pages_step:2 * n_pages_step]
    o_ref = rest[2 * n_pages_step]
    m_scr, l_scr, acc_scr = rest[2 * n_pages_step + 1:]
    j = pl.program_id(1)
    n_rows = wq_ref.shape[0]
    hd = wq_ref.shape[1] // n_heads
    tk = n_pages_step * page

    r = lax.broadcasted_iota(I32, (n_rows, 1), 0)
    head = r // (2 * n_new)
    qi = r % n_new
    slope = jnp.exp2(-(head + 1).astype(F32)) * LOG2E

    @pl.when(j == 0)
    def _():
        m_scr[...] = jnp.full(m_scr.shape, NEG, F32)
        l_scr[...] = jnp.zeros(l_scr.shape, F32)
        acc_scr[...] = jnp.zeros(acc_scr.shape, F32)

    wq = wq_ref[...]
    kc = jnp.concatenate([_rows_from_tiles(kr, page).astype(BF16) for kr in k_refs], axis=0)
    vc = jnp.concatenate([_rows_from_tiles(vr, page).astype(BF16) for vr in v_refs], axis=0)
    t = j * tk + lax.broadcasted_iota(I32, (1, tk), 1)
    s = _dot_nt(wq, kc) - slope * (past + qi - t).astype(F32)
    _online_update(s, vc, m_scr, l_scr, acc_scr)

    @pl.when(j == pl.num_programs(1) - 1)
    def _():
        pad = jnp.zeros((page - n_new, kn_ref.shape[1]), BF16)
        kn = jnp.concatenate([kn_ref[...].astype(BF16), pad], axis=0)
        vn = jnp.concatenate([vn_ref[...].astype(BF16), pad], axis=0)
        tj = lax.broadcasted_iota(I32, (1, page), 1)
        sn = _dot_nt(wq, kn) - slope * (qi - tj).astype(F32)
        sn = jnp.where(tj <= qi, sn, NEG)
        _online_update(sn, vn, m_scr, l_scr, acc_scr)

        lam = _lam(lamv_ref, lam_init)
        o = acc_scr[...] / l_scr[...]
        outs = []
        for h in range(n_heads):
            blk = o[h * 2 * n_new:(h + 1) * 2 * n_new, h * hd:(h + 1) * hd]
            att = blk[:n_new] - lam * blk[n_new:]
            outs.append(_rms(att, SUBLN_EPS) * gs_ref[...] * (1.0 - lam_init))
        o_ref[...] = jnp.concatenate(outs, axis=-1).astype(BF16)


def _sattn(wq, k_new, v_new, cache_k2, cache_v2, page_table, lamv, g_subln, n_heads, lam_init):
    n_seq, n_rows, aw = wq.shape
    n_new = k_new.shape[1]
    hd = aw // n_heads
    page = cache_k2.shape[1] // n_heads
    n_pages = page_table.shape[1]
    pstep = _pick(n_pages, (8, 4, 2, 1))
    past = n_pages * page
    assert n_heads == TILE_SUBLANES and hd == 128

    def page_spec(p):
        return pl.BlockSpec((None, page * n_heads, hd), lambda b, j, pt: (pt[b, j * pstep + p], 0, 0))

    seq_spec = lambda rows: pl.BlockSpec((None, rows, aw), lambda b, j, pt: (b, 0, 0))
    return pl.pallas_call(
        functools.partial(_sattn_kernel, n_pages_step=pstep, page=page, past=past, n_new=n_new,
                          n_heads=n_heads, lam_init=lam_init),
        grid_spec=pltpu.PrefetchScalarGridSpec(
            num_scalar_prefetch=1,
            grid=(n_seq, n_pages // pstep),
            in_specs=[
                pl.BlockSpec(lamv.shape, lambda b, j, pt: (0, 0)),
                pl.BlockSpec((1, hd), lambda b, j, pt: (0, 0)),
                seq_spec(n_rows), seq_spec(n_new), seq_spec(n_new),
            ] + [page_spec(p) for p in range(pstep)] * 2,
            out_specs=seq_spec(n_new),
            scratch_shapes=[
                pltpu.VMEM((n_rows, 1), F32),
                pltpu.VMEM((n_rows, 1), F32),
                pltpu.VMEM((n_rows, aw), F32),
            ],
        ),
        out_shape=jax.ShapeDtypeStruct((n_seq, n_new, aw), BF16),
        compiler_params=_cparams(2),
        name="sattn",
    )(page_table, lamv, g_subln.reshape(1, hd), wq, k_new, v_new,
      *([cache_k2] * pstep), *([cache_v2] * pstep))


def _gelu_tanh(x):
    return x * (0.5 * (1.0 + jnp.tanh(math.sqrt(2.0 / math.pi) * (x + 0.044715 * (x * x * x)))))


def _softplus(x):
    return jnp.maximum(x, 0.0) + jnp.log1p(jnp.exp(-jnp.abs(x)))


def _rglru_kernel(u_ref, gt_ref, c0_ref, h0_ref, cw_ref, cb_ref, wai_ref, ba_ref, bi_ref, lam_ref, gn_ref,
                  rec_ref, ht_ref, cout_ref, ubuf, hcar, *, tl, n_blocks):
    t = pl.program_id(1)
    halo = CONV_WIDTH - 1
    base = 8

    @pl.when(t == 0)
    def _():
        ubuf[base - halo:base, :] = c0_ref[...]
        hcar[...] = h0_ref[...]

    ubuf[base:base + tl, :] = u_ref[...]
    cw = cw_ref[...]
    xc = cb_ref[...] + cw[0:1] * ubuf[base - halo:base - halo + tl, :]
    for jj in range(1, CONV_WIDTH):
        xc = xc + cw[jj:jj + 1] * ubuf[base - halo + jj:base - halo + jj + tl, :]
    tail = ubuf[base + tl - halo:base + tl, :]
    ubuf[base - halo:base, :] = tail
    cout_ref[...] = tail

    bw = xc.shape[1] // n_blocks
    za, zi = [], []
    for n in range(n_blocks):
        z = _dot(xc[:, n * bw:(n + 1) * bw].astype(BF16), wai_ref[n])
        za.append(z[:, :bw])
        zi.append(z[:, bw:])
    r = jax.nn.sigmoid(jnp.concatenate(za, axis=-1) + ba_ref[...])
    i = jax.nn.sigmoid(jnp.concatenate(zi, axis=-1) + bi_ref[...])
    log_a = -RG_C * r * _softplus(-lam_ref[...])
    a = jnp.exp(log_a)
    th = jnp.tanh(log_a)
    b = xc * i * jnp.sqrt(-2.0 * th / (1.0 - th))

    rowi = lax.broadcasted_iota(I32, a.shape, 0)
    sft = 1
    while sft < tl:
        keep = rowi >= sft
        a_prev = jnp.where(keep, pltpu.roll(a, sft, 0), 1.0)
        b_prev = jnp.where(keep, pltpu.roll(b, sft, 0), 0.0)
        b = a * b_prev + b
        a = a * a_prev
        sft *= 2
    hs = a * hcar[...] + b
    h_last = hs[tl - 1:tl, :]
    hcar[...] = h_last
    ht_ref[...] = h_last

    rec = hs * _gelu_tanh(gt_ref[...])
    rec_ref[...] = (_rms(rec, NORM_EPS) * gn_ref[...]).astype(BF16)


def _rglru(u2d, gate2d, conv0, h0, conv_w, conv_b, wai_bf, b_a, b_i, rg_lambda, g_rgnorm, n_batch, seq):
    r, w = u2d.shape
    tl = _pick(seq, (256, 128, 64, 32, 16, 8))
    nt = seq // tl
    n_blocks = wai_bf.shape[0]
    halo = CONV_WIDTH - 1
    row = lambda b, t: (b * nt + t, 0)
    vec = pl.BlockSpec((1, w), lambda b, t: (0, 0))
    rec, ht, cout = pl.pallas_call(
        functools.partial(_rglru_kernel, tl=tl, n_blocks=n_blocks),
        grid=(n_batch, nt),
        in_specs=[
            pl.BlockSpec((tl, w), row),
            pl.BlockSpec((tl, w), row),
            pl.BlockSpec((None, halo, w), lambda b, t: (b, 0, 0)),
            pl.BlockSpec((None, 1, w), lambda b, t: (b, 0, 0)),
            pl.BlockSpec((CONV_WIDTH, w), lambda b, t: (0, 0)),
            vec,
            pl.BlockSpec(wai_bf.shape, lambda b, t: (0, 0, 0)),
            vec, vec, vec, vec,
        ],
        out_specs=[
            pl.BlockSpec((tl, w), row),
            pl.BlockSpec((None, 1, w), lambda b, t: (b, 0, 0)),
            pl.BlockSpec((None, halo, w), lambda b, t: (b, 0, 0)),
        ],
        out_shape=[
            jax.ShapeDtypeStruct((r, w), BF16),
            jax.ShapeDtypeStruct((n_batch, 1, w), F32),
            jax.ShapeDtypeStruct((n_batch, halo, w), F32),
        ],
        scratch_shapes=[pltpu.VMEM((tl + 8, w), F32), pltpu.VMEM((1, w), F32)],
        compiler_params=_cparams(2),
        name="rglru",
    )(u2d, gate2d, conv0, h0.reshape(n_batch, 1, w), conv_w, conv_b.reshape(1, w), wai_bf,
      b_a.reshape(1, w), b_i.reshape(1, w), rg_lambda.reshape(1, w), g_rgnorm.reshape(1, w))
    return rec, ht.reshape(n_batch, w), cout


def _oproj_kernel(att_ref, rec_ref, x_ref, g1_ref, sh_ref, sc_ref, gn_ref, wo_ref, wrt_ref, *rest,
                  aliased, n_tiles):
    x1_ref, h2p_ref, st_ref = rest[2:] if aliased else rest
    aw = att_ref.shape[1]
    i = pl.program_id(0)

    @pl.when(i < n_tiles)
    def _():
        mix = _dot(att_ref[...], wo_ref[:aw, :]) + _dot(rec_ref[...], wo_ref[aw:, :])
        x1 = x_ref[...] + g1_ref[...] * mix
        x1_ref[...] = x1
        h2 = (_rms(x1, NORM_EPS) * gn_ref[...]) * (1.0 + sc_ref[...]) + sh_ref[...]
        _rows_to_tiles(h2p_ref, _pack_halves(h2))
        h_hi, h_lo = _split(h2)
        w_hi, w_lo = _split(wrt_ref[...])
        logits_t = _dot_nt(w_hi, h_hi) + (_dot_nt(w_hi, h_lo) + _dot_nt(w_lo, h_hi))
        st_ref[...] = jax.nn.sigmoid(logits_t)

    @pl.when(i >= n_tiles)
    def _():
        h2p_ref[...] = jnp.zeros(h2p_ref.shape, U32)
        st_ref[...] = jnp.zeros(st_ref.shape, F32)


def _oproj(att, rec, x2d, g1, shift, scale, g_norm2, wo_bf, wr_t, rows_per_batch, row_offset, total_rows,
           h2p_all=None, st_all=None):
    r, d = x2d.shape
    aw = att.shape[1]
    n_exp = wr_t.shape[0]
    tm = _pick(r, (256, 128))
    assert row_offset % tm == 0
    off = row_offset // tm
    lanes = d // 2 // TILE_SUBLANES
    assert lanes == 128
    aliased = h2p_all is not None
    n_tiles = r // tm
    n_fill = 0 if aliased else (total_rows - r) // tm
    assert aliased or (row_offset == 0 and (total_rows - r) % tm == 0)
    row = lambda i: (jnp.minimum(i, n_tiles - 1), 0)
    mspec = _mod_spec(rows_per_batch, tm, d, n_tiles)
    in_specs = [
        pl.BlockSpec((tm, aw), row),
        pl.BlockSpec((tm, d - aw), row),
        pl.BlockSpec((tm, d), row),
        mspec, mspec, mspec,
        pl.BlockSpec((1, d), lambda i: (0, 0)),
        pl.BlockSpec((d, d), lambda i: (0, 0)),
        pl.BlockSpec((n_exp, d), lambda i: (0, 0)),
    ]
    args = [att, rec, x2d, _mod_array(g1, rows_per_batch, tm), _mod_array(shift, rows_per_batch, tm),
            _mod_array(scale, rows_per_batch, tm), g_norm2.reshape(1, d), wo_bf, wr_t]
    io_alias = {}
    if aliased:
        in_specs += [pl.BlockSpec(memory_space=pl.ANY), pl.BlockSpec(memory_space=pl.ANY)]
        io_alias = {len(args): 1, len(args) + 1: 2}
        args += [h2p_all, st_all]
    return pl.pallas_call(
        functools.partial(_oproj_kernel, aliased=aliased, n_tiles=n_tiles),
        grid=(n_tiles + n_fill,),
        in_specs=in_specs,
        out_specs=[
            pl.BlockSpec((tm, d), row),
            pl.BlockSpec((tm * TILE_SUBLANES, lanes), lambda i: (i + off, 0)),
            pl.BlockSpec((n_exp, tm), lambda i: (0, i + off)),
        ],
        out_shape=[
            jax.ShapeDtypeStruct((r, d), F32),
            jax.ShapeDtypeStruct((total_rows * TILE_SUBLANES, lanes), U32),
            jax.ShapeDtypeStruct((n_exp, total_rows), F32),
        ],
        input_output_aliases=io_alias,
        compiler_params=_cparams(1),
        name="oproj",
    )(*args)


def _route_kernel(st_ref, rb_ref, idx_ref, wts_ref, rank_ref, cnt_ref, carry):
    i = pl.program_id(0)
    n_exp, tr = st_ref.shape
    gsz = n_exp // N_GROUPS

    @pl.when(i == 0)
    def _():
        carry[...] = jnp.zeros(carry.shape, F32)

    s = st_ref[...]
    biased = s + rb_ref[...]
    g = biased.reshape(N_GROUPS, gsz, tr)
    within = lax.broadcasted_iota(I32, g.shape, 1)
    m1 = jnp.max(g, axis=1, keepdims=True)
    first = jnp.min(jnp.where(g == m1, within, gsz), axis=1, keepdims=True)
    m2 = jnp.max(jnp.where(within == first, -jnp.inf, g), axis=1, keepdims=True)
    gscore = (m1 + m2).reshape(N_GROUPS, tr)

    gidx = lax.broadcasted_iota(I32, gscore.shape, 0)
    gsel = jnp.zeros(gscore.shape, F32)
    for _ in range(TOPK_GROUPS):
        mg = jnp.max(gscore, axis=0, keepdims=True)
        fg = jnp.min(jnp.where(gscore == mg, gidx, N_GROUPS), axis=0, keepdims=True)
        hit = gidx == fg
        gsel = jnp.where(hit, 1.0, gsel)
        gscore = jnp.where(hit, -jnp.inf, gscore)
    masked = jnp.where(gsel.reshape(N_GROUPS, 1, tr) > 0.5, g, -jnp.inf).reshape(n_exp, tr)

    eidx = lax.broadcasted_iota(I32, (n_exp, tr), 0)
    idxs, ws = [], []
    chosen = jnp.zeros((n_exp, tr), jnp.bool_)
    for _ in range(TOP_K):
        mv = jnp.max(masked, axis=0, keepdims=True)
        fe = jnp.min(jnp.where(masked == mv, eidx, n_exp), axis=0, keepdims=True)
        hit = eidx == fe
        idxs.append(fe)
        ws.append(jnp.sum(jnp.where(hit, s, 0.0), axis=0, keepdims=True))
        chosen = chosen | hit
        masked = jnp.where(hit, -jnp.inf, masked)
    idx = jnp.concatenate(idxs, axis=0)
    w = jnp.concatenate(ws, axis=0)
    idx_ref[...] = idx
    wts_ref[...] = w / jnp.sum(w, axis=0, keepdims=True) * ROUTED_SCALE

    cmat = jnp.where(chosen, 1.0, 0.0)
    before = lax.broadcasted_iota(I32, (tr, tr), 0) < lax.broadcasted_iota(I32, (tr, tr), 1)
    prior = _dot(cmat.astype(BF16), jnp.where(before, 1.0, 0.0).astype(BF16)) + carry[...]
    ranks = [jnp.sum(jnp.where(eidx == idxs[k], prior, 0.0), axis=0, keepdims=True) for k in range(TOP_K)]
    rank_ref[...] = jnp.concatenate(ranks, axis=0).astype(I32)
    carry[...] = carry[...] + jnp.sum(cmat, axis=1, keepdims=True)
    cnt_ref[...] = carry[...].astype(I32)


def _route(st_all, router_bias):
    n_exp, t_all = st_all.shape
    tr = _pick(t_all, (640, 512, 256, 128))
    col = lambda i: (0, i)
    o8 = lambda dt: jax.ShapeDtypeStruct((TOP_K, t_all), dt)
    return pl.pallas_call(
        _route_kernel,
        grid=(t_all // tr,),
        in_specs=[pl.BlockSpec((n_exp, tr), col), pl.BlockSpec((n_exp, 1), lambda i: (0, 0))],
        out_specs=[pl.BlockSpec((TOP_K, tr), col)] * 3 + [pl.BlockSpec((n_exp, 1), lambda i: (0, 0))],
        out_shape=[o8(I32), o8(F32), o8(I32), jax.ShapeDtypeStruct((n_exp, 1), I32)],
        scratch_shapes=[pltpu.VMEM((n_exp, 1), F32)],
        compiler_params=_cparams(1),
        name="route",
    )(st_all, router_bias.reshape(n_exp, 1))


def _pos_kernel(idx_ref, rank_ref, start_ref, pos_ref):
    n_exp = start_ref.shape[0]
    tr = idx_ref.shape[1]
    eidx = lax.broadcasted_iota(I32, (n_exp, tr), 0)
    start = start_ref[...]
    rows = [jnp.sum(jnp.where(eidx == idx_ref[k:k + 1, :], start, 0.0), axis=0, keepdims=True)
            for k in range(TOP_K)]
    pos_ref[...] = jnp.concatenate(rows, axis=0).astype(I32) + rank_ref[...]


def _pos(idx, rank, start_rows):
    n_exp = start_rows.shape[0]
    t_all = idx.shape[1]
    tr = _pick(t_all, (640, 512, 256, 128))
    col = pl.BlockSpec((TOP_K, tr), lambda i: (0, i))
    return pl.pallas_call(
        _pos_kernel,
        grid=(t_all // tr,),
        in_specs=[col, col, pl.BlockSpec((n_exp, 1), lambda i: (0, 0))],
        out_specs=col,
        out_shape=jax.ShapeDtypeStruct((TOP_K, t_all), I32),
        compiler_params=_cparams(1),
        name="pos",
    )(idx, rank, start_rows.astype(F32).reshape(n_exp, 1))


def _tile_pos(pos, tile):
    k, t = pos.shape
    return pos.reshape(k, t // tile, tile).transpose(1, 0, 2)


def _sc_worker_chunks(n_chunks, fn):
    n_workers = V7X_SC_CORES * V7X_SC_SUBCORES
    worker = lax.axis_index("core") * V7X_SC_SUBCORES + lax.axis_index("subcore")

    @pl.loop(0, -(-n_chunks // n_workers))
    def _(it):
        chunk = it * n_workers + worker

        @pl.when(chunk < n_chunks)
        def _():
            fn(chunk)


def _sc_mesh():
    return plsc.VectorSubcoreMesh(core_axis_name="core", subcore_axis_name="subcore",
                                  num_cores=V7X_SC_CORES, num_subcores=V7X_SC_SUBCORES)


def _sc_scatter_rows(x3, idx3, n_rows):
    n_chunks, n_k, width = idx3.shape
    row = x3.shape[1:]

    @pl.kernel(out_type=jax.ShapeDtypeStruct((n_rows,) + row, x3.dtype), mesh=_sc_mesh(),
               scratch_types=[pltpu.VMEM((width,) + row, x3.dtype), pltpu.VMEM((n_k, width), I32),
                              pltpu.SemaphoreType.DMA],
               name="dispatch_sc")
    def scatter(x_hbm, i_hbm, o_hbm, xbuf, ibuf, sem):
        def one(chunk):
            pltpu.sync_copy(x_hbm.at[pl.ds(chunk * width, width)], xbuf)
            pltpu.sync_copy(i_hbm.at[chunk], ibuf)
            copies = [pltpu.make_async_copy(xbuf, o_hbm.at[ibuf.at[k]], sem) for k in range(n_k)]
            for cp in copies:
                cp.start()
            for cp in copies:
                cp.wait()

        _sc_worker_chunks(n_chunks, one)

    return scatter(x3, idx3)


def _sc_gather_rows(src3, idx2):
    n_chunks, width = idx2.shape
    row = src3.shape[1:]

    @pl.kernel(out_type=jax.ShapeDtypeStruct((n_chunks * width,) + row, src3.dtype), mesh=_sc_mesh(),
               scratch_types=[pltpu.VMEM((width,) + row, src3.dtype), pltpu.VMEM((width,), I32),
                              pltpu.SemaphoreType.DMA],
               name="combine_sc")
    def gather(s_hbm, i_hbm, o_hbm, buf, ibuf, sem):
        def one(chunk):
            pltpu.sync_copy(i_hbm.at[chunk], ibuf)
            pltpu.async_copy(s_hbm.at[ibuf], buf, sem).wait()
            pltpu.sync_copy(buf, o_hbm.at[pl.ds(chunk * width, width)])

        _sc_worker_chunks(n_chunks, one)

    return gather(src3, idx2)


def _swiglu_packed(xp, wg, wu, wd):
    xa, xb = _unpack_halves(xp)
    half = xp.shape[1]
    g = _dot(xa, wg[:half, :]) + _dot(xb, wg[half:, :])
    u = _dot(xa, wu[:half, :]) + _dot(xb, wu[half:, :])
    return _dot((_silu(g) * u).astype(BF16), wd[...])


def _experts_kernel(te_ref, tnv_ref, tord_ref, tnext_ref, tlo_ref, thi_ref, xs_ref, wg_hbm, wu_hbm, wd_hbm, ys_ref,
                    wg_f, wu_f, wd_f, wg_b, wu_b, wd_b, sems):
    i = pl.program_id(0)
    nv = tnv_ref[i]
    expert = te_ref[i]
    new_expert = jnp.logical_or(i == 0, expert != te_ref[jnp.maximum(i - 1, 0)])
    slot = tord_ref[i] % 2
    tm = ys_ref.shape[0] // TILE_SUBLANES

    def weight_copies(e, sl):
        copies = []
        for n, (hbm, buf) in enumerate(((wg_hbm, wg_f), (wu_hbm, wu_f), (wd_hbm, wd_f))):
            rows = hbm.shape[1] // EXPERT_WEIGHT_DMA_CHUNKS
            for c in range(EXPERT_WEIGHT_DMA_CHUNKS):
                span = pl.ds(c * rows, rows)
                copies.append(pltpu.make_async_copy(hbm.at[e, span], buf.at[sl, span], sems.at[n, sl]))
        return copies

    @pl.when(jnp.logical_and(nv > 0, new_expert))
    def _():
        @pl.when(i == 0)
        def _():
            for cp in weight_copies(expert, slot):
                cp.start()

        for cp in weight_copies(expert, slot):
            cp.wait()

    nxt = tnext_ref[i]

    @pl.when(jnp.logical_and(nv > 0, nxt >= 0))
    def _():
        lo, hi = tlo_ref[i], thi_ref[i]
        for c, cp in enumerate(weight_copies(nxt, 1 - slot)):
            @pl.when(jnp.logical_and(lo <= c, c < hi))
            def _():
                cp.start(priority=WEIGHT_PREFETCH_DMA_PRIORITY)

    @pl.when(jnp.logical_and(nv > 0, new_expert))
    def _():
        wg_b[...] = wg_f[slot].astype(BF16)
        wu_b[...] = wu_f[slot].astype(BF16)
        wd_b[...] = wd_f[slot].astype(BF16)

    @pl.when(nv > 0)
    def _():
        xp = _rows_from_tiles(xs_ref, tm)
        rowi = lax.broadcasted_iota(I32, xp.shape, 0)
        xp = jnp.where(rowi < nv, xp, jnp.uint32(0))
        _rows_to_tiles(ys_ref, _pack_halves(_swiglu_packed(xp, wg_b, wu_b, wd_b)))

    @pl.when(nv == 0)
    def _():
        ys_ref[...] = jnp.zeros(ys_ref.shape, U32)


def _experts(xs, tile_tables, w_gate, w_up, w_down):
    rows, lanes = xs.shape
    tm = EXPERT_TILE_ROWS
    n_exp, d, ff = w_gate.shape
    blk = pl.BlockSpec((tm * TILE_SUBLANES, lanes), lambda i, *_: (i, 0))
    hbm = pl.BlockSpec(memory_space=pl.ANY)
    return pl.pallas_call(
        _experts_kernel,
        grid_spec=pltpu.PrefetchScalarGridSpec(
            num_scalar_prefetch=len(tile_tables),
            grid=(rows // (tm * TILE_SUBLANES),),
            in_specs=[blk, hbm, hbm, hbm],
            out_specs=blk,
            scratch_shapes=[
                pltpu.VMEM((2, d, ff), F32), pltpu.VMEM((2, d, ff), F32), pltpu.VMEM((2, ff, d), F32),
                pltpu.VMEM((d, ff), BF16), pltpu.VMEM((d, ff), BF16), pltpu.VMEM((ff, d), BF16),
                pltpu.SemaphoreType.DMA((3, 2)),
            ],
        ),
        out_shape=jax.ShapeDtypeStruct((rows, lanes), U32),
        compiler_params=_cparams(1),
        name="experts",
    )(*tile_tables, xs, w_gate, w_up, w_down)


def _combine_kernel(w_ref, yg_ref, h2p_ref, x1_ref, g2_ref, gf_ref, wsg_ref, wsu_ref, wsd_ref, o_ref, *, tc):
    shared = _swiglu_packed(_rows_from_tiles(h2p_ref, tc), wsg_ref, wsu_ref, wsd_ref)
    w = w_ref[...]
    lo, hi = None, None
    for k in range(TOP_K):
        rows = tc * TILE_SUBLANES
        yk = _rows_from_tiles(yg_ref.at[pl.ds(k * rows, rows)], tc)
        wk = w[:, k:k + 1]
        yl = pltpu.unpack_elementwise(yk, index=0, packed_dtype=BF16, unpacked_dtype=F32) * wk
        yh = pltpu.unpack_elementwise(yk, index=1, packed_dtype=BF16, unpacked_dtype=F32) * wk
        lo = yl if lo is None else lo + yl
        hi = yh if hi is None else hi + yh
    routed = jnp.concatenate([lo, hi], axis=-1)
    x2 = x1_ref[...] + g2_ref[...] * (routed + shared)
    o_ref[...] = _rms(x2, NORM_EPS) * gf_ref[...]


def _combine(yg, wts_t, h2p_all, x1, g2, g_final, wsg_bf, wsu_bf, wsd_bf, rows_per_batch, row_offset):
    r, d = x1.shape
    lanes = h2p_all.shape[1]
    tc = COMBINE_TILE
    assert r % tc == 0 and row_offset % tc == 0
    off = row_offset // tc
    ff = wsg_bf.shape[1]
    const = lambda shp: pl.BlockSpec(shp, lambda i: (0,) * len(shp))
    return pl.pallas_call(
        functools.partial(_combine_kernel, tc=tc),
        grid=(r // tc,),
        in_specs=[
            pl.BlockSpec((tc, TOP_K), lambda i: (i, 0)),
            pl.BlockSpec((TOP_K * tc * TILE_SUBLANES, lanes), lambda i: (i + off, 0)),
            pl.BlockSpec((tc * TILE_SUBLANES, lanes), lambda i: (i + off, 0)),
            pl.BlockSpec((tc, d), lambda i: (i, 0)),
            _mod_spec(rows_per_batch, tc, d),
            const((1, d)), const((d, ff)), const((d, ff)), const((ff, d)),
        ],
        out_specs=pl.BlockSpec((tc, d), lambda i: (i, 0)),
        out_shape=jax.ShapeDtypeStruct((r, d), F32),
        compiler_params=_cparams(1),
        name="combine",
    )(wts_t, yg, h2p_all, x1, _mod_array(g2, rows_per_batch, tc), g_final.reshape(1, d),
      wsg_bf, wsu_bf, wsd_bf)


def kernel(x_prompt, x_sample, cache_k, cache_v, state_h, state_conv, page_table, c_prompt, c_sample,
           w_ada, b_ada, g_norm1, w_in, lambda_q1, lambda_k1, lambda_q2, lambda_k2, g_subln,
           conv_w, conv_b, w_rg_a, b_rg_a, w_rg_i, b_rg_i, rg_lambda, g_rgnorm, w_o, g_norm2,
           w_router, router_bias, w_e_gate, w_e_up, w_e_down, w_s_gate, w_s_up, w_s_down, g_final):
    depth = w_ada.shape[0]
    assert depth == 1, "single-layer step"
    bp, seq, d = x_prompt.shape
    bs, n_new, _ = x_sample.shape
    n_heads = cache_k.shape[3]
    k_row = cache_k.shape[4]
    v_head = cache_v.shape[4]
    aw = n_heads * v_head
    rw = d - aw
    assert k_row == v_head and w_in.shape[2] == 3 * aw + 2 * rw and aw == rw
    qk_half = k_row // 2
    n_exp = w_router.shape[2]
    lam_init = 0.8 - 0.6 * math.exp(-0.3 * 0)
    tp, ts = bp * seq, bs * n_new
    t_all = tp + ts

    w_in_bf = w_in[0].astype(BF16)
    wo_bf = w_o[0].astype(BF16)
    wr_t = w_router[0].T
    wai_bf = jnp.concatenate([w_rg_a[0], w_rg_i[0]], axis=-1).astype(BF16)
    wsg_bf, wsu_bf, wsd_bf = w_s_gate[0].astype(BF16), w_s_up[0].astype(BF16), w_s_down[0].astype(BF16)
    lamv = jnp.stack([lambda_q1[0], lambda_k1[0], lambda_q2[0], lambda_k2[0]])
    slopes = jnp.exp2(-8.0 * jnp.arange(1, n_heads + 1, dtype=F32) / n_heads)

    mod = _ada(jnp.concatenate([c_prompt, c_sample], axis=0), w_ada[0], b_ada[0])
    mod_p = [mod[:bp, i * d:(i + 1) * d] for i in range(6)]
    mod_s = [mod[bp:, i * d:(i + 1) * d] for i in range(6)]

    xp2, xs2 = x_prompt.reshape(tp, d), x_sample.reshape(ts, d)
    qscale = qk_half ** -0.5 * LOG2E
    qp, kp, vp, up, gp, kpb, vpb = _inproj(xp2, mod_p[0], mod_p[1], g_norm1[0], w_in_bf, seq, qscale)
    qs, ks, vs, us, gs, ksb, vsb = _inproj(xs2, mod_s[0], mod_s[1], g_norm1[0], w_in_bf, n_new, qscale)

    att_p = _pattn(qp, kpb, vpb, slopes, lamv, g_subln[0], bp, seq, n_heads, lam_init)

    q5 = qs.reshape(bs, n_new, n_heads, 2, qk_half).transpose(0, 2, 3, 1, 4)
    eye_h = jnp.eye(n_heads, dtype=BF16)
    eye_c = jnp.eye(2, dtype=BF16)
    wq = (q5[:, :, :, :, None, None, :] * eye_h[None, :, None, None, :, None, None]
          * eye_c[None, None, :, None, None, :, None]).reshape(bs, n_heads * 2 * n_new, aw)
    n_pool, page = cache_k.shape[1], cache_k.shape[2]
    att_s = _sattn(wq, ksb.reshape(bs, n_new, aw), vsb.reshape(bs, n_new, aw),
                   cache_k.reshape(n_pool, page * n_heads, k_row), cache_v.reshape(n_pool, page * n_heads, v_head),
                   page_table, lamv, g_subln[0], n_heads, lam_init).reshape(ts, aw)

    rg_args = (conv_w[0], conv_b[0], wai_bf, b_rg_a[0], b_rg_i[0], rg_lambda[0], g_rgnorm[0])
    rec_p, h_p, conv_p = _rglru(up, gp, jnp.zeros((bp, CONV_WIDTH - 1, rw), F32), jnp.zeros((bp, rw), F32),
                                *rg_args, bp, seq)
    rec_s, h_s, conv_s = _rglru(us, gs, state_conv[0], state_h[0], *rg_args, bs, n_new)

    x1p, h2p_all, st_all = _oproj(att_p, rec_p, xp2, mod_p[2], mod_p[3], mod_p[4], g_norm2[0], wo_bf, wr_t,
                                  seq, 0, t_all)
    x1s, h2p_all, st_all = _oproj(att_s, rec_s, xs2, mod_s[2], mod_s[3], mod_s[4], g_norm2[0], wo_bf, wr_t,
                                  n_new, tp, t_all, h2p_all, st_all)

    idx, wts, rank, counts = _route(st_all, router_bias[0])

    tm = EXPERT_TILE_ROWS
    counts = counts.reshape(n_exp)
    ptiles = (counts + tm - 1) // tm
    pend = jnp.cumsum(ptiles)
    pstart = pend - ptiles
    n_tiles = (t_all * TOP_K) // tm + n_exp
    tile_ids = jnp.arange(n_tiles, dtype=I32)
    tile_e = jnp.minimum(jnp.sum(pend[None, :] <= tile_ids[:, None], axis=1), n_exp - 1).astype(I32)
    tile_nv = jnp.where(tile_ids < pend[-1],
                        jnp.clip(counts[tile_e] - (tile_ids - pstart[tile_e]) * tm, 0, tm), 0).astype(I32)
    tile_ord = (jnp.cumsum(ptiles > 0) - 1)[tile_e].astype(I32)
    next_tile = pend[tile_e]
    tile_next = jnp.where(next_tile < pend[-1], tile_e[jnp.minimum(next_tile, n_tiles - 1)], -1).astype(I32)
    n_copies = 3 * EXPERT_WEIGHT_DMA_CHUNKS
    tile_j, tile_n = tile_ids - pstart[tile_e], jnp.maximum(ptiles[tile_e] - 1, 1)
    tile_lo = ((n_copies * tile_j + tile_n - 1) // tile_n).astype(I32)
    tile_hi = ((n_copies * (tile_j + 1) + tile_n - 1) // tile_n).astype(I32)
    pos = _pos(idx, rank, pstart * tm)

    lanes = h2p_all.shape[1]
    tile3 = lambda a2: a2.reshape(-1, TILE_SUBLANES, lanes)
    xs_rows = _sc_scatter_rows(tile3(h2p_all), _tile_pos(pos, SC_CHUNK_ROWS), n_tiles * tm)
    ys = _experts(xs_rows.reshape(-1, lanes), (tile_e, tile_nv, tile_ord, tile_next, tile_lo, tile_hi),
                  w_e_gate[0], w_e_up[0], w_e_down[0])
    yg = _sc_gather_rows(tile3(ys), _tile_pos(pos, COMBINE_TILE).reshape(-1, SC_CHUNK_ROWS)).reshape(-1, lanes)

    wts_t = wts.T
    y_p = _combine(yg, wts_t[:tp], h2p_all, x1p, mod_p[5], g_final, wsg_bf, wsu_bf, wsd_bf, seq, 0)
    y_s = _combine(yg, wts_t[tp:], h2p_all, x1s, mod_s[5], g_final, wsg_bf, wsu_bf, wsd_bf, n_new, tp)

    return (y_p.reshape(bp, seq, d), y_s.reshape(bs, n_new, d),
            kp.reshape(1, bp, seq, n_heads, k_row), vp.reshape(1, bp, seq, n_heads, v_head),
            h_p.reshape(1, bp, rw), conv_p.reshape(1, bp, CONV_WIDTH - 1, rw),
            ks.reshape(1, bs, n_new, n_heads, k_row), vs.reshape(1, bs, n_new, n_heads, v_head),
            h_s.reshape(1, bs, rw), conv_s.reshape(1, bs, CONV_WIDTH - 1, rw))
```

```python
import functools
import math

import jax
import jax.numpy as jnp
import numpy as np
from jax import lax
from jax.experimental import pallas as pl
from jax.experimental.pallas import tpu as pltpu
from jax.experimental.pallas import tpu_sc as plsc

F32 = jnp.float32
BF16 = jnp.bfloat16
I32 = jnp.int32
U32 = jnp.uint32

NORM_EPS = 1e-6
SUBLN_EPS = 1e-5
NEG = -1e30
RG_C = 8.0
ROUTED_SCALE = 2.5
N_GROUPS = 8
TOPK_GROUPS = 4
TOP_K = 8
CONV_WIDTH = 4

V7X_VMEM_LIMIT_BYTES = 56 * 1024 * 1024
EXPERT_TILE_ROWS = 256
EXPERT_WEIGHT_DMA_CHUNKS = 8
EXPERT_WEIGHT_SLOTS = 3
WEIGHT_PREFETCH_DMA_PRIORITY = 1
V7X_SC_CORES = 2
V7X_SC_SUBCORES = 16
COMBINE_TILE = 128
SC_CHUNK_ROWS = 64


def _cparams(n_axes):
    return pltpu.CompilerParams(
        dimension_semantics=("arbitrary",) * n_axes, vmem_limit_bytes=V7X_VMEM_LIMIT_BYTES
    )


def _pick(n, candidates):
    for c in candidates:
        if n % c == 0:
            return c
    return n


def _dot(a, b):
    return jnp.dot(a, b, preferred_element_type=F32)


def _dot_nt(a, b):
    return lax.dot_general(a, b, (((1,), (1,)), ((), ())), preferred_element_type=F32)


def _split(x):
    hi = x.astype(BF16)
    lo = (x - hi.astype(F32)).astype(BF16)
    return hi, lo


def _rms(x, eps):
    return x * lax.rsqrt(jnp.mean(x * x, axis=-1, keepdims=True) + eps)


def _silu(x):
    return x * jax.nn.sigmoid(x)


def _bf16_terms(x, n):
    terms = []
    for _ in range(n):
        bits = np.float32(x).view(np.uint32)
        bits = (bits + np.uint32(0x7FFF) + ((bits >> np.uint32(16)) & np.uint32(1))) & np.uint32(0xFFFF0000)
        t = float(bits.view(np.float32))
        terms.append(t)
        x -= t
    return tuple(terms)


LOG2E = math.log2(math.e)
LOG2E_BF16_TERMS = _bf16_terms(LOG2E, 3)
TILE_SUBLANES = 8


def _rows_from_tiles(ref, n_rows):
    return jnp.concatenate(
        [ref[pl.ds(sub, n_rows, stride=TILE_SUBLANES), :] for sub in range(TILE_SUBLANES)], axis=-1)


def _rows_to_tiles(ref, x):
    n_rows, width = x.shape
    lanes = width // TILE_SUBLANES
    for sub in range(TILE_SUBLANES):
        ref[pl.ds(sub, n_rows, stride=TILE_SUBLANES), :] = x[:, sub * lanes:(sub + 1) * lanes]


def _unpack_halves(xp):
    lo = pltpu.unpack_elementwise(xp, index=0, packed_dtype=BF16, unpacked_dtype=F32)
    hi = pltpu.unpack_elementwise(xp, index=1, packed_dtype=BF16, unpacked_dtype=F32)
    return lo.astype(BF16), hi.astype(BF16)


def _pack_halves(x):
    n = x.shape[-1] // 2
    return pltpu.pack_elementwise([x[:, :n], x[:, n:]], packed_dtype=BF16)


def _ada_kernel(c_ref, w_ref, b_ref, o_ref):
    a_hi, a_lo = _split(_silu(c_ref[...]))
    w_hi, w_lo = _split(w_ref[...])
    o_ref[...] = _dot(a_hi, w_hi) + (_dot(a_hi, w_lo) + _dot(a_lo, w_hi)) + b_ref[...]


def _ada(c, w, b):
    n, d = c.shape
    d_out = w.shape[1]
    tn = _pick(d_out, (512, 256, 128))
    return pl.pallas_call(
        _ada_kernel,
        grid=(d_out // tn,),
        in_specs=[
            pl.BlockSpec((n, d), lambda j: (0, 0)),
            pl.BlockSpec((d, tn), lambda j: (0, j)),
            pl.BlockSpec((1, tn), lambda j: (0, j)),
        ],
        out_specs=pl.BlockSpec((n, tn), lambda j: (0, j)),
        out_shape=jax.ShapeDtypeStruct((n, d_out), F32),
        compiler_params=_cparams(1),
        name="ada",
    )(c, w, b.reshape(1, d_out))


def _mod_spec(rows_per_batch, tm, d, n_tiles=None):
    clamp = (lambda i: i) if n_tiles is None else (lambda i: jnp.minimum(i, n_tiles - 1))
    if rows_per_batch % tm == 0:
        per = rows_per_batch // tm
        return pl.BlockSpec((None, 1, d), lambda i, *_: (clamp(i) // per, 0, 0))
    return pl.BlockSpec((None, tm, d), lambda i, *_: (clamp(i), 0, 0))


def _mod_array(m, rows_per_batch, tm):
    nb, d = m.shape
    if rows_per_batch % tm == 0:
        return m.reshape(nb, 1, d)
    assert tm % rows_per_batch == 0
    return jnp.repeat(m, rows_per_batch, axis=0).reshape(nb * rows_per_batch // tm, tm, d)


def _inproj_kernel(x_ref, sh_ref, sc_ref, g_ref, w_ref,
                   q_ref, k_ref, v_ref, u_ref, gt_ref, kb_ref, vb_ref, h_scr, *, qscale):
    j = pl.program_id(1)

    @pl.when(j == 0)
    def _():
        y = _rms(x_ref[...], NORM_EPS) * g_ref[...]
        h_scr[...] = (y * (1.0 + sc_ref[...]) + sh_ref[...]).astype(BF16)

    z = _dot(h_scr[...], w_ref[...])

    @pl.when(j == 0)
    def _():
        q_ref[...] = (z * qscale).astype(BF16)

    @pl.when(j == 1)
    def _():
        k_ref[...] = z
        kb_ref[...] = z.astype(BF16)

    @pl.when(j == 2)
    def _():
        v_ref[...] = z
        vb_ref[...] = z.astype(BF16)

    @pl.when(j == 3)
    def _():
        u_ref[...] = z

    @pl.when(j == 4)
    def _():
        gt_ref[...] = z


def _inproj(x2d, shift, scale, g, w_bf, rows_per_batch, qscale):
    r, d = x2d.shape
    wd = w_bf.shape[1] // 5
    tm = _pick(r, (512, 256, 128, 64, 32, 16, 8))
    row = lambda i, j: (i, 0)
    f32o = jax.ShapeDtypeStruct((r, wd), F32)
    bfo = jax.ShapeDtypeStruct((r, wd), BF16)
    return pl.pallas_call(
        functools.partial(_inproj_kernel, qscale=qscale),
        grid=(r // tm, 5),
        in_specs=[
            pl.BlockSpec((tm, d), row),
            _mod_spec(rows_per_batch, tm, d),
            _mod_spec(rows_per_batch, tm, d),
            pl.BlockSpec((1, d), lambda i, j: (0, 0)),
            pl.BlockSpec((d, wd), lambda i, j: (0, j)),
        ],
        out_specs=[pl.BlockSpec((tm, wd), row)] * 7,
        out_shape=[bfo, f32o, f32o, f32o, f32o, bfo, bfo],
        scratch_shapes=[pltpu.VMEM((tm, d), BF16)],
        compiler_params=_cparams(2),
        name="inproj",
    )(x2d, _mod_array(shift, rows_per_batch, tm), _mod_array(scale, rows_per_batch, tm), g.reshape(1, d), w_bf)


def _lam(lamv_ref, lam_init):
    lv = lamv_ref[...]
    s1 = jnp.sum(lv[0:1] * lv[1:2], axis=-1, keepdims=True)
    s2 = jnp.sum(lv[2:3] * lv[3:4], axis=-1, keepdims=True)
    return jnp.exp(s1) - jnp.exp(s2) + lam_init


def _online_update(s, v, m_scr, l_scr, acc_scr, shift=None):
    m_old = m_scr[...]
    s_max = jnp.max(s, axis=-1, keepdims=True)
    m_new = jnp.maximum(m_old, s_max if shift is None else s_max + shift)
    alpha = jnp.exp2(m_old - m_new)
    p = jnp.exp2(s - (m_new if shift is None else m_new - shift))
    l_scr[...] = alpha * l_scr[...] + jnp.sum(p, axis=-1, keepdims=True)
    acc_scr[...] = alpha * acc_scr[...] + _dot(p.astype(BF16), v)
    m_scr[...] = m_new


def _pattn_kernel(slopes_ref, lamv_ref, gs_ref, q_ref, k_ref, v_ref, o_ref, q_scr, ka_scr, *state,
                  tq, rc, lam_init):
    n_chunks = 2 * tq // rc
    m_scrs, l_scrs, acc_scrs = state[:n_chunks], state[n_chunks:2 * n_chunks], state[2 * n_chunks:]
    h = pl.program_id(1)
    qi = pl.program_id(2)
    slope = slopes_ref[h]
    hd = q_ref.shape[1]
    half = hd // 2
    n_terms = len(LOG2E_BF16_TERMS)

    q = q_ref[...]
    lane = lax.broadcasted_iota(I32, q.shape, 1)
    zero = jnp.zeros_like(q)
    qa = jnp.zeros(q.shape, F32)
    for n, term in enumerate(LOG2E_BF16_TERMS):
        qa = jnp.where(jnp.logical_or(lane == n, lane == n + n_terms), term, qa)
    qa = qa.astype(BF16)
    q_scr[0:tq, 0:hd] = jnp.where(lane < half, q, zero)
    q_scr[tq:2 * tq, 0:hd] = jnp.where(lane >= half, q, zero)
    q_scr[0:tq, hd:2 * hd] = qa
    q_scr[tq:2 * tq, hd:2 * hd] = qa
    c = lax.broadcasted_iota(I32, (tq, hd), 0)
    c_lo = c % 256
    ka = jnp.where(lane < n_terms, c_lo.astype(F32) * slope,
                   jnp.where(lane < 2 * n_terms, (c - c_lo).astype(F32) * slope, 0.0))
    ka_scr[...] = ka.astype(BF16)

    for m_scr, l_scr, acc_scr in zip(m_scrs, l_scrs, acc_scrs):
        m_scr[...] = jnp.full(m_scr.shape, NEG, F32)
        l_scr[...] = jnp.zeros(l_scr.shape, F32)
        acc_scr[...] = jnp.zeros(acc_scr.shape, F32)

    def step(j, masked):
        start = pl.multiple_of(j * tq, tq)
        kaug = jnp.concatenate([k_ref[pl.ds(start, tq), :], ka_scr[...]], axis=-1)
        v = v_ref[pl.ds(start, tq), :]
        for ci in range(n_chunks):
            r0 = ci * rc
            q0 = r0 % tq
            ncol = q0 + rc if masked else tq
            rowpos = q0 + lax.broadcasted_iota(I32, (rc, 1), 0)
            s = _dot_nt(q_scr[r0:r0 + rc, :], kaug[:ncol])
            if masked:
                local = lax.broadcasted_iota(I32, (rc, rc), 1) <= lax.broadcasted_iota(I32, (rc, rc), 0)
                diag = jnp.where(local, s[:, q0:], NEG)
                s = diag if q0 == 0 else jnp.concatenate([s[:, :q0], diag], axis=1)
            shift = ((j - qi) * tq - rowpos).astype(F32) * (slope * LOG2E)
            _online_update(s, v[:ncol], m_scrs[ci], l_scrs[ci], acc_scrs[ci], shift=shift)

    def body(j, carry):
        step(j, False)
        return carry

    lax.fori_loop(0, qi, body, 0)
    step(qi, True)

    lam = _lam(lamv_ref, lam_init)
    o = jnp.concatenate([acc[...] / l[...] for acc, l in zip(acc_scrs, l_scrs)], axis=0)
    att = o[:tq] - lam * o[tq:]
    att = _rms(att, SUBLN_EPS) * gs_ref[...] * (1.0 - lam_init)
    o_ref[...] = att.astype(BF16)


def _pattn(qb, kb, vb, slopes, lamv, g_subln, n_batch, seq, n_heads, lam_init):
    r, aw = qb.shape
    hd = aw // n_heads
    tq = _pick(seq, (2048, 1024, 512, 256, 128))
    rc = min(tq, 256)
    n_chunks = 2 * tq // rc
    nq = seq // tq
    return pl.pallas_call(
        functools.partial(_pattn_kernel, tq=tq, rc=rc, lam_init=lam_init),
        grid_spec=pltpu.PrefetchScalarGridSpec(
            num_scalar_prefetch=1,
            grid=(n_batch, n_heads, nq),
            in_specs=[
                pl.BlockSpec(lamv.shape, lambda b, h, i, *_: (0, 0)),
                pl.BlockSpec((1, hd), lambda b, h, i, *_: (0, 0)),
                pl.BlockSpec((tq, hd), lambda b, h, i, *_: (b * nq + i, h)),
                pl.BlockSpec((seq, hd), lambda b, h, i, *_: (b, h)),
                pl.BlockSpec((seq, hd), lambda b, h, i, *_: (b, h)),
            ],
            out_specs=pl.BlockSpec((tq, hd), lambda b, h, i, *_: (b * nq + i, h)),
            scratch_shapes=[pltpu.VMEM((2 * tq, 2 * hd), BF16), pltpu.VMEM((tq, hd), BF16)]
            + [pltpu.VMEM((rc, 1), F32)] * (2 * n_chunks) + [pltpu.VMEM((rc, hd), F32)] * n_chunks,
        ),
        out_shape=jax.ShapeDtypeStruct((r, aw), BF16),
        compiler_params=_cparams(3),
        name="pattn",
    )(slopes, lamv, g_subln.reshape(1, hd), qb, kb, vb)


def _sattn_kernel(pt_ref, lamv_ref, gs_ref, wq_ref, kn_ref, vn_ref, *rest,
                  n_pages_step, page, past, n_new, n_heads, lam_init):
    k_refs = rest[:n_pages_step]
    v_refs = rest[n_pages_step:2 * n_pages_step]
    o_ref = rest[2 * n_pages_step]
    m_scr, l_scr, acc_scr = rest[2 * n_pages_step + 1:]
    j = pl.program_id(1)
    n_rows = wq_ref.shape[0]
    hd = wq_ref.shape[1] // n_heads
    tk = n_pages_step * page

    r = lax.broadcasted_iota(I32, (n_rows, 1), 0)
    head = r // (2 * n_new)
    qi = r % n_new
    slope = jnp.exp2(-(head + 1).astype(F32)) * LOG2E

    @pl.when(j == 0)
    def _():
        m_scr[...] = jnp.full(m_scr.shape, NEG, F32)
        l_scr[...] = jnp.zeros(l_scr.shape, F32)
        acc_scr[...] = jnp.zeros(acc_scr.shape, F32)

    wq = wq_ref[...]
    kc = jnp.concatenate([_rows_from_tiles(kr, page).astype(BF16) for kr in k_refs], axis=0)
    vc = jnp.concatenate([_rows_from_tiles(vr, page).astype(BF16) for vr in v_refs], axis=0)
    t = j * tk + lax.broadcasted_iota(I32, (1, tk), 1)
    s = _dot_nt(wq, kc) - slope * (past + qi - t).astype(F32)
    _online_update(s, vc, m_scr, l_scr, acc_scr)

    @pl.when(j == pl.num_programs(1) - 1)
    def _():
        pad = jnp.zeros((page - n_new, kn_ref.shape[1]), BF16)
        kn = jnp.concatenate([kn_ref[...].astype(BF16), pad], axis=0)
        vn = jnp.concatenate([vn_ref[...].astype(BF16), pad], axis=0)
        tj = lax.broadcasted_iota(I32, (1, page), 1)
        sn = _dot_nt(wq, kn) - slope * (qi - tj).astype(F32)
        sn = jnp.where(tj <= qi, sn, NEG)
        _online_update(sn, vn, m_scr, l_scr, acc_scr)

        lam = _lam(lamv_ref, lam_init)
        o = acc_scr[...] / l_scr[...]
        outs = []
        for h in range(n_heads):
            blk = o[h * 2 * n_new:(h + 1) * 2 * n_new, h * hd:(h + 1) * hd]
            att = blk[:n_new] - lam * blk[n_new:]
            outs.append(_rms(att, SUBLN_EPS) * gs_ref[...] * (1.0 - lam_init))
        o_ref[...] = jnp.concatenate(outs, axis=-1).astype(BF16)


def _sattn(wq, k_new, v_new, cache_k2, cache_v2, page_table, lamv, g_subln, n_heads, lam_init):
    n_seq, n_rows, aw = wq.shape
    n_new = k_new.shape[1]
    hd = aw // n_heads
    page = cache_k2.shape[1] // n_heads
    n_pages = page_table.shape[1]
    pstep = _pick(n_pages, (8, 4, 2, 1))
    past = n_pages * page
    assert n_heads == TILE_SUBLANES and hd == 128

    def page_spec(p):
        return pl.BlockSpec((None, page * n_heads, hd), lambda b, j, pt: (pt[b, j * pstep + p], 0, 0))

    seq_spec = lambda rows: pl.BlockSpec((None, rows, aw), lambda b, j, pt: (b, 0, 0))
    return pl.pallas_call(
        functools.partial(_sattn_kernel, n_pages_step=pstep, page=page, past=past, n_new=n_new,
                          n_heads=n_heads, lam_init=lam_init),
        grid_spec=pltpu.PrefetchScalarGridSpec(
            num_scalar_prefetch=1,
            grid=(n_seq, n_pages // pstep),
            in_specs=[
                pl.BlockSpec(lamv.shape, lambda b, j, pt: (0, 0)),
                pl.BlockSpec((1, hd), lambda b, j, pt: (0, 0)),
                seq_spec(n_rows), seq_spec(n_new), seq_spec(n_new),
            ] + [page_spec(p) for p in range(pstep)] * 2,
            out_specs=seq_spec(n_new),
            scratch_shapes=[
                pltpu.VMEM((n_rows, 1), F32),
                pltpu.VMEM((n_rows, 1), F32),
                pltpu.VMEM((n_rows, aw), F32),
            ],
        ),
        out_shape=jax.ShapeDtypeStruct((n_seq, n_new, aw), BF16),
        compiler_params=_cparams(2),
        name="sattn",
    )(page_table, lamv, g_subln.reshape(1, hd), wq, k_new, v_new,
      *([cache_k2] * pstep), *([cache_v2] * pstep))


def _gelu_tanh(x):
    return x * (0.5 * (1.0 + jnp.tanh(math.sqrt(2.0 / math.pi) * (x + 0.044715 * (x * x * x)))))


def _softplus(x):
    return jnp.maximum(x, 0.0) + jnp.log1p(jnp.exp(-jnp.abs(x)))


def _rglru_kernel(u_ref, gt_ref, c0_ref, h0_ref, cw_ref, cb_ref, wai_ref, ba_ref, bi_ref, lam_ref, gn_ref,
                  rec_ref, ht_ref, cout_ref, ubuf, hcar, *, tl, n_blocks):
    t = pl.program_id(1)
    halo = CONV_WIDTH - 1
    base = 8

    @pl.when(t == 0)
    def _():
        ubuf[base - halo:base, :] = c0_ref[...]
        hcar[...] = h0_ref[...]

    ubuf[base:base + tl, :] = u_ref[...]
    cw = cw_ref[...]
    xc = cb_ref[...] + cw[0:1] * ubuf[base - halo:base - halo + tl, :]
    for jj in range(1, CONV_WIDTH):
        xc = xc + cw[jj:jj + 1] * ubuf[base - halo + jj:base - halo + jj + tl, :]
    tail = ubuf[base + tl - halo:base + tl, :]
    ubuf[base - halo:base, :] = tail
    cout_ref[...] = tail

    bw = xc.shape[1] // n_blocks
    za, zi = [], []
    for n in range(n_blocks):
        z = _dot(xc[:, n * bw:(n + 1) * bw].astype(BF16), wai_ref[n])
        za.append(z[:, :bw])
        zi.append(z[:, bw:])
    r = jax.nn.sigmoid(jnp.concatenate(za, axis=-1) + ba_ref[...])
    i = jax.nn.sigmoid(jnp.concatenate(zi, axis=-1) + bi_ref[...])
    log_a = -RG_C * r * _softplus(-lam_ref[...])
    a = jnp.exp(log_a)
    th = jnp.tanh(log_a)
    b = xc * i * jnp.sqrt(-2.0 * th / (1.0 - th))

    rowi = lax.broadcasted_iota(I32, a.shape, 0)
    sft = 1
    while sft < tl:
        keep = rowi >= sft
        a_prev = jnp.where(keep, pltpu.roll(a, sft, 0), 1.0)
        b_prev = jnp.where(keep, pltpu.roll(b, sft, 0), 0.0)
        b = a * b_prev + b
        a = a * a_prev
        sft *= 2
    hs = a * hcar[...] + b
    h_last = hs[tl - 1:tl, :]
    hcar[...] = h_last
    ht_ref[...] = h_last

    rec = hs * _gelu_tanh(gt_ref[...])
    rec_ref[...] = (_rms(rec, NORM_EPS) * gn_ref[...]).astype(BF16)


def _rglru(u2d, gate2d, conv0, h0, conv_w, conv_b, wai_bf, b_a, b_i, rg_lambda, g_rgnorm, n_batch, seq):
    r, w = u2d.shape
    tl = _pick(seq, (256, 128, 64, 32, 16, 8))
    nt = seq // tl
    n_blocks = wai_bf.shape[0]
    halo = CONV_WIDTH - 1
    row = lambda b, t: (b * nt + t, 0)
    vec = pl.BlockSpec((1, w), lambda b, t: (0, 0))
    rec, ht, cout = pl.pallas_call(
        functools.partial(_rglru_kernel, tl=tl, n_blocks=n_blocks),
        grid=(n_batch, nt),
        in_specs=[
            pl.BlockSpec((tl, w), row),
            pl.BlockSpec((tl, w), row),
            pl.BlockSpec((None, halo, w), lambda b, t: (b, 0, 0)),
            pl.BlockSpec((None, 1, w), lambda b, t: (b, 0, 0)),
            pl.BlockSpec((CONV_WIDTH, w), lambda b, t: (0, 0)),
            vec,
            pl.BlockSpec(wai_bf.shape, lambda b, t: (0, 0, 0)),
            vec, vec, vec, vec,
        ],
        out_specs=[
            pl.BlockSpec((tl, w), row),
            pl.BlockSpec((None, 1, w), lambda b, t: (b, 0, 0)),
            pl.BlockSpec((None, halo, w), lambda b, t: (b, 0, 0)),
        ],
        out_shape=[
            jax.ShapeDtypeStruct((r, w), BF16),
            jax.ShapeDtypeStruct((n_batch, 1, w), F32),
            jax.ShapeDtypeStruct((n_batch, halo, w), F32),
        ],
        scratch_shapes=[pltpu.VMEM((tl + 8, w), F32), pltpu.VMEM((1, w), F32)],
        compiler_params=_cparams(2),
        name="rglru",
    )(u2d, gate2d, conv0, h0.reshape(n_batch, 1, w), conv_w, conv_b.reshape(1, w), wai_bf,
      b_a.reshape(1, w), b_i.reshape(1, w), rg_lambda.reshape(1, w), g_rgnorm.reshape(1, w))
    return rec, ht.reshape(n_batch, w), cout


def _oproj_kernel(att_ref, rec_ref, x_ref, g1_ref, sh_ref, sc_ref, gn_ref, wo_ref, wrt_ref, *rest,
                  aliased, n_tiles):
    x1_ref, h2p_ref, st_ref = rest[2:] if aliased else rest
    aw = att_ref.shape[1]
    i = pl.program_id(0)

    @pl.when(i < n_tiles)
    def _():
        mix = _dot(att_ref[...], wo_ref[:aw, :]) + _dot(rec_ref[...], wo_ref[aw:, :])
        x1 = x_ref[...] + g1_ref[...] * mix
        x1_ref[...] = x1
        h2 = (_rms(x1, NORM_EPS) * gn_ref[...]) * (1.0 + sc_ref[...]) + sh_ref[...]
        _rows_to_tiles(h2p_ref, _pack_halves(h2))
        h_hi, h_lo = _split(h2)
        w_hi, w_lo = _split(wrt_ref[...])
        logits_t = _dot_nt(w_hi, h_hi) + (_dot_nt(w_hi, h_lo) + _dot_nt(w_lo, h_hi))
        st_ref[...] = jax.nn.sigmoid(logits_t)

    @pl.when(i >= n_tiles)
    def _():
        h2p_ref[...] = jnp.zeros(h2p_ref.shape, U32)
        st_ref[...] = jnp.zeros(st_ref.shape, F32)


def _oproj(att, rec, x2d, g1, shift, scale, g_norm2, wo_bf, wr_t, rows_per_batch, row_offset, total_rows,
           h2p_all=None, st_all=None):
    r, d = x2d.shape
    aw = att.shape[1]
    n_exp = wr_t.shape[0]
    tm = _pick(r, (256, 128))
    assert row_offset % tm == 0
    off = row_offset // tm
    lanes = d // 2 // TILE_SUBLANES
    assert lanes == 128
    aliased = h2p_all is not None
    n_tiles = r // tm
    n_fill = 0 if aliased else (total_rows - r) // tm
    assert aliased or (row_offset == 0 and (total_rows - r) % tm == 0)
    row = lambda i: (jnp.minimum(i, n_tiles - 1), 0)
    mspec = _mod_spec(rows_per_batch, tm, d, n_tiles)
    in_specs = [
        pl.BlockSpec((tm, aw), row),
        pl.BlockSpec((tm, d - aw), row),
        pl.BlockSpec((tm, d), row),
        mspec, mspec, mspec,
        pl.BlockSpec((1, d), lambda i: (0, 0)),
        pl.BlockSpec((d, d), lambda i: (0, 0)),
        pl.BlockSpec((n_exp, d), lambda i: (0, 0)),
    ]
    args = [att, rec, x2d, _mod_array(g1, rows_per_batch, tm), _mod_array(shift, rows_per_batch, tm),
            _mod_array(scale, rows_per_batch, tm), g_norm2.reshape(1, d), wo_bf, wr_t]
    io_alias = {}
    if aliased:
        in_specs += [pl.BlockSpec(memory_space=pl.ANY), pl.BlockSpec(memory_space=pl.ANY)]
        io_alias = {len(args): 1, len(args) + 1: 2}
        args += [h2p_all, st_all]
    return pl.pallas_call(
        functools.partial(_oproj_kernel, aliased=aliased, n_tiles=n_tiles),
        grid=(n_tiles + n_fill,),
        in_specs=in_specs,
        out_specs=[
            pl.BlockSpec((tm, d), row),
            pl.BlockSpec((tm * TILE_SUBLANES, lanes), lambda i: (i + off, 0)),
            pl.BlockSpec((n_exp, tm), lambda i: (0, i + off)),
        ],
        out_shape=[
            jax.ShapeDtypeStruct((r, d), F32),
            jax.ShapeDtypeStruct((total_rows * TILE_SUBLANES, lanes), U32),
            jax.ShapeDtypeStruct((n_exp, total_rows), F32),
        ],
        input_output_aliases=io_alias,
        compiler_params=_cparams(1),
        name="oproj",
    )(*args)


def _route_kernel(st_ref, rb_ref, idx_ref, wts_ref, rank_ref, cnt_ref, carry):
    i = pl.program_id(0)
    n_exp, tr = st_ref.shape
    gsz = n_exp // N_GROUPS

    @pl.when(i == 0)
    def _():
        carry[...] = jnp.zeros(carry.shape, F32)

    s = st_ref[...]
    biased = s + rb_ref[...]
    g = biased.reshape(N_GROUPS, gsz, tr)
    within = lax.broadcasted_iota(I32, g.shape, 1)
    m1 = jnp.max(g, axis=1, keepdims=True)
    first = jnp.min(jnp.where(g == m1, within, gsz), axis=1, keepdims=True)
    m2 = jnp.max(jnp.where(within == first, -jnp.inf, g), axis=1, keepdims=True)
    gscore = (m1 + m2).reshape(N_GROUPS, tr)

    gidx = lax.broadcasted_iota(I32, gscore.shape, 0)
    gsel = jnp.zeros(gscore.shape, F32)
    for _ in range(TOPK_GROUPS):
        mg = jnp.max(gscore, axis=0, keepdims=True)
        fg = jnp.min(jnp.where(gscore == mg, gidx, N_GROUPS), axis=0, keepdims=True)
        hit = gidx == fg
        gsel = jnp.where(hit, 1.0, gsel)
        gscore = jnp.where(hit, -jnp.inf, gscore)
    masked = jnp.where(gsel.reshape(N_GROUPS, 1, tr) > 0.5, g, -jnp.inf).reshape(n_exp, tr)

    eidx = lax.broadcasted_iota(I32, (n_exp, tr), 0)
    idxs, ws = [], []
    chosen = jnp.zeros((n_exp, tr), jnp.bool_)
    for _ in range(TOP_K):
        mv = jnp.max(masked, axis=0, keepdims=True)
        fe = jnp.min(jnp.where(masked == mv, eidx, n_exp), axis=0, keepdims=True)
        hit = eidx == fe
        idxs.append(fe)
        ws.append(jnp.sum(jnp.where(hit, s, 0.0), axis=0, keepdims=True))
        chosen = chosen | hit
        masked = jnp.where(hit, -jnp.inf, masked)
    idx = jnp.concatenate(idxs, axis=0)
    w = jnp.concatenate(ws, axis=0)
    idx_ref[...] = idx
    wts_ref[...] = w / jnp.sum(w, axis=0, keepdims=True) * ROUTED_SCALE

    cmat = jnp.where(chosen, 1.0, 0.0)
    before = lax.broadcasted_iota(I32, (tr, tr), 0) < lax.broadcasted_iota(I32, (tr, tr), 1)
    prior = _dot(cmat.astype(BF16), jnp.where(before, 1.0, 0.0).astype(BF16)) + carry[...]
    ranks = [jnp.sum(jnp.where(eidx == idxs[k], prior, 0.0), axis=0, keepdims=True) for k in range(TOP_K)]
    rank_ref[...] = jnp.concatenate(ranks, axis=0).astype(I32)
    carry[...] = carry[...] + jnp.sum(cmat, axis=1, keepdims=True)
    cnt_ref[...] = carry[...].astype(I32)


def _route(st_all, router_bias):
    n_exp, t_all = st_all.shape
    tr = _pick(t_all, (640, 512, 256, 128))
    col = lambda i: (0, i)
    o8 = lambda dt: jax.ShapeDtypeStruct((TOP_K, t_all), dt)
    return pl.pallas_call(
        _route_kernel,
        grid=(t_all // tr,),
        in_specs=[pl.BlockSpec((n_exp, tr), col), pl.BlockSpec((n_exp, 1), lambda i: (0, 0))],
        out_specs=[pl.BlockSpec((TOP_K, tr), col)] * 3 + [pl.BlockSpec((n_exp, 1), lambda i: (0, 0))],
        out_shape=[o8(I32), o8(F32), o8(I32), jax.ShapeDtypeStruct((n_exp, 1), I32)],
        scratch_shapes=[pltpu.VMEM((n_exp, 1), F32)],
        compiler_params=_cparams(1),
        name="route",
    )(st_all, router_bias.reshape(n_exp, 1))


def _pos_kernel(idx_ref, rank_ref, start_ref, pos_ref):
    n_exp = start_ref.shape[0]
    tr = idx_ref.shape[1]
    eidx = lax.broadcasted_iota(I32, (n_exp, tr), 0)
    start = start_ref[...]
    rows = [jnp.sum(jnp.where(eidx == idx_ref[k:k + 1, :], start, 0.0), axis=0, keepdims=True)
            for k in range(TOP_K)]
    pos_ref[...] = jnp.concatenate(rows, axis=0).astype(I32) + rank_ref[...]


def _pos(idx, rank, start_rows):
    n_exp = start_rows.shape[0]
    t_all = idx.shape[1]
    tr = _pick(t_all, (640, 512, 256, 128))
    col = pl.BlockSpec((TOP_K, tr), lambda i: (0, i))
    return pl.pallas_call(
        _pos_kernel,
        grid=(t_all // tr,),
        in_specs=[col, col, pl.BlockSpec((n_exp, 1), lambda i: (0, 0))],
        out_specs=col,
        out_shape=jax.ShapeDtypeStruct((TOP_K, t_all), I32),
        compiler_params=_cparams(1),
        name="pos",
    )(idx, rank, start_rows.astype(F32).reshape(n_exp, 1))


def _tile_pos(pos, tile):
    k, t = pos.shape
    return pos.reshape(k, t // tile, tile).transpose(1, 0, 2)


def _sc_worker_chunks(n_chunks, fn):
    n_workers = V7X_SC_CORES * V7X_SC_SUBCORES
    worker = lax.axis_index("core") * V7X_SC_SUBCORES + lax.axis_index("subcore")

    @pl.loop(0, -(-n_chunks // n_workers))
    def _(it):
        chunk = it * n_workers + worker

        @pl.when(chunk < n_chunks)
        def _():
            fn(chunk)


def _sc_mesh():
    return plsc.VectorSubcoreMesh(core_axis_name="core", subcore_axis_name="subcore",
                                  num_cores=V7X_SC_CORES, num_subcores=V7X_SC_SUBCORES)


def _sc_scatter_rows(x3, idx3, n_rows):
    n_chunks, n_k, width = idx3.shape
    row = x3.shape[1:]

    @pl.kernel(out_type=jax.ShapeDtypeStruct((n_rows,) + row, x3.dtype), mesh=_sc_mesh(),
               scratch_types=[pltpu.VMEM((width,) + row, x3.dtype), pltpu.VMEM((n_k, width), I32),
                              pltpu.SemaphoreType.DMA],
               name="dispatch_sc")
    def scatter(x_hbm, i_hbm, o_hbm, xbuf, ibuf, sem):
        def one(chunk):
            pltpu.sync_copy(x_hbm.at[pl.ds(chunk * width, width)], xbuf)
            pltpu.sync_copy(i_hbm.at[chunk], ibuf)
            copies = [pltpu.make_async_copy(xbuf, o_hbm.at[ibuf.at[k]], sem) for k in range(n_k)]
            for cp in copies:
                cp.start()
            for cp in copies:
                cp.wait()

        _sc_worker_chunks(n_chunks, one)

    return scatter(x3, idx3)


def _sc_gather_rows(src3, idx2):
    n_chunks, width = idx2.shape
    row = src3.shape[1:]

    @pl.kernel(out_type=jax.ShapeDtypeStruct((n_chunks * width,) + row, src3.dtype), mesh=_sc_mesh(),
               scratch_types=[pltpu.VMEM((width,) + row, src3.dtype), pltpu.VMEM((width,), I32),
                              pltpu.SemaphoreType.DMA],
               name="combine_sc")
    def gather(s_hbm, i_hbm, o_hbm, buf, ibuf, sem):
        def one(chunk):
            pltpu.sync_copy(i_hbm.at[chunk], ibuf)
            pltpu.async_copy(s_hbm.at[ibuf], buf, sem).wait()
            pltpu.sync_copy(buf, o_hbm.at[pl.ds(chunk * width, width)])

        _sc_worker_chunks(n_chunks, one)

    return gather(src3, idx2)


def _swiglu_packed(xp, wg, wu, wd):
    xa, xb = _unpack_halves(xp)
    half = xp.shape[1]
    g = _dot(xa, wg[:half, :]) + _dot(xb, wg[half:, :])
    u = _dot(xa, wu[:half, :]) + _dot(xb, wu[half:, :])
    return _dot((_silu(g) * u).astype(BF16), wd[...])


def _experts_kernel(te_ref, tnv_ref, tord_ref, tnext_ref, tnext2_ref, tlo_ref, thi_ref,
                    xs_ref, wg_hbm, wu_hbm, wd_hbm, ys_ref, wg_f, wu_f, wd_f, wg_b, wu_b, wd_b, sems):
    i = pl.program_id(0)
    nv = tnv_ref[i]
    expert = te_ref[i]
    new_expert = jnp.logical_or(i == 0, expert != te_ref[jnp.maximum(i - 1, 0)])
    slot = tord_ref[i] % EXPERT_WEIGHT_SLOTS
    tm = ys_ref.shape[0] // TILE_SUBLANES

    def weight_copies(e, sl):
        copies = []
        for n, (hbm, buf) in enumerate(((wg_hbm, wg_f), (wu_hbm, wu_f), (wd_hbm, wd_f))):
            rows = hbm.shape[1] // EXPERT_WEIGHT_DMA_CHUNKS
            for c in range(EXPERT_WEIGHT_DMA_CHUNKS):
                span = pl.ds(c * rows, rows)
                copies.append(pltpu.make_async_copy(hbm.at[e, span], buf.at[sl, span], sems.at[n, sl]))
        return copies

    @pl.when(jnp.logical_and(nv > 0, new_expert))
    def _():
        @pl.when(i == 0)
        def _():
            for cp in weight_copies(expert, slot):
                cp.start()
            nxt = tnext_ref[i]

            @pl.when(nxt >= 0)
            def _():
                for cp in weight_copies(nxt, (slot + 1) % EXPERT_WEIGHT_SLOTS):
                    cp.start(priority=WEIGHT_PREFETCH_DMA_PRIORITY)

        for cp in weight_copies(expert, slot):
            cp.wait()

    nxt2 = tnext2_ref[i]

    @pl.when(jnp.logical_and(nv > 0, nxt2 >= 0))
    def _():
        lo, hi = tlo_ref[i], thi_ref[i]
        for c, cp in enumerate(weight_copies(nxt2, (slot + 2) % EXPERT_WEIGHT_SLOTS)):
            @pl.when(jnp.logical_and(lo <= c, c < hi))
            def _():
                cp.start(priority=WEIGHT_PREFETCH_DMA_PRIORITY)

    @pl.when(jnp.logical_and(nv > 0, new_expert))
    def _():
        wg_b[...] = wg_f[slot].astype(BF16)
        wu_b[...] = wu_f[slot].astype(BF16)
        wd_b[...] = wd_f[slot].astype(BF16)

    @pl.when(nv > 0)
    def _():
        xp = _rows_from_tiles(xs_ref, tm)
        rowi = lax.broadcasted_iota(I32, xp.shape, 0)
        xp = jnp.where(rowi < nv, xp, jnp.uint32(0))
        _rows_to_tiles(ys_ref, _pack_halves(_swiglu_packed(xp, wg_b, wu_b, wd_b)))

    @pl.when(nv == 0)
    def _():
        ys_ref[...] = jnp.zeros(ys_ref.shape, U32)


def _experts(xs, tile_tables, w_gate, w_up, w_down):
    rows, lanes = xs.shape
    tm = EXPERT_TILE_ROWS
    n_exp, d, ff = w_gate.shape
    blk = pl.BlockSpec((tm * TILE_SUBLANES, lanes), lambda i, *_: (i, 0))
    hbm = pl.BlockSpec(memory_space=pl.ANY)
    return pl.pallas_call(
        _experts_kernel,
        grid_spec=pltpu.PrefetchScalarGridSpec(
            num_scalar_prefetch=len(tile_tables),
            grid=(rows // (tm * TILE_SUBLANES),),
            in_specs=[blk, hbm, hbm, hbm],
            out_specs=blk,
            scratch_shapes=[
                pltpu.VMEM((EXPERT_WEIGHT_SLOTS, d, ff), F32), pltpu.VMEM((EXPERT_WEIGHT_SLOTS, d, ff), F32),
                pltpu.VMEM((EXPERT_WEIGHT_SLOTS, ff, d), F32),
                pltpu.VMEM((d, ff), BF16), pltpu.VMEM((d, ff), BF16), pltpu.VMEM((ff, d), BF16),
                pltpu.SemaphoreType.DMA((3, EXPERT_WEIGHT_SLOTS)),
            ],
        ),
        out_shape=jax.ShapeDtypeStruct((rows, lanes), U32),
        compiler_params=_cparams(1),
        name="experts",
    )(*tile_tables, xs, w_gate, w_up, w_down)


def _combine_kernel(w_ref, yg_ref, h2p_ref, x1_ref, g2_ref, gf_ref, wsg_ref, wsu_ref, wsd_ref, o_ref, *, tc):
    shared = _swiglu_packed(_rows_from_tiles(h2p_ref, tc), wsg_ref, wsu_ref, wsd_ref)
    w = w_ref[...]
    lo, hi = None, None
    for k in range(TOP_K):
        rows = tc * TILE_SUBLANES
        yk = _rows_from_tiles(yg_ref.at[pl.ds(k * rows, rows)], tc)
        wk = w[:, k:k + 1]
        yl = pltpu.unpack_elementwise(yk, index=0, packed_dtype=BF16, unpacked_dtype=F32) * wk
        yh = pltpu.unpack_elementwise(yk, index=1, packed_dtype=BF16, unpacked_dtype=F32) * wk
        lo = yl if lo is None else lo + yl
        hi = yh if hi is None else hi + yh
    routed = jnp.concatenate([lo, hi], axis=-1)
    x2 = x1_ref[...] + g2_ref[...] * (routed + shared)
    o_ref[...] = _rms(x2, NORM_EPS) * gf_ref[...]


def _combine(yg, wts_t, h2p_all, x1, g2, g_final, wsg_bf, wsu_bf, wsd_bf, rows_per_batch, row_offset):
    r, d = x1.shape
    lanes = h2p_all.shape[1]
    tc = COMBINE_TILE
    assert r % tc == 0 and row_offset % tc == 0
    off = row_offset // tc
    ff = wsg_bf.shape[1]
    const = lambda shp: pl.BlockSpec(shp, lambda i: (0,) * len(shp))
    return pl.pallas_call(
        functools.partial(_combine_kernel, tc=tc),
        grid=(r // tc,),
        in_specs=[
            pl.BlockSpec((tc, TOP_K), lambda i: (i, 0)),
            pl.BlockSpec((TOP_K * tc * TILE_SUBLANES, lanes), lambda i: (i + off, 0)),
            pl.BlockSpec((tc * TILE_SUBLANES, lanes), lambda i: (i + off, 0)),
            pl.BlockSpec((tc, d), lambda i: (i, 0)),
            _mod_spec(rows_per_batch, tc, d),
            const((1, d)), const((d, ff)), const((d, ff)), const((ff, d)),
        ],
        out_specs=pl.BlockSpec((tc, d), lambda i: (i, 0)),
        out_shape=jax.ShapeDtypeStruct((r, d), F32),
        compiler_params=_cparams(1),
        name="combine",
    )(wts_t, yg, h2p_all, x1, _mod_array(g2, rows_per_batch, tc), g_final.reshape(1, d),
      wsg_bf, wsu_bf, wsd_bf)


def kernel(x_prompt, x_sample, cache_k, cache_v, state_h, state_conv, page_table, c_prompt, c_sample,
           w_ada, b_ada, g_norm1, w_in, lambda_q1, lambda_k1, lambda_q2, lambda_k2, g_subln,
           conv_w, conv_b, w_rg_a, b_rg_a, w_rg_i, b_rg_i, rg_lambda, g_rgnorm, w_o, g_norm2,
           w_router, router_bias, w_e_gate, w_e_up, w_e_down, w_s_gate, w_s_up, w_s_down, g_final):
    depth = w_ada.shape[0]
    assert depth == 1, "single-layer step"
    bp, seq, d = x_prompt.shape
    bs, n_new, _ = x_sample.shape
    n_heads = cache_k.shape[3]
    k_row = cache_k.shape[4]
    v_head = cache_v.shape[4]
    aw = n_heads * v_head
    rw = d - aw
    assert k_row == v_head and w_in.shape[2] == 3 * aw + 2 * rw and aw == rw
    qk_half = k_row // 2
    n_exp = w_router.shape[2]
    lam_init = 0.8 - 0.6 * math.exp(-0.3 * 0)
    tp, ts = bp * seq, bs * n_new
    t_all = tp + ts

    w_in_bf = w_in[0].astype(BF16)
    wo_bf = w_o[0].astype(BF16)
    wr_t = w_router[0].T
    wai_bf = jnp.concatenate([w_rg_a[0], w_rg_i[0]], axis=-1).astype(BF16)
    wsg_bf, wsu_bf, wsd_bf = w_s_gate[0].astype(BF16), w_s_up[0].astype(BF16), w_s_down[0].astype(BF16)
    lamv = jnp.stack([lambda_q1[0], lambda_k1[0], lambda_q2[0], lambda_k2[0]])
    slopes = jnp.exp2(-8.0 * jnp.arange(1, n_heads + 1, dtype=F32) / n_heads)

    mod = _ada(jnp.concatenate([c_prompt, c_sample], axis=0), w_ada[0], b_ada[0])
    mod_p = [mod[:bp, i * d:(i + 1) * d] for i in range(6)]
    mod_s = [mod[bp:, i * d:(i + 1) * d] for i in range(6)]

    xp2, xs2 = x_prompt.reshape(tp, d), x_sample.reshape(ts, d)
    qscale = qk_half ** -0.5 * LOG2E
    qp, kp, vp, up, gp, kpb, vpb = _inproj(xp2, mod_p[0], mod_p[1], g_norm1[0], w_in_bf, seq, qscale)
    qs, ks, vs, us, gs, ksb, vsb = _inproj(xs2, mod_s[0], mod_s[1], g_norm1[0], w_in_bf, n_new, qscale)

    att_p = _pattn(qp, kpb, vpb, slopes, lamv, g_subln[0], bp, seq, n_heads, lam_init)

    n_grp = 2 * n_heads
    own = (jnp.arange(aw)[None, :] // qk_half) == (jnp.arange(n_grp * n_new)[:, None] // n_new)
    wq = jnp.where(own[None], jnp.tile(qs.reshape(bs, n_new, aw), (1, n_grp, 1)), jnp.zeros((), BF16))
    n_pool, page = cache_k.shape[1], cache_k.shape[2]
    att_s = _sattn(wq, ksb.reshape(bs, n_new, aw), vsb.reshape(bs, n_new, aw),
                   cache_k.reshape(n_pool, page * n_heads, k_row), cache_v.reshape(n_pool, page * n_heads, v_head),
                   page_table, lamv, g_subln[0], n_heads, lam_init).reshape(ts, aw)

    rg_args = (conv_w[0], conv_b[0], wai_bf, b_rg_a[0], b_rg_i[0], rg_lambda[0], g_rgnorm[0])
    rec_p, h_p, conv_p = _rglru(up, gp, jnp.zeros((bp, CONV_WIDTH - 1, rw), F32), jnp.zeros((bp, rw), F32),
                                *rg_args, bp, seq)
    rec_s, h_s, conv_s = _rglru(us, gs, state_conv[0], state_h[0], *rg_args, bs, n_new)

    x1p, h2p_all, st_all = _oproj(att_p, rec_p, xp2, mod_p[2], mod_p[3], mod_p[4], g_norm2[0], wo_bf, wr_t,
                                  seq, 0, t_all)
    x1s, h2p_all, st_all = _oproj(att_s, rec_s, xs2, mod_s[2], mod_s[3], mod_s[4], g_norm2[0], wo_bf, wr_t,
                                  n_new, tp, t_all, h2p_all, st_all)

    idx, wts, rank, counts = _route(st_all, router_bias[0])

    tm = EXPERT_TILE_ROWS
    counts = counts.reshape(n_exp)
    ptiles = (counts + tm - 1) // tm
    pend = jnp.cumsum(ptiles)
    pstart = pend - ptiles
    n_tiles = (t_all * TOP_K) // tm + n_exp
    tile_ids = jnp.arange(n_tiles, dtype=I32)
    tile_e = jnp.minimum(jnp.sum(pend[None, :] <= tile_ids[:, None], axis=1), n_exp - 1).astype(I32)
    tile_nv = jnp.where(tile_ids < pend[-1],
                        jnp.clip(counts[tile_e] - (tile_ids - pstart[tile_e]) * tm, 0, tm), 0).astype(I32)
    tile_ord = (jnp.cumsum(ptiles > 0) - 1)[tile_e].astype(I32)
    next_tile = pend[tile_e]
    e_next = tile_e[jnp.minimum(next_tile, n_tiles - 1)]
    tile_next = jnp.where(next_tile < pend[-1], e_next, -1).astype(I32)
    next_tile2 = pend[e_next]
    tile_next2 = jnp.where(jnp.logical_and(next_tile < pend[-1], next_tile2 < pend[-1]),
                           tile_e[jnp.minimum(next_tile2, n_tiles - 1)], -1).astype(I32)
    n_copies = 3 * EXPERT_WEIGHT_DMA_CHUNKS
    tile_j, tile_n = tile_ids - pstart[tile_e], jnp.maximum(ptiles[tile_e] - 1, 1)
    tile_lo = ((n_copies * tile_j + tile_n - 1) // tile_n).astype(I32)
    tile_hi = ((n_copies * (tile_j + 1) + tile_n - 1) // tile_n).astype(I32)
    pos = _pos(idx, rank, pstart * tm)

    lanes = h2p_all.shape[1]
    tile3 = lambda a2: a2.reshape(-1, TILE_SUBLANES, lanes)
    xs_rows = _sc_scatter_rows(tile3(h2p_all), _tile_pos(pos, SC_CHUNK_ROWS), n_tiles * tm)
    ys = _experts(xs_rows.reshape(-1, lanes), (tile_e, tile_nv, tile_ord, tile_next, tile_next2, tile_lo, tile_hi),
                  w_e_gate[0], w_e_up[0], w_e_down[0])
    yg = _sc_gather_rows(tile3(ys), _tile_pos(pos, COMBINE_TILE).reshape(-1, SC_CHUNK_ROWS)).reshape(-1, lanes)

    wts_t = wts.T
    y_p = _combine(yg, wts_t[:tp], h2p_all, x1p, mod_p[5], g_final, wsg_bf, wsu_bf, wsd_bf, seq, 0)
    y_s = _combine(yg, wts_t[tp:], h2p_all, x1s, mod_s[5], g_final, wsg_bf, wsu_bf, wsd_bf, n_new, tp)

    return (y_p.reshape(bp, seq, d), y_s.reshape(bs, n_new, d),
            kp.reshape(1, bp, seq, n_heads, k_row), vp.reshape(1, bp, seq, n_heads, v_head),
            h_p.reshape(1, bp, rw), conv_p.reshape(1, bp, CONV_WIDTH - 1, rw),
            ks.reshape(1, bs, n_new, n_heads, k_row), vs.reshape(1, bs, n_new, n_heads, v_head),
            h_s.reshape(1, bs, rw), conv_s.reshape(1, bs, CONV_WIDTH - 1, rw))
```

```python
import functools
import math

import jax
import jax.numpy as jnp
import numpy as np
from jax import lax
from jax.experimental import pallas as pl
from jax.experimental.pallas import tpu as pltpu
from jax.experimental.pallas import tpu_sc as plsc

F32 = jnp.float32
BF16 = jnp.bfloat16
I32 = jnp.int32
U32 = jnp.uint32

NORM_EPS = 1e-6
SUBLN_EPS = 1e-5
NEG = -1e30
RG_C = 8.0
ROUTED_SCALE = 2.5
N_GROUPS = 8
TOPK_GROUPS = 4
TOP_K = 8
CONV_WIDTH = 4

V7X_VMEM_LIMIT_BYTES = 56 * 1024 * 1024
EXPERT_TILE_ROWS = 256
EXPERT_WEIGHT_DMA_CHUNKS = 8
EXPERT_WEIGHT_SLOTS = 3
WEIGHT_PREFETCH_DMA_PRIORITY = 1
V7X_SC_CORES = 2
V7X_SC_SUBCORES = 16
COMBINE_TILE = 256
SC_CHUNK_ROWS = 64


def _cparams(n_axes):
    return pltpu.CompilerParams(
        dimension_semantics=("arbitrary",) * n_axes, vmem_limit_bytes=V7X_VMEM_LIMIT_BYTES
    )


def _pick(n, candidates):
    for c in candidates:
        if n % c == 0:
            return c
    return n


def _dot(a, b):
    return jnp.dot(a, b, preferred_element_type=F32)


def _dot_nt(a, b):
    return lax.dot_general(a, b, (((1,), (1,)), ((), ())), preferred_element_type=F32)


def _split(x):
    hi = x.astype(BF16)
    lo = (x - hi.astype(F32)).astype(BF16)
    return hi, lo


def _rms(x, eps):
    return x * lax.rsqrt(jnp.mean(x * x, axis=-1, keepdims=True) + eps)


def _silu(x):
    return x * jax.nn.sigmoid(x)


def _bf16_terms(x, n):
    terms = []
    for _ in range(n):
        bits = np.float32(x).view(np.uint32)
        bits = (bits + np.uint32(0x7FFF) + ((bits >> np.uint32(16)) & np.uint32(1))) & np.uint32(0xFFFF0000)
        t = float(bits.view(np.float32))
        terms.append(t)
        x -= t
    return tuple(terms)


LOG2E = math.log2(math.e)
LOG2E_BF16_TERMS = _bf16_terms(LOG2E, 3)
TILE_SUBLANES = 8


def _rows_from_tiles(ref, n_rows):
    return jnp.concatenate(
        [ref[pl.ds(sub, n_rows, stride=TILE_SUBLANES), :] for sub in range(TILE_SUBLANES)], axis=-1)


def _rows_to_tiles(ref, x):
    n_rows, width = x.shape
    lanes = width // TILE_SUBLANES
    for sub in range(TILE_SUBLANES):
        ref[pl.ds(sub, n_rows, stride=TILE_SUBLANES), :] = x[:, sub * lanes:(sub + 1) * lanes]


def _unpack_halves(xp):
    lo = pltpu.unpack_elementwise(xp, index=0, packed_dtype=BF16, unpacked_dtype=F32)
    hi = pltpu.unpack_elementwise(xp, index=1, packed_dtype=BF16, unpacked_dtype=F32)
    return lo.astype(BF16), hi.astype(BF16)


def _pack_halves(x):
    n = x.shape[-1] // 2
    return pltpu.pack_elementwise([x[:, :n], x[:, n:]], packed_dtype=BF16)


def _ada_kernel(c_ref, w_ref, b_ref, o_ref):
    a_hi, a_lo = _split(_silu(c_ref[...]))
    w_hi, w_lo = _split(w_ref[...])
    o_ref[...] = _dot(a_hi, w_hi) + (_dot(a_hi, w_lo) + _dot(a_lo, w_hi)) + b_ref[...]


def _ada(c, w, b):
    n, d = c.shape
    d_out = w.shape[1]
    tn = _pick(d_out, (512, 256, 128))
    return pl.pallas_call(
        _ada_kernel,
        grid=(d_out // tn,),
        in_specs=[
            pl.BlockSpec((n, d), lambda j: (0, 0)),
            pl.BlockSpec((d, tn), lambda j: (0, j)),
            pl.BlockSpec((1, tn), lambda j: (0, j)),
        ],
        out_specs=pl.BlockSpec((n, tn), lambda j: (0, j)),
        out_shape=jax.ShapeDtypeStruct((n, d_out), F32),
        compiler_params=_cparams(1),
        name="ada",
    )(c, w, b.reshape(1, d_out))


def _mod_spec(rows_per_batch, tm, d, n_tiles=None):
    clamp = (lambda i: i) if n_tiles is None else (lambda i: jnp.minimum(i, n_tiles - 1))
    if rows_per_batch % tm == 0:
        per = rows_per_batch // tm
        return pl.BlockSpec((None, 1, d), lambda i, *_: (clamp(i) // per, 0, 0))
    return pl.BlockSpec((None, tm, d), lambda i, *_: (clamp(i), 0, 0))


def _mod_array(m, rows_per_batch, tm):
    nb, d = m.shape
    if rows_per_batch % tm == 0:
        return m.reshape(nb, 1, d)
    assert tm % rows_per_batch == 0
    return jnp.repeat(m, rows_per_batch, axis=0).reshape(nb * rows_per_batch // tm, tm, d)


def _inproj_kernel(x_ref, sh_ref, sc_ref, g_ref, w_ref,
                   q_ref, k_ref, v_ref, u_ref, gt_ref, kb_ref, vb_ref, h_scr, *, qscale):
    j = pl.program_id(1)

    @pl.when(j == 0)
    def _():
        y = _rms(x_ref[...], NORM_EPS) * g_ref[...]
        h_scr[...] = (y * (1.0 + sc_ref[...]) + sh_ref[...]).astype(BF16)

    z = _dot(h_scr[...], w_ref[...])

    @pl.when(j == 0)
    def _():
        q_ref[...] = (z * qscale).astype(BF16)

    @pl.when(j == 1)
    def _():
        k_ref[...] = z
        kb_ref[...] = z.astype(BF16)

    @pl.when(j == 2)
    def _():
        v_ref[...] = z
        vb_ref[...] = z.astype(BF16)

    @pl.when(j == 3)
    def _():
        u_ref[...] = z

    @pl.when(j == 4)
    def _():
        gt_ref[...] = z


def _inproj(x2d, shift, scale, g, w_bf, rows_per_batch, qscale):
    r, d = x2d.shape
    wd = w_bf.shape[1] // 5
    tm = _pick(r, (512, 256, 128, 64, 32, 16, 8))
    row = lambda i, j: (i, 0)
    f32o = jax.ShapeDtypeStruct((r, wd), F32)
    bfo = jax.ShapeDtypeStruct((r, wd), BF16)
    return pl.pallas_call(
        functools.partial(_inproj_kernel, qscale=qscale),
        grid=(r // tm, 5),
        in_specs=[
            pl.BlockSpec((tm, d), row),
            _mod_spec(rows_per_batch, tm, d),
            _mod_spec(rows_per_batch, tm, d),
            pl.BlockSpec((1, d), lambda i, j: (0, 0)),
            pl.BlockSpec((d, wd), lambda i, j: (0, j)),
        ],
        out_specs=[pl.BlockSpec((tm, wd), row)] * 7,
        out_shape=[bfo, f32o, f32o, f32o, f32o, bfo, bfo],
        scratch_shapes=[pltpu.VMEM((tm, d), BF16)],
        compiler_params=_cparams(2),
        name="inproj",
    )(x2d, _mod_array(shift, rows_per_batch, tm), _mod_array(scale, rows_per_batch, tm), g.reshape(1, d), w_bf)


def _lam(lamv_ref, lam_init):
    lv = lamv_ref[...]
    s1 = jnp.sum(lv[0:1] * lv[1:2], axis=-1, keepdims=True)
    s2 = jnp.sum(lv[2:3] * lv[3:4], axis=-1, keepdims=True)
    return jnp.exp(s1) - jnp.exp(s2) + lam_init


def _online_update(s, v, m_scr, l_scr, acc_scr, shift=None):
    m_old = m_scr[...]
    s_max = jnp.max(s, axis=-1, keepdims=True)
    m_new = jnp.maximum(m_old, s_max if shift is None else s_max + shift)
    alpha = jnp.exp2(m_old - m_new)
    p = jnp.exp2(s - (m_new if shift is None else m_new - shift))
    l_scr[...] = alpha * l_scr[...] + jnp.sum(p, axis=-1, keepdims=True)
    acc_scr[...] = alpha * acc_scr[...] + _dot(p.astype(BF16), v)
    m_scr[...] = m_new


def _pattn_kernel(slopes_ref, lamv_ref, gs_ref, q_ref, k_ref, v_ref, o_ref, q_scr, ka_scr, *state,
                  tq, rc, lam_init):
    n_chunks = 2 * tq // rc
    m_scrs, l_scrs, acc_scrs = state[:n_chunks], state[n_chunks:2 * n_chunks], state[2 * n_chunks:]
    h = pl.program_id(1)
    qi = pl.program_id(2)
    slope = slopes_ref[h]
    hd = q_ref.shape[1]
    half = hd // 2
    n_terms = len(LOG2E_BF16_TERMS)

    q = q_ref[...]
    lane = lax.broadcasted_iota(I32, q.shape, 1)
    zero = jnp.zeros_like(q)
    qa = jnp.zeros(q.shape, F32)
    for n, term in enumerate(LOG2E_BF16_TERMS):
        qa = jnp.where(jnp.logical_or(lane == n, lane == n + n_terms), term, qa)
    qa = qa.astype(BF16)
    q_scr[0:tq, 0:hd] = jnp.where(lane < half, q, zero)
    q_scr[tq:2 * tq, 0:hd] = jnp.where(lane >= half, q, zero)
    q_scr[0:tq, hd:2 * hd] = qa
    q_scr[tq:2 * tq, hd:2 * hd] = qa
    c = lax.broadcasted_iota(I32, (tq, hd), 0)
    c_lo = c % 256
    ka = jnp.where(lane < n_terms, c_lo.astype(F32) * slope,
                   jnp.where(lane < 2 * n_terms, (c - c_lo).astype(F32) * slope, 0.0))
    ka_scr[...] = ka.astype(BF16)

    for m_scr, l_scr, acc_scr in zip(m_scrs, l_scrs, acc_scrs):
        m_scr[...] = jnp.full(m_scr.shape, NEG, F32)
        l_scr[...] = jnp.zeros(l_scr.shape, F32)
        acc_scr[...] = jnp.zeros(acc_scr.shape, F32)

    def step(j, masked):
        start = pl.multiple_of(j * tq, tq)
        kaug = jnp.concatenate([k_ref[pl.ds(start, tq), :], ka_scr[...]], axis=-1)
        v = v_ref[pl.ds(start, tq), :]
        for ci in range(n_chunks):
            r0 = ci * rc
            q0 = r0 % tq
            ncol = q0 + rc if masked else tq
            rowpos = q0 + lax.broadcasted_iota(I32, (rc, 1), 0)
            s = _dot_nt(q_scr[r0:r0 + rc, :], kaug[:ncol])
            if masked:
                local = lax.broadcasted_iota(I32, (rc, rc), 1) <= lax.broadcasted_iota(I32, (rc, rc), 0)
                diag = jnp.where(local, s[:, q0:], NEG)
                s = diag if q0 == 0 else jnp.concatenate([s[:, :q0], diag], axis=1)
            shift = ((j - qi) * tq - rowpos).astype(F32) * (slope * LOG2E)
            _online_update(s, v[:ncol], m_scrs[ci], l_scrs[ci], acc_scrs[ci], shift=shift)

    def body(j, carry):
        step(j, False)
        return carry

    lax.fori_loop(0, qi, body, 0)
    step(qi, True)

    lam = _lam(lamv_ref, lam_init)
    o = jnp.concatenate([acc[...] / l[...] for acc, l in zip(acc_scrs, l_scrs)], axis=0)
    att = o[:tq] - lam * o[tq:]
    att = _rms(att, SUBLN_EPS) * gs_ref[...] * (1.0 - lam_init)
    o_ref[...] = att.astype(BF16)


def _pattn(qb, kb, vb, slopes, lamv, g_subln, n_batch, seq, n_heads, lam_init):
    r, aw = qb.shape
    hd = aw // n_heads
    tq = _pick(seq, (2048, 1024, 512, 256, 128))
    rc = min(tq, 256)
    n_chunks = 2 * tq // rc
    nq = seq // tq
    return pl.pallas_call(
        functools.partial(_pattn_kernel, tq=tq, rc=rc, lam_init=lam_init),
        grid_spec=pltpu.PrefetchScalarGridSpec(
            num_scalar_prefetch=1,
            grid=(n_batch, n_heads, nq),
            in_specs=[
                pl.BlockSpec(lamv.shape, lambda b, h, i, *_: (0, 0)),
                pl.BlockSpec((1, hd), lambda b, h, i, *_: (0, 0)),
                pl.BlockSpec((tq, hd), lambda b, h, i, *_: (b * nq + i, h)),
                pl.BlockSpec((seq, hd), lambda b, h, i, *_: (b, h)),
                pl.BlockSpec((seq, hd), lambda b, h, i, *_: (b, h)),
            ],
            out_specs=pl.BlockSpec((tq, hd), lambda b, h, i, *_: (b * nq + i, h)),
            scratch_shapes=[pltpu.VMEM((2 * tq, 2 * hd), BF16), pltpu.VMEM((tq, hd), BF16)]
            + [pltpu.VMEM((rc, 1), F32)] * (2 * n_chunks) + [pltpu.VMEM((rc, hd), F32)] * n_chunks,
        ),
        out_shape=jax.ShapeDtypeStruct((r, aw), BF16),
        compiler_params=_cparams(3),
        name="pattn",
    )(slopes, lamv, g_subln.reshape(1, hd), qb, kb, vb)


def _sattn_kernel(pt_ref, lamv_ref, gs_ref, wq_ref, kn_ref, vn_ref, *rest,
                  n_pages_step, page, past, n_new, n_heads, lam_init):
    k_refs = rest[:n_pages_step]
    v_refs = rest[n_pages_step:2 * n_pages_step]
    o_ref = rest[2 * n_pages_step]
    m_scr, l_scr, acc_scr = rest[2 * n_pages_step + 1:]
    j = pl.program_id(1)
    n_rows = wq_ref.shape[0]
    hd = wq_ref.shape[1] // n_heads
    tk = n_pages_step * page

    r = lax.broadcasted_iota(I32, (n_rows, 1), 0)
    head = r // (2 * n_new)
    qi = r % n_new
    slope = jnp.exp2(-(head + 1).astype(F32)) * LOG2E

    @pl.when(j == 0)
    def _():
        m_scr[...] = jnp.full(m_scr.shape, NEG, F32)
        l_scr[...] = jnp.zeros(l_scr.shape, F32)
        acc_scr[...] = jnp.zeros(acc_scr.shape, F32)

    wq = wq_ref[...]
    kc = jnp.concatenate([_rows_from_tiles(kr, page).astype(BF16) for kr in k_refs], axis=0)
    vc = jnp.concatenate([_rows_from_tiles(vr, page).astype(BF16) for vr in v_refs], axis=0)
    t = j * tk + lax.broadcasted_iota(I32, (1, tk), 1)
    s = _dot_nt(wq, kc) - slope * (past + qi - t).astype(F32)
    _online_update(s, vc, m_scr, l_scr, acc_scr)

    @pl.when(j == pl.num_programs(1) - 1)
    def _():
        pad = jnp.zeros((page - n_new, kn_ref.shape[1]), BF16)
        kn = jnp.concatenate([kn_ref[...].astype(BF16), pad], axis=0)
        vn = jnp.concatenate([vn_ref[...].astype(BF16), pad], axis=0)
        tj = lax.broadcasted_iota(I32, (1, page), 1)
        sn = _dot_nt(wq, kn) - slope * (qi - tj).astype(F32)
        sn = jnp.where(tj <= qi, sn, NEG)
        _online_update(sn, vn, m_scr, l_scr, acc_scr)

        lam = _lam(lamv_ref, lam_init)
        o = acc_scr[...] / l_scr[...]
        outs = []
        for h in range(n_heads):
            blk = o[h * 2 * n_new:(h + 1) * 2 * n_new, h * hd:(h + 1) * hd]
            att = blk[:n_new] - lam * blk[n_new:]
            outs.append(_rms(att, SUBLN_EPS) * gs_ref[...] * (1.0 - lam_init))
        o_ref[...] = jnp.concatenate(outs, axis=-1).astype(BF16)


def _sattn(wq, k_new, v_new, cache_k2, cache_v2, page_table, lamv, g_subln, n_heads, lam_init):
    n_seq, n_rows, aw = wq.shape
    n_new = k_new.shape[1]
    hd = aw // n_heads
    page = cache_k2.shape[1] // n_heads
    n_pages = page_table.shape[1]
    pstep = _pick(n_pages, (16, 8, 4, 2, 1))
    past = n_pages * page
    assert n_heads == TILE_SUBLANES and hd == 128

    def page_spec(p):
        return pl.BlockSpec((None, page * n_heads, hd), lambda b, j, pt: (pt[b, j * pstep + p], 0, 0))

    seq_spec = lambda rows: pl.BlockSpec((None, rows, aw), lambda b, j, pt: (b, 0, 0))
    return pl.pallas_call(
        functools.partial(_sattn_kernel, n_pages_step=pstep, page=page, past=past, n_new=n_new,
                          n_heads=n_heads, lam_init=lam_init),
        grid_spec=pltpu.PrefetchScalarGridSpec(
            num_scalar_prefetch=1,
            grid=(n_seq, n_pages // pstep),
            in_specs=[
                pl.BlockSpec(lamv.shape, lambda b, j, pt: (0, 0)),
                pl.BlockSpec((1, hd), lambda b, j, pt: (0, 0)),
                seq_spec(n_rows), seq_spec(n_new), seq_spec(n_new),
            ] + [page_spec(p) for p in range(pstep)] * 2,
            out_specs=seq_spec(n_new),
            scratch_shapes=[
                pltpu.VMEM((n_rows, 1), F32),
                pltpu.VMEM((n_rows, 1), F32),
                pltpu.VMEM((n_rows, aw), F32),
            ],
        ),
        out_shape=jax.ShapeDtypeStruct((n_seq, n_new, aw), BF16),
        compiler_params=_cparams(2),
        name="sattn",
    )(page_table, lamv, g_subln.reshape(1, hd), wq, k_new, v_new,
      *([cache_k2] * pstep), *([cache_v2] * pstep))


def _gelu_tanh(x):
    return x * (0.5 * (1.0 + jnp.tanh(math.sqrt(2.0 / math.pi) * (x + 0.044715 * (x * x * x)))))


def _softplus(x):
    return jnp.maximum(x, 0.0) + jnp.log1p(jnp.exp(-jnp.abs(x)))


def _rglru_kernel(u_ref, gt_ref, c0_ref, h0_ref, cw_ref, cb_ref, wai_ref, ba_ref, bi_ref, lam_ref, gn_ref,
                  rec_ref, ht_ref, cout_ref, ubuf, hcar, *, tl, n_blocks):
    t = pl.program_id(1)
    halo = CONV_WIDTH - 1
    base = 8

    @pl.when(t == 0)
    def _():
        ubuf[base - halo:base, :] = c0_ref[...]
        hcar[...] = h0_ref[...]

    ubuf[base:base + tl, :] = u_ref[...]
    cw = cw_ref[...]
    xc = cb_ref[...] + cw[0:1] * ubuf[base - halo:base - halo + tl, :]
    for jj in range(1, CONV_WIDTH):
        xc = xc + cw[jj:jj + 1] * ubuf[base - halo + jj:base - halo + jj + tl, :]
    tail = ubuf[base + tl - halo:base + tl, :]
    ubuf[base - halo:base, :] = tail
    cout_ref[...] = tail

    bw = xc.shape[1] // n_blocks
    za, zi = [], []
    for n in range(n_blocks):
        z = _dot(xc[:, n * bw:(n + 1) * bw].astype(BF16), wai_ref[n])
        za.append(z[:, :bw])
        zi.append(z[:, bw:])
    r = jax.nn.sigmoid(jnp.concatenate(za, axis=-1) + ba_ref[...])
    i = jax.nn.sigmoid(jnp.concatenate(zi, axis=-1) + bi_ref[...])
    log_a = -RG_C * r * _softplus(-lam_ref[...])
    a = jnp.exp(log_a)
    th = jnp.tanh(log_a)
    b = xc * i * jnp.sqrt(-2.0 * th / (1.0 - th))

    rowi = lax.broadcasted_iota(I32, a.shape, 0)
    sft = 1
    while sft < tl:
        keep = rowi >= sft
        a_prev = jnp.where(keep, pltpu.roll(a, sft, 0), 1.0)
        b_prev = jnp.where(keep, pltpu.roll(b, sft, 0), 0.0)
        b = a * b_prev + b
        a = a * a_prev
        sft *= 2
    hs = a * hcar[...] + b
    h_last = hs[tl - 1:tl, :]
    hcar[...] = h_last
    ht_ref[...] = h_last

    rec = hs * _gelu_tanh(gt_ref[...])
    rec_ref[...] = (_rms(rec, NORM_EPS) * gn_ref[...]).astype(BF16)


def _rglru(u2d, gate2d, conv0, h0, conv_w, conv_b, wai_bf, b_a, b_i, rg_lambda, g_rgnorm, n_batch, seq):
    r, w = u2d.shape
    tl = _pick(seq, (256, 128, 64, 32, 16, 8))
    nt = seq // tl
    n_blocks = wai_bf.shape[0]
    halo = CONV_WIDTH - 1
    row = lambda b, t: (b * nt + t, 0)
    vec = pl.BlockSpec((1, w), lambda b, t: (0, 0))
    rec, ht, cout = pl.pallas_call(
        functools.partial(_rglru_kernel, tl=tl, n_blocks=n_blocks),
        grid=(n_batch, nt),
        in_specs=[
            pl.BlockSpec((tl, w), row),
            pl.BlockSpec((tl, w), row),
            pl.BlockSpec((None, halo, w), lambda b, t: (b, 0, 0)),
            pl.BlockSpec((None, 1, w), lambda b, t: (b, 0, 0)),
            pl.BlockSpec((CONV_WIDTH, w), lambda b, t: (0, 0)),
            vec,
            pl.BlockSpec(wai_bf.shape, lambda b, t: (0, 0, 0)),
            vec, vec, vec, vec,
        ],
        out_specs=[
            pl.BlockSpec((tl, w), row),
            pl.BlockSpec((None, 1, w), lambda b, t: (b, 0, 0)),
            pl.BlockSpec((None, halo, w), lambda b, t: (b, 0, 0)),
        ],
        out_shape=[
            jax.ShapeDtypeStruct((r, w), BF16),
            jax.ShapeDtypeStruct((n_batch, 1, w), F32),
            jax.ShapeDtypeStruct((n_batch, halo, w), F32),
        ],
        scratch_shapes=[pltpu.VMEM((tl + 8, w), F32), pltpu.VMEM((1, w), F32)],
        compiler_params=_cparams(2),
        name="rglru",
    )(u2d, gate2d, conv0, h0.reshape(n_batch, 1, w), conv_w, conv_b.reshape(1, w), wai_bf,
      b_a.reshape(1, w), b_i.reshape(1, w), rg_lambda.reshape(1, w), g_rgnorm.reshape(1, w))
    return rec, ht.reshape(n_batch, w), cout


def _oproj_kernel(att_ref, rec_ref, x_ref, g1_ref, sh_ref, sc_ref, gn_ref, wo_ref, wrt_ref, *rest,
                  aliased, n_tiles):
    x1_ref, h2p_ref, st_ref = rest[2:] if aliased else rest
    aw = att_ref.shape[1]
    i = pl.program_id(0)

    @pl.when(i < n_tiles)
    def _():
        mix = _dot(att_ref[...], wo_ref[:aw, :]) + _dot(rec_ref[...], wo_ref[aw:, :])
        x1 = x_ref[...] + g1_ref[...] * mix
        x1_ref[...] = x1
        h2 = (_rms(x1, NORM_EPS) * gn_ref[...]) * (1.0 + sc_ref[...]) + sh_ref[...]
        _rows_to_tiles(h2p_ref, _pack_halves(h2))
        h_hi, h_lo = _split(h2)
        w_hi, w_lo = _split(wrt_ref[...])
        logits_t = _dot_nt(w_hi, h_hi) + (_dot_nt(w_hi, h_lo) + _dot_nt(w_lo, h_hi))
        st_ref[...] = jax.nn.sigmoid(logits_t)

    @pl.when(i >= n_tiles)
    def _():
        h2p_ref[...] = jnp.zeros(h2p_ref.shape, U32)
        st_ref[...] = jnp.zeros(st_ref.shape, F32)


def _oproj(att, rec, x2d, g1, shift, scale, g_norm2, wo_bf, wr_t, rows_per_batch, row_offset, total_rows,
           h2p_all=None, st_all=None):
    r, d = x2d.shape
    aw = att.shape[1]
    n_exp = wr_t.shape[0]
    tm = _pick(r, (256, 128))
    assert row_offset % tm == 0
    off = row_offset // tm
    lanes = d // 2 // TILE_SUBLANES
    assert lanes == 128
    aliased = h2p_all is not None
    n_tiles = r // tm
    n_fill = 0 if aliased else (total_rows - r) // tm
    assert aliased or (row_offset == 0 and (total_rows - r) % tm == 0)
    row = lambda i: (jnp.minimum(i, n_tiles - 1), 0)
    mspec = _mod_spec(rows_per_batch, tm, d, n_tiles)
    in_specs = [
        pl.BlockSpec((tm, aw), row),
        pl.BlockSpec((tm, d - aw), row),
        pl.BlockSpec((tm, d), row),
        mspec, mspec, mspec,
        pl.BlockSpec((1, d), lambda i: (0, 0)),
        pl.BlockSpec((d, d), lambda i: (0, 0)),
        pl.BlockSpec((n_exp, d), lambda i: (0, 0)),
    ]
    args = [att, rec, x2d, _mod_array(g1, rows_per_batch, tm), _mod_array(shift, rows_per_batch, tm),
            _mod_array(scale, rows_per_batch, tm), g_norm2.reshape(1, d), wo_bf, wr_t]
    io_alias = {}
    if aliased:
        in_specs += [pl.BlockSpec(memory_space=pl.ANY), pl.BlockSpec(memory_space=pl.ANY)]
        io_alias = {len(args): 1, len(args) + 1: 2}
        args += [h2p_all, st_all]
    return pl.pallas_call(
        functools.partial(_oproj_kernel, aliased=aliased, n_tiles=n_tiles),
        grid=(n_tiles + n_fill,),
        in_specs=in_specs,
        out_specs=[
            pl.BlockSpec((tm, d), row),
            pl.BlockSpec((tm * TILE_SUBLANES, lanes), lambda i: (i + off, 0)),
            pl.BlockSpec((n_exp, tm), lambda i: (0, i + off)),
        ],
        out_shape=[
            jax.ShapeDtypeStruct((r, d), F32),
            jax.ShapeDtypeStruct((total_rows * TILE_SUBLANES, lanes), U32),
            jax.ShapeDtypeStruct((n_exp, total_rows), F32),
        ],
        input_output_aliases=io_alias,
        compiler_params=_cparams(1),
        name="oproj",
    )(*args)


def _route_kernel(st_ref, rb_ref, idx_ref, wts_ref, rank_ref, cnt_ref, carry):
    i = pl.program_id(0)
    n_exp, tr = st_ref.shape
    gsz = n_exp // N_GROUPS

    @pl.when(i == 0)
    def _():
        carry[...] = jnp.zeros(carry.shape, F32)

    s = st_ref[...]
    biased = s + rb_ref[...]
    g = biased.reshape(N_GROUPS, gsz, tr)
    within = lax.broadcasted_iota(I32, g.shape, 1)
    m1 = jnp.max(g, axis=1, keepdims=True)
    first = jnp.min(jnp.where(g == m1, within, gsz), axis=1, keepdims=True)
    m2 = jnp.max(jnp.where(within == first, -jnp.inf, g), axis=1, keepdims=True)
    gscore = (m1 + m2).reshape(N_GROUPS, tr)

    gidx = lax.broadcasted_iota(I32, gscore.shape, 0)
    gsel = jnp.zeros(gscore.shape, F32)
    for _ in range(TOPK_GROUPS):
        mg = jnp.max(gscore, axis=0, keepdims=True)
        fg = jnp.min(jnp.where(gscore == mg, gidx, N_GROUPS), axis=0, keepdims=True)
        hit = gidx == fg
        gsel = jnp.where(hit, 1.0, gsel)
        gscore = jnp.where(hit, -jnp.inf, gscore)
    masked = jnp.where(gsel.reshape(N_GROUPS, 1, tr) > 0.5, g, -jnp.inf).reshape(n_exp, tr)

    eidx = lax.broadcasted_iota(I32, (n_exp, tr), 0)
    idxs, ws = [], []
    chosen = jnp.zeros((n_exp, tr), jnp.bool_)
    for _ in range(TOP_K):
        mv = jnp.max(masked, axis=0, keepdims=True)
        fe = jnp.min(jnp.where(masked == mv, eidx, n_exp), axis=0, keepdims=True)
        hit = eidx == fe
        idxs.append(fe)
        ws.append(jnp.sum(jnp.where(hit, s, 0.0), axis=0, keepdims=True))
        chosen = chosen | hit
        masked = jnp.where(hit, -jnp.inf, masked)
    idx = jnp.concatenate(idxs, axis=0)
    w = jnp.concatenate(ws, axis=0)
    idx_ref[...] = idx
    wts_ref[...] = w / jnp.sum(w, axis=0, keepdims=True) * ROUTED_SCALE

    cmat = jnp.where(chosen, 1.0, 0.0)
    before = lax.broadcasted_iota(I32, (tr, tr), 0) < lax.broadcasted_iota(I32, (tr, tr), 1)
    prior = _dot(cmat.astype(BF16), jnp.where(before, 1.0, 0.0).astype(BF16)) + carry[...]
    ranks = [jnp.sum(jnp.where(eidx == idxs[k], prior, 0.0), axis=0, keepdims=True) for k in range(TOP_K)]
    rank_ref[...] = jnp.concatenate(ranks, axis=0).astype(I32)
    carry[...] = carry[...] + jnp.sum(cmat, axis=1, keepdims=True)
    cnt_ref[...] = carry[...].astype(I32)


def _route(st_all, router_bias):
    n_exp, t_all = st_all.shape
    tr = _pick(t_all, (640, 512, 256, 128))
    col = lambda i: (0, i)
    o8 = lambda dt: jax.ShapeDtypeStruct((TOP_K, t_all), dt)
    return pl.pallas_call(
        _route_kernel,
        grid=(t_all // tr,),
        in_specs=[pl.BlockSpec((n_exp, tr), col), pl.BlockSpec((n_exp, 1), lambda i: (0, 0))],
        out_specs=[pl.BlockSpec((TOP_K, tr), col)] * 3 + [pl.BlockSpec((n_exp, 1), lambda i: (0, 0))],
        out_shape=[o8(I32), o8(F32), o8(I32), jax.ShapeDtypeStruct((n_exp, 1), I32)],
        scratch_shapes=[pltpu.VMEM((n_exp, 1), F32)],
        compiler_params=_cparams(1),
        name="route",
    )(st_all, router_bias.reshape(n_exp, 1))


def _pos_kernel(idx_ref, rank_ref, start_ref, pos_ref):
    n_exp = start_ref.shape[0]
    tr = idx_ref.shape[1]
    eidx = lax.broadcasted_iota(I32, (n_exp, tr), 0)
    start = start_ref[...]
    rows = [jnp.sum(jnp.where(eidx == idx_ref[k:k + 1, :], start, 0.0), axis=0, keepdims=True)
            for k in range(TOP_K)]
    pos_ref[...] = jnp.concatenate(rows, axis=0).astype(I32) + rank_ref[...]


def _pos(idx, rank, start_rows):
    n_exp = start_rows.shape[0]
    t_all = idx.shape[1]
    tr = _pick(t_all, (640, 512, 256, 128))
    col = pl.BlockSpec((TOP_K, tr), lambda i: (0, i))
    return pl.pallas_call(
        _pos_kernel,
        grid=(t_all // tr,),
        in_specs=[col, col, pl.BlockSpec((n_exp, 1), lambda i: (0, 0))],
        out_specs=col,
        out_shape=jax.ShapeDtypeStruct((TOP_K, t_all), I32),
        compiler_params=_cparams(1),
        name="pos",
    )(idx, rank, start_rows.astype(F32).reshape(n_exp, 1))


def _tile_pos(pos, tile):
    k, t = pos.shape
    return pos.reshape(k, t // tile, tile).transpose(1, 0, 2)


def _sc_worker_chunks(n_chunks, fn):
    n_workers = V7X_SC_CORES * V7X_SC_SUBCORES
    worker = lax.axis_index("core") * V7X_SC_SUBCORES + lax.axis_index("subcore")

    @pl.loop(0, -(-n_chunks // n_workers))
    def _(it):
        chunk = it * n_workers + worker

        @pl.when(chunk < n_chunks)
        def _():
            fn(chunk)


def _sc_mesh():
    return plsc.VectorSubcoreMesh(core_axis_name="core", subcore_axis_name="subcore",
                                  num_cores=V7X_SC_CORES, num_subcores=V7X_SC_SUBCORES)


def _sc_scatter_rows(x3, idx3, n_rows):
    n_chunks, n_k, width = idx3.shape
    row = x3.shape[1:]

    @pl.kernel(out_type=jax.ShapeDtypeStruct((n_rows,) + row, x3.dtype), mesh=_sc_mesh(),
               scratch_types=[pltpu.VMEM((width,) + row, x3.dtype), pltpu.VMEM((n_k, width), I32),
                              pltpu.SemaphoreType.DMA],
               name="dispatch_sc")
    def scatter(x_hbm, i_hbm, o_hbm, xbuf, ibuf, sem):
        def one(chunk):
            pltpu.sync_copy(x_hbm.at[pl.ds(chunk * width, width)], xbuf)
            pltpu.sync_copy(i_hbm.at[chunk], ibuf)
            copies = [pltpu.make_async_copy(xbuf, o_hbm.at[ibuf.at[k]], sem) for k in range(n_k)]
            for cp in copies:
                cp.start()
            for cp in copies:
                cp.wait()

        _sc_worker_chunks(n_chunks, one)

    return scatter(x3, idx3)


def _sc_gather_rows(src3, idx2):
    n_chunks, width = idx2.shape
    row = src3.shape[1:]

    @pl.kernel(out_type=jax.ShapeDtypeStruct((n_chunks * width,) + row, src3.dtype), mesh=_sc_mesh(),
               scratch_types=[pltpu.VMEM((width,) + row, src3.dtype), pltpu.VMEM((width,), I32),
                              pltpu.SemaphoreType.DMA],
               name="combine_sc")
    def gather(s_hbm, i_hbm, o_hbm, buf, ibuf, sem):
        def one(chunk):
            pltpu.sync_copy(i_hbm.at[chunk], ibuf)
            pltpu.async_copy(s_hbm.at[ibuf], buf, sem).wait()
            pltpu.sync_copy(buf, o_hbm.at[pl.ds(chunk * width, width)])

        _sc_worker_chunks(n_chunks, one)

    return gather(src3, idx2)


def _swiglu_packed(xp, wg, wu, wd):
    xa, xb = _unpack_halves(xp)
    half = xp.shape[1]
    g = _dot(xa, wg[:half, :]) + _dot(xb, wg[half:, :])
    u = _dot(xa, wu[:half, :]) + _dot(xb, wu[half:, :])
    return _dot((_silu(g) * u).astype(BF16), wd[...])


def _experts_kernel(te_ref, tnv_ref, tord_ref, tnext_ref, tnext2_ref, tlo_ref, thi_ref,
                    xs_ref, wg_hbm, wu_hbm, wd_hbm, ys_ref, wg_f, wu_f, wd_f, wg_b, wu_b, wd_b, sems):
    i = pl.program_id(0)
    nv = tnv_ref[i]
    expert = te_ref[i]
    new_expert = jnp.logical_or(i == 0, expert != te_ref[jnp.maximum(i - 1, 0)])
    slot = tord_ref[i] % EXPERT_WEIGHT_SLOTS
    tm = ys_ref.shape[0] // TILE_SUBLANES

    def weight_copies(e, sl):
        copies = []
        for n, (hbm, buf) in enumerate(((wg_hbm, wg_f), (wu_hbm, wu_f), (wd_hbm, wd_f))):
            rows = hbm.shape[1] // EXPERT_WEIGHT_DMA_CHUNKS
            for c in range(EXPERT_WEIGHT_DMA_CHUNKS):
                span = pl.ds(c * rows, rows)
                copies.append(pltpu.make_async_copy(hbm.at[e, span], buf.at[sl, span], sems.at[n, sl]))
        return copies

    @pl.when(jnp.logical_and(nv > 0, new_expert))
    def _():
        @pl.when(i == 0)
        def _():
            for cp in weight_copies(expert, slot):
                cp.start()
            nxt = tnext_ref[i]

            @pl.when(nxt >= 0)
            def _():
                for cp in weight_copies(nxt, (slot + 1) % EXPERT_WEIGHT_SLOTS):
                    cp.start(priority=WEIGHT_PREFETCH_DMA_PRIORITY)

        for cp in weight_copies(expert, slot):
            cp.wait()

    nxt2 = tnext2_ref[i]

    @pl.when(jnp.logical_and(nv > 0, nxt2 >= 0))
    def _():
        lo, hi = tlo_ref[i], thi_ref[i]
        for c, cp in enumerate(weight_copies(nxt2, (slot + 2) % EXPERT_WEIGHT_SLOTS)):
            @pl.when(jnp.logical_and(lo <= c, c < hi))
            def _():
                cp.start(priority=WEIGHT_PREFETCH_DMA_PRIORITY)

    @pl.when(jnp.logical_and(nv > 0, new_expert))
    def _():
        wg_b[...] = wg_f[slot].astype(BF16)
        wu_b[...] = wu_f[slot].astype(BF16)
        wd_b[...] = wd_f[slot].astype(BF16)

    @pl.when(nv > 0)
    def _():
        xp = _rows_from_tiles(xs_ref, tm)
        rowi = lax.broadcasted_iota(I32, xp.shape, 0)
        xp = jnp.where(rowi < nv, xp, jnp.uint32(0))
        _rows_to_tiles(ys_ref, _pack_halves(_swiglu_packed(xp, wg_b, wu_b, wd_b)))

    @pl.when(nv == 0)
    def _():
        ys_ref[...] = jnp.zeros(ys_ref.shape, U32)


def _experts(xs, tile_tables, w_gate, w_up, w_down):
    rows, lanes = xs.shape
    tm = EXPERT_TILE_ROWS
    n_exp, d, ff = w_gate.shape
    blk = pl.BlockSpec((tm * TILE_SUBLANES, lanes), lambda i, *_: (i, 0))
    hbm = pl.BlockSpec(memory_space=pl.ANY)
    return pl.pallas_call(
        _experts_kernel,
        grid_spec=pltpu.PrefetchScalarGridSpec(
            num_scalar_prefetch=len(tile_tables),
            grid=(rows // (tm * TILE_SUBLANES),),
            in_specs=[blk, hbm, hbm, hbm],
            out_specs=blk,
            scratch_shapes=[
                pltpu.VMEM((EXPERT_WEIGHT_SLOTS, d, ff), F32), pltpu.VMEM((EXPERT_WEIGHT_SLOTS, d, ff), F32),
                pltpu.VMEM((EXPERT_WEIGHT_SLOTS, ff, d), F32),
                pltpu.VMEM((d, ff), BF16), pltpu.VMEM((d, ff), BF16), pltpu.VMEM((ff, d), BF16),
                pltpu.SemaphoreType.DMA((3, EXPERT_WEIGHT_SLOTS)),
            ],
        ),
        out_shape=jax.ShapeDtypeStruct((rows, lanes), U32),
        compiler_params=_cparams(1),
        name="experts",
    )(*tile_tables, xs, w_gate, w_up, w_down)


def _combine_kernel(w_ref, yg_ref, h2p_ref, x1_ref, g2_ref, gf_ref, wsg_ref, wsu_ref, wsd_ref, o_ref, *, tc):
    shared = _swiglu_packed(_rows_from_tiles(h2p_ref, tc), wsg_ref, wsu_ref, wsd_ref)
    w = w_ref[...]
    lo, hi = None, None
    for k in range(TOP_K):
        rows = tc * TILE_SUBLANES
        yk = _rows_from_tiles(yg_ref.at[pl.ds(k * rows, rows)], tc)
        wk = w[:, k:k + 1]
        yl = pltpu.unpack_elementwise(yk, index=0, packed_dtype=BF16, unpacked_dtype=F32) * wk
        yh = pltpu.unpack_elementwise(yk, index=1, packed_dtype=BF16, unpacked_dtype=F32) * wk
        lo = yl if lo is None else lo + yl
        hi = yh if hi is None else hi + yh
    routed = jnp.concatenate([lo, hi], axis=-1)
    x2 = x1_ref[...] + g2_ref[...] * (routed + shared)
    o_ref[...] = _rms(x2, NORM_EPS) * gf_ref[...]


def _combine(yg, wts_t, h2p_all, x1, g2, g_final, wsg_bf, wsu_bf, wsd_bf, rows_per_batch, row_offset):
    r, d = x1.shape
    lanes = h2p_all.shape[1]
    tc = COMBINE_TILE
    assert r % tc == 0 and row_offset % tc == 0
    off = row_offset // tc
    ff = wsg_bf.shape[1]
    const = lambda shp: pl.BlockSpec(shp, lambda i: (0,) * len(shp))
    return pl.pallas_call(
        functools.partial(_combine_kernel, tc=tc),
        grid=(r // tc,),
        in_specs=[
            pl.BlockSpec((tc, TOP_K), lambda i: (i, 0)),
            pl.BlockSpec((TOP_K * tc * TILE_SUBLANES, lanes), lambda i: (i + off, 0)),
            pl.BlockSpec((tc * TILE_SUBLANES, lanes), lambda i: (i + off, 0)),
            pl.BlockSpec((tc, d), lambda i: (i, 0)),
            _mod_spec(rows_per_batch, tc, d),
            const((1, d)), const((d, ff)), const((d, ff)), const((ff, d)),
        ],
        out_specs=pl.BlockSpec((tc, d), lambda i: (i, 0)),
        out_shape=jax.ShapeDtypeStruct((r, d), F32),
        compiler_params=_cparams(1),
        name="combine",
    )(wts_t, yg, h2p_all, x1, _mod_array(g2, rows_per_batch, tc), g_final.reshape(1, d),
      wsg_bf, wsu_bf, wsd_bf)


def kernel(x_prompt, x_sample, cache_k, cache_v, state_h, state_conv, page_table, c_prompt, c_sample,
           w_ada, b_ada, g_norm1, w_in, lambda_q1, lambda_k1, lambda_q2, lambda_k2, g_subln,
           conv_w, conv_b, w_rg_a, b_rg_a, w_rg_i, b_rg_i, rg_lambda, g_rgnorm, w_o, g_norm2,
           w_router, router_bias, w_e_gate, w_e_up, w_e_down, w_s_gate, w_s_up, w_s_down, g_final):
    depth = w_ada.shape[0]
    assert depth == 1, "single-layer step"
    bp, seq, d = x_prompt.shape
    bs, n_new, _ = x_sample.shape
    n_heads = cache_k.shape[3]
    k_row = cache_k.shape[4]
    v_head = cache_v.shape[4]
    aw = n_heads * v_head
    rw = d - aw
    assert k_row == v_head and w_in.shape[2] == 3 * aw + 2 * rw and aw == rw
    qk_half = k_row // 2
    n_exp = w_router.shape[2]
    lam_init = 0.8 - 0.6 * math.exp(-0.3 * 0)
    tp, ts = bp * seq, bs * n_new
    t_all = tp + ts

    w_in_bf = w_in[0].astype(BF16)
    wo_bf = w_o[0].astype(BF16)
    wr_t = w_router[0].T
    wai_bf = jnp.concatenate([w_rg_a[0], w_rg_i[0]], axis=-1).astype(BF16)
    wsg_bf, wsu_bf, wsd_bf = w_s_gate[0].astype(BF16), w_s_up[0].astype(BF16), w_s_down[0].astype(BF16)
    lamv = jnp.stack([lambda_q1[0], lambda_k1[0], lambda_q2[0], lambda_k2[0]])
    slopes = jnp.exp2(-8.0 * jnp.arange(1, n_heads + 1, dtype=F32) / n_heads)

    mod = _ada(jnp.concatenate([c_prompt, c_sample], axis=0), w_ada[0], b_ada[0])
    mod_p = [mod[:bp, i * d:(i + 1) * d] for i in range(6)]
    mod_s = [mod[bp:, i * d:(i + 1) * d] for i in range(6)]

    xp2, xs2 = x_prompt.reshape(tp, d), x_sample.reshape(ts, d)
    qscale = qk_half ** -0.5 * LOG2E
    qp, kp, vp, up, gp, kpb, vpb = _inproj(xp2, mod_p[0], mod_p[1], g_norm1[0], w_in_bf, seq, qscale)
    qs, ks, vs, us, gs, ksb, vsb = _inproj(xs2, mod_s[0], mod_s[1], g_norm1[0], w_in_bf, n_new, qscale)

    att_p = _pattn(qp, kpb, vpb, slopes, lamv, g_subln[0], bp, seq, n_heads, lam_init)

    n_grp = 2 * n_heads
    own = (jnp.arange(aw)[None, :] // qk_half) == (jnp.arange(n_grp * n_new)[:, None] // n_new)
    wq = jnp.where(own[None], jnp.tile(qs.reshape(bs, n_new, aw), (1, n_grp, 1)), jnp.zeros((), BF16))
    n_pool, page = cache_k.shape[1], cache_k.shape[2]
    att_s = _sattn(wq, ksb.reshape(bs, n_new, aw), vsb.reshape(bs, n_new, aw),
                   cache_k.reshape(n_pool, page * n_heads, k_row), cache_v.reshape(n_pool, page * n_heads, v_head),
                   page_table, lamv, g_subln[0], n_heads, lam_init).reshape(ts, aw)

    rg_args = (conv_w[0], conv_b[0], wai_bf, b_rg_a[0], b_rg_i[0], rg_lambda[0], g_rgnorm[0])
    rec_p, h_p, conv_p = _rglru(up, gp, jnp.zeros((bp, CONV_WIDTH - 1, rw), F32), jnp.zeros((bp, rw), F32),
                                *rg_args, bp, seq)
    rec_s, h_s, conv_s = _rglru(us, gs, state_conv[0], state_h[0], *rg_args, bs, n_new)

    x1p, h2p_all, st_all = _oproj(att_p, rec_p, xp2, mod_p[2], mod_p[3], mod_p[4], g_norm2[0], wo_bf, wr_t,
                                  seq, 0, t_all)
    x1s, h2p_all, st_all = _oproj(att_s, rec_s, xs2, mod_s[2], mod_s[3], mod_s[4], g_norm2[0], wo_bf, wr_t,
                                  n_new, tp, t_all, h2p_all, st_all)

    idx, wts, rank, counts = _route(st_all, router_bias[0])

    tm = EXPERT_TILE_ROWS
    counts = counts.reshape(n_exp)
    ptiles = (counts + tm - 1) // tm
    pend = jnp.cumsum(ptiles)
    pstart = pend - ptiles
    n_tiles = (t_all * TOP_K) // tm + n_exp
    tile_ids = jnp.arange(n_tiles, dtype=I32)
    tile_e = jnp.minimum(jnp.sum(pend[None, :] <= tile_ids[:, None], axis=1), n_exp - 1).astype(I32)
    tile_nv = jnp.where(tile_ids < pend[-1],
                        jnp.clip(counts[tile_e] - (tile_ids - pstart[tile_e]) * tm, 0, tm), 0).astype(I32)
    tile_ord = (jnp.cumsum(ptiles > 0) - 1)[tile_e].astype(I32)
    next_tile = pend[tile_e]
    e_next = tile_e[jnp.minimum(next_tile, n_tiles - 1)]
    tile_next = jnp.where(next_tile < pend[-1], e_next, -1).astype(I32)
    next_tile2 = pend[e_next]
    tile_next2 = jnp.where(jnp.logical_and(next_tile < pend[-1], next_tile2 < pend[-1]),
                           tile_e[jnp.minimum(next_tile2, n_tiles - 1)], -1).astype(I32)
    n_copies = 3 * EXPERT_WEIGHT_DMA_CHUNKS
    tile_j, tile_n = tile_ids - pstart[tile_e], jnp.maximum(ptiles[tile_e] - 1, 1)
    tile_lo = ((n_copies * tile_j + tile_n - 1) // tile_n).astype(I32)
    tile_hi = ((n_copies * (tile_j + 1) + tile_n - 1) // tile_n).astype(I32)
    pos = _pos(idx, rank, pstart * tm)

    lanes = h2p_all.shape[1]
    tile3 = lambda a2: a2.reshape(-1, TILE_SUBLANES, lanes)
    xs_rows = _sc_scatter_rows(tile3(h2p_all), _tile_pos(pos, SC_CHUNK_ROWS), n_tiles * tm)
    ys = _experts(xs_rows.reshape(-1, lanes), (tile_e, tile_nv, tile_ord, tile_next, tile_next2, tile_lo, tile_hi),
                  w_e_gate[0], w_e_up[0], w_e_down[0])
    yg = _sc_gather_rows(tile3(ys), _tile_pos(pos, COMBINE_TILE).reshape(-1, SC_CHUNK_ROWS)).reshape(-1, lanes)

    wts_t = wts.T
    y_p = _combine(yg, wts_t[:tp], h2p_all, x1p, mod_p[5], g_final, wsg_bf, wsu_bf, wsd_bf, seq, 0)
    y_s = _combine(yg, wts_t[tp:], h2p_all, x1s, mod_s[5], g_final, wsg_bf, wsu_bf, wsd_bf, n_new, tp)

    return (y_p.reshape(bp, seq, d), y_s.reshape(bs, n_new, d),
            kp.reshape(1, bp, seq, n_heads, k_row), vp.reshape(1, bp, seq, n_heads, v_head),
            h_p.reshape(1, bp, rw), conv_p.reshape(1, bp, CONV_WIDTH - 1, rw),
            ks.reshape(1, bs, n_new, n_heads, k_row), vs.reshape(1, bs, n_new, n_heads, v_head),
            h_s.reshape(1, bs, rw), conv_s.reshape(1, bs, CONV_WIDTH - 1, rw))
```

```python
import functools
import math

import jax
import jax.numpy as jnp
import numpy as np
from jax import lax
from jax.experimental import pallas as pl
from jax.experimental.pallas import tpu as pltpu
from jax.experimental.pallas import tpu_sc as plsc

F32 = jnp.float32
BF16 = jnp.bfloat16
I32 = jnp.int32
U32 = jnp.uint32

NORM_EPS = 1e-6
SUBLN_EPS = 1e-5
NEG = -1e30
RG_C = 8.0
ROUTED_SCALE = 2.5
N_GROUPS = 8
TOPK_GROUPS = 4
TOP_K = 8
CONV_WIDTH = 4

V7X_VMEM_LIMIT_BYTES = 56 * 1024 * 1024
EXPERT_TILE_ROWS = 256
EXPERT_WEIGHT_DMA_CHUNKS = 4
EXPERT_WEIGHT_SLOTS = 3
WEIGHT_PREFETCH_DMA_PRIORITY = 1
V7X_SC_CORES = 2
V7X_SC_SUBCORES = 16
COMBINE_TILE = 256
SC_CHUNK_ROWS = 64


def _cparams(n_axes):
    return pltpu.CompilerParams(
        dimension_semantics=("arbitrary",) * n_axes, vmem_limit_bytes=V7X_VMEM_LIMIT_BYTES
    )


def _pick(n, candidates):
    for c in candidates:
        if n % c == 0:
            return c
    return n


def _dot(a, b):
    return jnp.dot(a, b, preferred_element_type=F32)


def _dot_nt(a, b):
    return lax.dot_general(a, b, (((1,), (1,)), ((), ())), preferred_element_type=F32)


def _split(x):
    hi = x.astype(BF16)
    lo = (x - hi.astype(F32)).astype(BF16)
    return hi, lo


def _rms(x, eps):
    return x * lax.rsqrt(jnp.mean(x * x, axis=-1, keepdims=True) + eps)


def _silu(x):
    return x * jax.nn.sigmoid(x)


def _bf16_terms(x, n):
    terms = []
    for _ in range(n):
        bits = np.float32(x).view(np.uint32)
        bits = (bits + np.uint32(0x7FFF) + ((bits >> np.uint32(16)) & np.uint32(1))) & np.uint32(0xFFFF0000)
        t = float(bits.view(np.float32))
        terms.append(t)
        x -= t
    return tuple(terms)


LOG2E = math.log2(math.e)
LOG2E_BF16_TERMS = _bf16_terms(LOG2E, 3)
TILE_SUBLANES = 8


def _rows_from_tiles(ref, n_rows):
    return jnp.concatenate(
        [ref[pl.ds(sub, n_rows, stride=TILE_SUBLANES), :] for sub in range(TILE_SUBLANES)], axis=-1)


def _rows_to_tiles(ref, x):
    n_rows, width = x.shape
    lanes = width // TILE_SUBLANES
    for sub in range(TILE_SUBLANES):
        ref[pl.ds(sub, n_rows, stride=TILE_SUBLANES), :] = x[:, sub * lanes:(sub + 1) * lanes]


def _unpack_halves(xp):
    lo = pltpu.unpack_elementwise(xp, index=0, packed_dtype=BF16, unpacked_dtype=F32)
    hi = pltpu.unpack_elementwise(xp, index=1, packed_dtype=BF16, unpacked_dtype=F32)
    return lo.astype(BF16), hi.astype(BF16)


def _pack_halves(x):
    n = x.shape[-1] // 2
    return pltpu.pack_elementwise([x[:, :n], x[:, n:]], packed_dtype=BF16)


def _ada_kernel(c_ref, w_ref, b_ref, o_ref):
    a_hi, a_lo = _split(_silu(c_ref[...]))
    w_hi, w_lo = _split(w_ref[...])
    o_ref[...] = _dot(a_hi, w_hi) + (_dot(a_hi, w_lo) + _dot(a_lo, w_hi)) + b_ref[...]


def _ada(c, w, b):
    n, d = c.shape
    d_out = w.shape[1]
    tn = _pick(d_out, (512, 256, 128))
    return pl.pallas_call(
        _ada_kernel,
        grid=(d_out // tn,),
        in_specs=[
            pl.BlockSpec((n, d), lambda j: (0, 0)),
            pl.BlockSpec((d, tn), lambda j: (0, j)),
            pl.BlockSpec((1, tn), lambda j: (0, j)),
        ],
        out_specs=pl.BlockSpec((n, tn), lambda j: (0, j)),
        out_shape=jax.ShapeDtypeStruct((n, d_out), F32),
        compiler_params=_cparams(1),
        name="ada",
    )(c, w, b.reshape(1, d_out))


def _mod_spec(rows_per_batch, tm, d, n_tiles=None):
    clamp = (lambda i: i) if n_tiles is None else (lambda i: jnp.minimum(i, n_tiles - 1))
    if rows_per_batch % tm == 0:
        per = rows_per_batch // tm
        return pl.BlockSpec((None, 1, d), lambda i, *_: (clamp(i) // per, 0, 0))
    return pl.BlockSpec((None, tm, d), lambda i, *_: (clamp(i), 0, 0))


def _mod_array(m, rows_per_batch, tm):
    nb, d = m.shape
    if rows_per_batch % tm == 0:
        return m.reshape(nb, 1, d)
    assert tm % rows_per_batch == 0
    return jnp.repeat(m, rows_per_batch, axis=0).reshape(nb * rows_per_batch // tm, tm, d)


def _inproj_kernel(x_ref, sh_ref, sc_ref, g_ref, w_ref,
                   q_ref, k_ref, v_ref, u_ref, gt_ref, kb_ref, vb_ref, h_scr, *, qscale):
    j = pl.program_id(1)

    @pl.when(j == 0)
    def _():
        y = _rms(x_ref[...], NORM_EPS) * g_ref[...]
        h_scr[...] = (y * (1.0 + sc_ref[...]) + sh_ref[...]).astype(BF16)

    z = _dot(h_scr[...], w_ref[...])

    @pl.when(j == 0)
    def _():
        q_ref[...] = (z * qscale).astype(BF16)

    @pl.when(j == 1)
    def _():
        k_ref[...] = z
        kb_ref[...] = z.astype(BF16)

    @pl.when(j == 2)
    def _():
        v_ref[...] = z
        vb_ref[...] = z.astype(BF16)

    @pl.when(j == 3)
    def _():
        u_ref[...] = z

    @pl.when(j == 4)
    def _():
        gt_ref[...] = z


def _inproj(x2d, shift, scale, g, w_bf, rows_per_batch, qscale):
    r, d = x2d.shape
    wd = w_bf.shape[1] // 5
    tm = _pick(r, (512, 256, 128, 64, 32, 16, 8))
    row = lambda i, j: (i, 0)
    f32o = jax.ShapeDtypeStruct((r, wd), F32)
    bfo = jax.ShapeDtypeStruct((r, wd), BF16)
    return pl.pallas_call(
        functools.partial(_inproj_kernel, qscale=qscale),
        grid=(r // tm, 5),
        in_specs=[
            pl.BlockSpec((tm, d), row),
            _mod_spec(rows_per_batch, tm, d),
            _mod_spec(rows_per_batch, tm, d),
            pl.BlockSpec((1, d), lambda i, j: (0, 0)),
            pl.BlockSpec((d, wd), lambda i, j: (0, j)),
        ],
        out_specs=[pl.BlockSpec((tm, wd), row)] * 7,
        out_shape=[bfo, f32o, f32o, f32o, f32o, bfo, bfo],
        scratch_shapes=[pltpu.VMEM((tm, d), BF16)],
        compiler_params=_cparams(2),
        name="inproj",
    )(x2d, _mod_array(shift, rows_per_batch, tm), _mod_array(scale, rows_per_batch, tm), g.reshape(1, d), w_bf)


def _lam(lamv_ref, lam_init):
    lv = lamv_ref[...]
    s1 = jnp.sum(lv[0:1] * lv[1:2], axis=-1, keepdims=True)
    s2 = jnp.sum(lv[2:3] * lv[3:4], axis=-1, keepdims=True)
    return jnp.exp(s1) - jnp.exp(s2) + lam_init


def _online_update(s, v, m_scr, l_scr, acc_scr, shift=None):
    m_old = m_scr[...]
    s_max = jnp.max(s, axis=-1, keepdims=True)
    m_new = jnp.maximum(m_old, s_max if shift is None else s_max + shift)
    alpha = jnp.exp2(m_old - m_new)
    p = jnp.exp2(s - (m_new if shift is None else m_new - shift))
    l_scr[...] = alpha * l_scr[...] + jnp.sum(p, axis=-1, keepdims=True)
    acc_scr[...] = alpha * acc_scr[...] + _dot(p.astype(BF16), v)
    m_scr[...] = m_new


def _pattn_kernel(slopes_ref, lamv_ref, gs_ref, q_ref, k_ref, v_ref, o_ref, q_scr, ka_scr, *state,
                  tq, rc, lam_init):
    n_chunks = 2 * tq // rc
    m_scrs, l_scrs, acc_scrs = state[:n_chunks], state[n_chunks:2 * n_chunks], state[2 * n_chunks:]
    h = pl.program_id(1)
    qi = pl.program_id(2)
    slope = slopes_ref[h]
    hd = q_ref.shape[1]
    half = hd // 2
    n_terms = len(LOG2E_BF16_TERMS)

    q = q_ref[...]
    lane = lax.broadcasted_iota(I32, q.shape, 1)
    zero = jnp.zeros_like(q)
    qa = jnp.zeros(q.shape, F32)
    for n, term in enumerate(LOG2E_BF16_TERMS):
        qa = jnp.where(jnp.logical_or(lane == n, lane == n + n_terms), term, qa)
    qa = qa.astype(BF16)
    q_scr[0:tq, 0:hd] = jnp.where(lane < half, q, zero)
    q_scr[tq:2 * tq, 0:hd] = jnp.where(lane >= half, q, zero)
    q_scr[0:tq, hd:2 * hd] = qa
    q_scr[tq:2 * tq, hd:2 * hd] = qa
    c = lax.broadcasted_iota(I32, (tq, hd), 0)
    c_lo = c % 256
    ka = jnp.where(lane < n_terms, c_lo.astype(F32) * slope,
                   jnp.where(lane < 2 * n_terms, (c - c_lo).astype(F32) * slope, 0.0))
    ka_scr[...] = ka.astype(BF16)

    for m_scr, l_scr, acc_scr in zip(m_scrs, l_scrs, acc_scrs):
        m_scr[...] = jnp.full(m_scr.shape, NEG, F32)
        l_scr[...] = jnp.zeros(l_scr.shape, F32)
        acc_scr[...] = jnp.zeros(acc_scr.shape, F32)

    def step(j, masked):
        start = pl.multiple_of(j * tq, tq)
        kaug = jnp.concatenate([k_ref[pl.ds(start, tq), :], ka_scr[...]], axis=-1)
        v = v_ref[pl.ds(start, tq), :]
        for ci in range(n_chunks):
            r0 = ci * rc
            q0 = r0 % tq
            ncol = q0 + rc if masked else tq
            rowpos = q0 + lax.broadcasted_iota(I32, (rc, 1), 0)
            s = _dot_nt(q_scr[r0:r0 + rc, :], kaug[:ncol])
            if masked:
                local = lax.broadcasted_iota(I32, (rc, rc), 1) <= lax.broadcasted_iota(I32, (rc, rc), 0)
                diag = jnp.where(local, s[:, q0:], NEG)
                s = diag if q0 == 0 else jnp.concatenate([s[:, :q0], diag], axis=1)
            shift = ((j - qi) * tq - rowpos).astype(F32) * (slope * LOG2E)
            _online_update(s, v[:ncol], m_scrs[ci], l_scrs[ci], acc_scrs[ci], shift=shift)

    def body(j, carry):
        step(j, False)
        return carry

    lax.fori_loop(0, qi, body, 0)
    step(qi, True)

    lam = _lam(lamv_ref, lam_init)
    o = jnp.concatenate([acc[...] / l[...] for acc, l in zip(acc_scrs, l_scrs)], axis=0)
    att = o[:tq] - lam * o[tq:]
    att = _rms(att, SUBLN_EPS) * gs_ref[...] * (1.0 - lam_init)
    o_ref[...] = att.astype(BF16)


def _pattn(qb, kb, vb, slopes, lamv, g_subln, n_batch, seq, n_heads, lam_init):
    r, aw = qb.shape
    hd = aw // n_heads
    tq = _pick(seq, (2048, 1024, 512, 256, 128))
    rc = min(tq, 256)
    n_chunks = 2 * tq // rc
    nq = seq // tq
    return pl.pallas_call(
        functools.partial(_pattn_kernel, tq=tq, rc=rc, lam_init=lam_init),
        grid_spec=pltpu.PrefetchScalarGridSpec(
            num_scalar_prefetch=1,
            grid=(n_batch, n_heads, nq),
            in_specs=[
                pl.BlockSpec(lamv.shape, lambda b, h, i, *_: (0, 0)),
                pl.BlockSpec((1, hd), lambda b, h, i, *_: (0, 0)),
                pl.BlockSpec((tq, hd), lambda b, h, i, *_: (b * nq + i, h)),
                pl.BlockSpec((seq, hd), lambda b, h, i, *_: (b, h)),
                pl.BlockSpec((seq, hd), lambda b, h, i, *_: (b, h)),
            ],
            out_specs=pl.BlockSpec((tq, hd), lambda b, h, i, *_: (b * nq + i, h)),
            scratch_shapes=[pltpu.VMEM((2 * tq, 2 * hd), BF16), pltpu.VMEM((tq, hd), BF16)]
            + [pltpu.VMEM((rc, 1), F32)] * (2 * n_chunks) + [pltpu.VMEM((rc, hd), F32)] * n_chunks,
        ),
        out_shape=jax.ShapeDtypeStruct((r, aw), BF16),
        compiler_params=_cparams(3),
        name="pattn",
    )(slopes, lamv, g_subln.reshape(1, hd), qb, kb, vb)


def _sattn_kernel(pt_ref, lamv_ref, gs_ref, wq_ref, kn_ref, vn_ref, *rest,
                  n_pages_step, page, past, n_new, n_heads, lam_init):
    k_refs = rest[:n_pages_step]
    v_refs = rest[n_pages_step:2 * n_pages_step]
    o_ref = rest[2 * n_pages_step]
    m_scr, l_scr, acc_scr = rest[2 * n_pages_step + 1:]
    j = pl.program_id(1)
    n_rows = wq_ref.shape[0]
    hd = wq_ref.shape[1] // n_heads
    tk = n_pages_step * page

    r = lax.broadcasted_iota(I32, (n_rows, 1), 0)
    head = r // (2 * n_new)
    qi = r % n_new
    slope = jnp.exp2(-(head + 1).astype(F32)) * LOG2E

    @pl.when(j == 0)
    def _():
        m_scr[...] = jnp.full(m_scr.shape, NEG, F32)
        l_scr[...] = jnp.zeros(l_scr.shape, F32)
        acc_scr[...] = jnp.zeros(acc_scr.shape, F32)

    wq = wq_ref[...]
    kc = jnp.concatenate([_rows_from_tiles(kr, page).astype(BF16) for kr in k_refs], axis=0)
    vc = jnp.concatenate([_rows_from_tiles(vr, page).astype(BF16) for vr in v_refs], axis=0)
    t = j * tk + lax.broadcasted_iota(I32, (1, tk), 1)
    s = _dot_nt(wq, kc) - slope * (past + qi - t).astype(F32)
    _online_update(s, vc, m_scr, l_scr, acc_scr)

    @pl.when(j == pl.num_programs(1) - 1)
    def _():
        pad = jnp.zeros((page - n_new, kn_ref.shape[1]), BF16)
        kn = jnp.concatenate([kn_ref[...].astype(BF16), pad], axis=0)
        vn = jnp.concatenate([vn_ref[...].astype(BF16), pad], axis=0)
        tj = lax.broadcasted_iota(I32, (1, page), 1)
        sn = _dot_nt(wq, kn) - slope * (qi - tj).astype(F32)
        sn = jnp.where(tj <= qi, sn, NEG)
        _online_update(sn, vn, m_scr, l_scr, acc_scr)

        lam = _lam(lamv_ref, lam_init)
        o = acc_scr[...] / l_scr[...]
        outs = []
        for h in range(n_heads):
            blk = o[h * 2 * n_new:(h + 1) * 2 * n_new, h * hd:(h + 1) * hd]
            att = blk[:n_new] - lam * blk[n_new:]
            outs.append(_rms(att, SUBLN_EPS) * gs_ref[...] * (1.0 - lam_init))
        o_ref[...] = jnp.concatenate(outs, axis=-1).astype(BF16)


def _sattn(wq, k_new, v_new, cache_k2, cache_v2, page_table, lamv, g_subln, n_heads, lam_init):
    n_seq, n_rows, aw = wq.shape
    n_new = k_new.shape[1]
    hd = aw // n_heads
    page = cache_k2.shape[1] // n_heads
    n_pages = page_table.shape[1]
    pstep = _pick(n_pages, (16, 8, 4, 2, 1))
    past = n_pages * page
    assert n_heads == TILE_SUBLANES and hd == 128

    def page_spec(p):
        return pl.BlockSpec((None, page * n_heads, hd), lambda b, j, pt: (pt[b, j * pstep + p], 0, 0))

    seq_spec = lambda rows: pl.BlockSpec((None, rows, aw), lambda b, j, pt: (b, 0, 0))
    return pl.pallas_call(
        functools.partial(_sattn_kernel, n_pages_step=pstep, page=page, past=past, n_new=n_new,
                          n_heads=n_heads, lam_init=lam_init),
        grid_spec=pltpu.PrefetchScalarGridSpec(
            num_scalar_prefetch=1,
            grid=(n_seq, n_pages // pstep),
            in_specs=[
                pl.BlockSpec(lamv.shape, lambda b, j, pt: (0, 0)),
                pl.BlockSpec((1, hd), lambda b, j, pt: (0, 0)),
                seq_spec(n_rows), seq_spec(n_new), seq_spec(n_new),
            ] + [page_spec(p) for p in range(pstep)] * 2,
            out_specs=seq_spec(n_new),
            scratch_shapes=[
                pltpu.VMEM((n_rows, 1), F32),
                pltpu.VMEM((n_rows, 1), F32),
                pltpu.VMEM((n_rows, aw), F32),
            ],
        ),
        out_shape=jax.ShapeDtypeStruct((n_seq, n_new, aw), BF16),
        compiler_params=_cparams(2),
        name="sattn",
    )(page_table, lamv, g_subln.reshape(1, hd), wq, k_new, v_new,
      *([cache_k2] * pstep), *([cache_v2] * pstep))


def _gelu_tanh(x):
    return x * (0.5 * (1.0 + jnp.tanh(math.sqrt(2.0 / math.pi) * (x + 0.044715 * (x * x * x)))))


def _softplus(x):
    return jnp.maximum(x, 0.0) + jnp.log1p(jnp.exp(-jnp.abs(x)))


def _rglru_kernel(u_ref, gt_ref, c0_ref, h0_ref, cw_ref, cb_ref, wai_ref, ba_ref, bi_ref, lam_ref, gn_ref,
                  rec_ref, ht_ref, cout_ref, ubuf, hcar, *, tl, n_blocks):
    t = pl.program_id(1)
    halo = CONV_WIDTH - 1
    base = 8

    @pl.when(t == 0)
    def _():
        ubuf[base - halo:base, :] = c0_ref[...]
        hcar[...] = h0_ref[...]

    ubuf[base:base + tl, :] = u_ref[...]
    cw = cw_ref[...]
    xc = cb_ref[...] + cw[0:1] * ubuf[base - halo:base - halo + tl, :]
    for jj in range(1, CONV_WIDTH):
        xc = xc + cw[jj:jj + 1] * ubuf[base - halo + jj:base - halo + jj + tl, :]
    tail = ubuf[base + tl - halo:base + tl, :]
    ubuf[base - halo:base, :] = tail
    cout_ref[...] = tail

    bw = xc.shape[1] // n_blocks
    za, zi = [], []
    for n in range(n_blocks):
        z = _dot(xc[:, n * bw:(n + 1) * bw].astype(BF16), wai_ref[n])
        za.append(z[:, :bw])
        zi.append(z[:, bw:])
    r = jax.nn.sigmoid(jnp.concatenate(za, axis=-1) + ba_ref[...])
    i = jax.nn.sigmoid(jnp.concatenate(zi, axis=-1) + bi_ref[...])
    log_a = -RG_C * r * _softplus(-lam_ref[...])
    a = jnp.exp(log_a)
    th = jnp.tanh(log_a)
    b = xc * i * jnp.sqrt(-2.0 * th / (1.0 - th))

    rowi = lax.broadcasted_iota(I32, a.shape, 0)
    sft = 1
    while sft < tl:
        keep = rowi >= sft
        a_prev = jnp.where(keep, pltpu.roll(a, sft, 0), 1.0)
        b_prev = jnp.where(keep, pltpu.roll(b, sft, 0), 0.0)
        b = a * b_prev + b
        a = a * a_prev
        sft *= 2
    hs = a * hcar[...] + b
    h_last = hs[tl - 1:tl, :]
    hcar[...] = h_last
    ht_ref[...] = h_last

    rec = hs * _gelu_tanh(gt_ref[...])
    rec_ref[...] = (_rms(rec, NORM_EPS) * gn_ref[...]).astype(BF16)


def _rglru(u2d, gate2d, conv0, h0, conv_w, conv_b, wai_bf, b_a, b_i, rg_lambda, g_rgnorm, n_batch, seq):
    r, w = u2d.shape
    tl = _pick(seq, (256, 128, 64, 32, 16, 8))
    nt = seq // tl
    n_blocks = wai_bf.shape[0]
    halo = CONV_WIDTH - 1
    row = lambda b, t: (b * nt + t, 0)
    vec = pl.BlockSpec((1, w), lambda b, t: (0, 0))
    rec, ht, cout = pl.pallas_call(
        functools.partial(_rglru_kernel, tl=tl, n_blocks=n_blocks),
        grid=(n_batch, nt),
        in_specs=[
            pl.BlockSpec((tl, w), row),
            pl.BlockSpec((tl, w), row),
            pl.BlockSpec((None, halo, w), lambda b, t: (b, 0, 0)),
            pl.BlockSpec((None, 1, w), lambda b, t: (b, 0, 0)),
            pl.BlockSpec((CONV_WIDTH, w), lambda b, t: (0, 0)),
            vec,
            pl.BlockSpec(wai_bf.shape, lambda b, t: (0, 0, 0)),
            vec, vec, vec, vec,
        ],
        out_specs=[
            pl.BlockSpec((tl, w), row),
            pl.BlockSpec((None, 1, w), lambda b, t: (b, 0, 0)),
            pl.BlockSpec((None, halo, w), lambda b, t: (b, 0, 0)),
        ],
        out_shape=[
            jax.ShapeDtypeStruct((r, w), BF16),
            jax.ShapeDtypeStruct((n_batch, 1, w), F32),
            jax.ShapeDtypeStruct((n_batch, halo, w), F32),
        ],
        scratch_shapes=[pltpu.VMEM((tl + 8, w), F32), pltpu.VMEM((1, w), F32)],
        compiler_params=_cparams(2),
        name="rglru",
    )(u2d, gate2d, conv0, h0.reshape(n_batch, 1, w), conv_w, conv_b.reshape(1, w), wai_bf,
      b_a.reshape(1, w), b_i.reshape(1, w), rg_lambda.reshape(1, w), g_rgnorm.reshape(1, w))
    return rec, ht.reshape(n_batch, w), cout


def _oproj_kernel(att_ref, rec_ref, x_ref, g1_ref, sh_ref, sc_ref, gn_ref, wo_ref, wrt_ref, *rest,
                  aliased, n_tiles):
    x1_ref, h2p_ref, st_ref = rest[2:] if aliased else rest
    aw = att_ref.shape[1]
    i = pl.program_id(0)

    @pl.when(i < n_tiles)
    def _():
        mix = _dot(att_ref[...], wo_ref[:aw, :]) + _dot(rec_ref[...], wo_ref[aw:, :])
        x1 = x_ref[...] + g1_ref[...] * mix
        x1_ref[...] = x1
        h2 = (_rms(x1, NORM_EPS) * gn_ref[...]) * (1.0 + sc_ref[...]) + sh_ref[...]
        _rows_to_tiles(h2p_ref, _pack_halves(h2))
        h_hi, h_lo = _split(h2)
        w_hi, w_lo = _split(wrt_ref[...])
        logits_t = _dot_nt(w_hi, h_hi) + (_dot_nt(w_hi, h_lo) + _dot_nt(w_lo, h_hi))
        st_ref[...] = jax.nn.sigmoid(logits_t)

    @pl.when(i >= n_tiles)
    def _():
        h2p_ref[...] = jnp.zeros(h2p_ref.shape, U32)
        st_ref[...] = jnp.zeros(st_ref.shape, F32)


def _oproj(att, rec, x2d, g1, shift, scale, g_norm2, wo_bf, wr_t, rows_per_batch, row_offset, total_rows,
           h2p_all=None, st_all=None):
    r, d = x2d.shape
    aw = att.shape[1]
    n_exp = wr_t.shape[0]
    tm = _pick(r, (256, 128))
    assert row_offset % tm == 0
    off = row_offset // tm
    lanes = d // 2 // TILE_SUBLANES
    assert lanes == 128
    aliased = h2p_all is not None
    n_tiles = r // tm
    n_fill = 0 if aliased else (total_rows - r) // tm
    assert aliased or (row_offset == 0 and (total_rows - r) % tm == 0)
    row = lambda i: (jnp.minimum(i, n_tiles - 1), 0)
    mspec = _mod_spec(rows_per_batch, tm, d, n_tiles)
    in_specs = [
        pl.BlockSpec((tm, aw), row),
        pl.BlockSpec((tm, d - aw), row),
        pl.BlockSpec((tm, d), row),
        mspec, mspec, mspec,
        pl.BlockSpec((1, d), lambda i: (0, 0)),
        pl.BlockSpec((d, d), lambda i: (0, 0)),
        pl.BlockSpec((n_exp, d), lambda i: (0, 0)),
    ]
    args = [att, rec, x2d, _mod_array(g1, rows_per_batch, tm), _mod_array(shift, rows_per_batch, tm),
            _mod_array(scale, rows_per_batch, tm), g_norm2.reshape(1, d), wo_bf, wr_t]
    io_alias = {}
    if aliased:
        in_specs += [pl.BlockSpec(memory_space=pl.ANY), pl.BlockSpec(memory_space=pl.ANY)]
        io_alias = {len(args): 1, len(args) + 1: 2}
        args += [h2p_all, st_all]
    return pl.pallas_call(
        functools.partial(_oproj_kernel, aliased=aliased, n_tiles=n_tiles),
        grid=(n_tiles + n_fill,),
        in_specs=in_specs,
        out_specs=[
            pl.BlockSpec((tm, d), row),
            pl.BlockSpec((tm * TILE_SUBLANES, lanes), lambda i: (i + off, 0)),
            pl.BlockSpec((n_exp, tm), lambda i: (0, i + off)),
        ],
        out_shape=[
            jax.ShapeDtypeStruct((r, d), F32),
            jax.ShapeDtypeStruct((total_rows * TILE_SUBLANES, lanes), U32),
            jax.ShapeDtypeStruct((n_exp, total_rows), F32),
        ],
        input_output_aliases=io_alias,
        compiler_params=_cparams(1),
        name="oproj",
    )(*args)


def _route_kernel(st_ref, rb_ref, idx_ref, wts_ref, rank_ref, cnt_ref, carry):
    i = pl.program_id(0)
    n_exp, tr = st_ref.shape
    gsz = n_exp // N_GROUPS

    @pl.when(i == 0)
    def _():
        carry[...] = jnp.zeros(carry.shape, F32)

    s = st_ref[...]
    biased = s + rb_ref[...]
    g = biased.reshape(N_GROUPS, gsz, tr)
    within = lax.broadcasted_iota(I32, g.shape, 1)
    m1 = jnp.max(g, axis=1, keepdims=True)
    first = jnp.min(jnp.where(g == m1, within, gsz), axis=1, keepdims=True)
    m2 = jnp.max(jnp.where(within == first, -jnp.inf, g), axis=1, keepdims=True)
    gscore = (m1 + m2).reshape(N_GROUPS, tr)

    gidx = lax.broadcasted_iota(I32, gscore.shape, 0)
    gsel = jnp.zeros(gscore.shape, F32)
    for _ in range(TOPK_GROUPS):
        mg = jnp.max(gscore, axis=0, keepdims=True)
        fg = jnp.min(jnp.where(gscore == mg, gidx, N_GROUPS), axis=0, keepdims=True)
        hit = gidx == fg
        gsel = jnp.where(hit, 1.0, gsel)
        gscore = jnp.where(hit, -jnp.inf, gscore)
    masked = jnp.where(gsel.reshape(N_GROUPS, 1, tr) > 0.5, g, -jnp.inf).reshape(n_exp, tr)

    eidx = lax.broadcasted_iota(I32, (n_exp, tr), 0)
    idxs, ws = [], []
    chosen = jnp.zeros((n_exp, tr), jnp.bool_)
    for _ in range(TOP_K):
        mv = jnp.max(masked, axis=0, keepdims=True)
        fe = jnp.min(jnp.where(masked == mv, eidx, n_exp), axis=0, keepdims=True)
        hit = eidx == fe
        idxs.append(fe)
        ws.append(jnp.sum(jnp.where(hit, s, 0.0), axis=0, keepdims=True))
        chosen = chosen | hit
        masked = jnp.where(hit, -jnp.inf, masked)
    idx = jnp.concatenate(idxs, axis=0)
    w = jnp.concatenate(ws, axis=0)
    idx_ref[...] = idx
    wts_ref[...] = w / jnp.sum(w, axis=0, keepdims=True) * ROUTED_SCALE

    cmat = jnp.where(chosen, 1.0, 0.0)
    before = lax.broadcasted_iota(I32, (tr, tr), 0) < lax.broadcasted_iota(I32, (tr, tr), 1)
    prior = _dot(cmat.astype(BF16), jnp.where(before, 1.0, 0.0).astype(BF16)) + carry[...]
    ranks = [jnp.sum(jnp.where(eidx == idxs[k], prior, 0.0), axis=0, keepdims=True) for k in range(TOP_K)]
    rank_ref[...] = jnp.concatenate(ranks, axis=0).astype(I32)
    carry[...] = carry[...] + jnp.sum(cmat, axis=1, keepdims=True)
    cnt_ref[...] = carry[...].astype(I32)


def _route(st_all, router_bias):
    n_exp, t_all = st_all.shape
    tr = _pick(t_all, (640, 512, 256, 128))
    col = lambda i: (0, i)
    o8 = lambda dt: jax.ShapeDtypeStruct((TOP_K, t_all), dt)
    return pl.pallas_call(
        _route_kernel,
        grid=(t_all // tr,),
        in_specs=[pl.BlockSpec((n_exp, tr), col), pl.BlockSpec((n_exp, 1), lambda i: (0, 0))],
        out_specs=[pl.BlockSpec((TOP_K, tr), col)] * 3 + [pl.BlockSpec((n_exp, 1), lambda i: (0, 0))],
        out_shape=[o8(I32), o8(F32), o8(I32), jax.ShapeDtypeStruct((n_exp, 1), I32)],
        scratch_shapes=[pltpu.VMEM((n_exp, 1), F32)],
        compiler_params=_cparams(1),
        name="route",
    )(st_all, router_bias.reshape(n_exp, 1))


def _pos_kernel(idx_ref, rank_ref, start_ref, pos_ref):
    n_exp = start_ref.shape[0]
    tr = idx_ref.shape[1]
    eidx = lax.broadcasted_iota(I32, (n_exp, tr), 0)
    start = start_ref[...]
    rows = [jnp.sum(jnp.where(eidx == idx_ref[k:k + 1, :], start, 0.0), axis=0, keepdims=True)
            for k in range(TOP_K)]
    pos_ref[...] = jnp.concatenate(rows, axis=0).astype(I32) + rank_ref[...]


def _pos(idx, rank, start_rows):
    n_exp = start_rows.shape[0]
    t_all = idx.shape[1]
    tr = _pick(t_all, (640, 512, 256, 128))
    col = pl.BlockSpec((TOP_K, tr), lambda i: (0, i))
    return pl.pallas_call(
        _pos_kernel,
        grid=(t_all // tr,),
        in_specs=[col, col, pl.BlockSpec((n_exp, 1), lambda i: (0, 0))],
        out_specs=col,
        out_shape=jax.ShapeDtypeStruct((TOP_K, t_all), I32),
        compiler_params=_cparams(1),
        name="pos",
    )(idx, rank, start_rows.astype(F32).reshape(n_exp, 1))


def _tile_pos(pos, tile):
    k, t = pos.shape
    return pos.reshape(k, t // tile, tile).transpose(1, 0, 2)


def _sc_worker_chunks(n_chunks, fn):
    n_workers = V7X_SC_CORES * V7X_SC_SUBCORES
    worker = lax.axis_index("core") * V7X_SC_SUBCORES + lax.axis_index("subcore")

    @pl.loop(0, -(-n_chunks // n_workers))
    def _(it):
        chunk = it * n_workers + worker

        @pl.when(chunk < n_chunks)
        def _():
            fn(chunk)


def _sc_mesh():
    return plsc.VectorSubcoreMesh(core_axis_name="core", subcore_axis_name="subcore",
                                  num_cores=V7X_SC_CORES, num_subcores=V7X_SC_SUBCORES)


def _sc_scatter_rows(x3, idx3, n_rows):
    n_chunks, n_k, width = idx3.shape
    row = x3.shape[1:]

    @pl.kernel(out_type=jax.ShapeDtypeStruct((n_rows,) + row, x3.dtype), mesh=_sc_mesh(),
               scratch_types=[pltpu.VMEM((width,) + row, x3.dtype), pltpu.VMEM((n_k, width), I32),
                              pltpu.SemaphoreType.DMA],
               name="dispatch_sc")
    def scatter(x_hbm, i_hbm, o_hbm, xbuf, ibuf, sem):
        def one(chunk):
            pltpu.sync_copy(x_hbm.at[pl.ds(chunk * width, width)], xbuf)
            pltpu.sync_copy(i_hbm.at[chunk], ibuf)
            copies = [pltpu.make_async_copy(xbuf, o_hbm.at[ibuf.at[k]], sem) for k in range(n_k)]
            for cp in copies:
                cp.start()
            for cp in copies:
                cp.wait()

        _sc_worker_chunks(n_chunks, one)

    return scatter(x3, idx3)


def _sc_gather_rows(src3, idx2):
    n_chunks, width = idx2.shape
    row = src3.shape[1:]

    @pl.kernel(out_type=jax.ShapeDtypeStruct((n_chunks * width,) + row, src3.dtype), mesh=_sc_mesh(),
               scratch_types=[pltpu.VMEM((width,) + row, src3.dtype), pltpu.VMEM((width,), I32),
                              pltpu.SemaphoreType.DMA],
               name="combine_sc")
    def gather(s_hbm, i_hbm, o_hbm, buf, ibuf, sem):
        def one(chunk):
            pltpu.sync_copy(i_hbm.at[chunk], ibuf)
            pltpu.async_copy(s_hbm.at[ibuf], buf, sem).wait()
            pltpu.sync_copy(buf, o_hbm.at[pl.ds(chunk * width, width)])

        _sc_worker_chunks(n_chunks, one)

    return gather(src3, idx2)


def _swiglu_packed(xp, wg, wu, wd):
    xa, xb = _unpack_halves(xp)
    half = xp.shape[1]
    g = _dot(xa, wg[:half, :]) + _dot(xb, wg[half:, :])
    u = _dot(xa, wu[:half, :]) + _dot(xb, wu[half:, :])
    return _dot((_silu(g) * u).astype(BF16), wd[...])


def _experts_kernel(te_ref, tnv_ref, tord_ref, tnext_ref, tnext2_ref, tlo_ref, thi_ref,
                    xs_ref, wg_hbm, wu_hbm, wd_hbm, ys_ref, wg_f, wu_f, wd_f, wg_b, wu_b, wd_b, sems):
    i = pl.program_id(0)
    nv = tnv_ref[i]
    expert = te_ref[i]
    new_expert = jnp.logical_or(i == 0, expert != te_ref[jnp.maximum(i - 1, 0)])
    slot = tord_ref[i] % EXPERT_WEIGHT_SLOTS
    tm = ys_ref.shape[0] // TILE_SUBLANES

    def weight_copies(e, sl):
        copies = []
        for n, (hbm, buf) in enumerate(((wg_hbm, wg_f), (wu_hbm, wu_f), (wd_hbm, wd_f))):
            rows = hbm.shape[1] // EXPERT_WEIGHT_DMA_CHUNKS
            for c in range(EXPERT_WEIGHT_DMA_CHUNKS):
                span = pl.ds(c * rows, rows)
                copies.append(pltpu.make_async_copy(hbm.at[e, span], buf.at[sl, span], sems.at[n, sl]))
        return copies

    @pl.when(jnp.logical_and(nv > 0, new_expert))
    def _():
        @pl.when(i == 0)
        def _():
            for cp in weight_copies(expert, slot):
                cp.start()
            nxt = tnext_ref[i]

            @pl.when(nxt >= 0)
            def _():
                for cp in weight_copies(nxt, (slot + 1) % EXPERT_WEIGHT_SLOTS):
                    cp.start(priority=WEIGHT_PREFETCH_DMA_PRIORITY)

        for cp in weight_copies(expert, slot):
            cp.wait()

    nxt2 = tnext2_ref[i]

    @pl.when(jnp.logical_and(nv > 0, nxt2 >= 0))
    def _():
        lo, hi = tlo_ref[i], thi_ref[i]
        for c, cp in enumerate(weight_copies(nxt2, (slot + 2) % EXPERT_WEIGHT_SLOTS)):
            @pl.when(jnp.logical_and(lo <= c, c < hi))
            def _():
                cp.start(priority=WEIGHT_PREFETCH_DMA_PRIORITY)

    @pl.when(jnp.logical_and(nv > 0, new_expert))
    def _():
        wg_b[...] = wg_f[slot].astype(BF16)
        wu_b[...] = wu_f[slot].astype(BF16)
        wd_b[...] = wd_f[slot].astype(BF16)

    @pl.when(nv > 0)
    def _():
        xp = _rows_from_tiles(xs_ref, tm)
        rowi = lax.broadcasted_iota(I32, xp.shape, 0)
        xp = jnp.where(rowi < nv, xp, jnp.uint32(0))
        _rows_to_tiles(ys_ref, _pack_halves(_swiglu_packed(xp, wg_b, wu_b, wd_b)))

    @pl.when(nv == 0)
    def _():
        ys_ref[...] = jnp.zeros(ys_ref.shape, U32)


def _experts(xs, tile_tables, w_gate, w_up, w_down):
    rows, lanes = xs.shape
    tm = EXPERT_TILE_ROWS
    n_exp, d, ff = w_gate.shape
    blk = pl.BlockSpec((tm * TILE_SUBLANES, lanes), lambda i, *_: (i, 0))
    hbm = pl.BlockSpec(memory_space=pl.ANY)
    return pl.pallas_call(
        _experts_kernel,
        grid_spec=pltpu.PrefetchScalarGridSpec(
            num_scalar_prefetch=len(tile_tables),
            grid=(rows // (tm * TILE_SUBLANES),),
            in_specs=[blk, hbm, hbm, hbm],
            out_specs=blk,
            scratch_shapes=[
                pltpu.VMEM((EXPERT_WEIGHT_SLOTS, d, ff), F32), pltpu.VMEM((EXPERT_WEIGHT_SLOTS, d, ff), F32),
                pltpu.VMEM((EXPERT_WEIGHT_SLOTS, ff, d), F32),
                pltpu.VMEM((d, ff), BF16), pltpu.VMEM((d, ff), BF16), pltpu.VMEM((ff, d), BF16),
                pltpu.SemaphoreType.DMA((3, EXPERT_WEIGHT_SLOTS)),
            ],
        ),
        out_shape=jax.ShapeDtypeStruct((rows, lanes), U32),
        compiler_params=_cparams(1),
        name="experts",
    )(*tile_tables, xs, w_gate, w_up, w_down)


def _combine_kernel(w_ref, yg_ref, h2p_ref, x1_ref, g2_ref, gf_ref, wsg_ref, wsu_ref, wsd_ref, o_ref, *, tc):
    shared = _swiglu_packed(_rows_from_tiles(h2p_ref, tc), wsg_ref, wsu_ref, wsd_ref)
    w = w_ref[...]
    lo, hi = None, None
    for k in range(TOP_K):
        rows = tc * TILE_SUBLANES
        yk = _rows_from_tiles(yg_ref.at[pl.ds(k * rows, rows)], tc)
        wk = w[:, k:k + 1]
        yl = pltpu.unpack_elementwise(yk, index=0, packed_dtype=BF16, unpacked_dtype=F32) * wk
        yh = pltpu.unpack_elementwise(yk, index=1, packed_dtype=BF16, unpacked_dtype=F32) * wk
        lo = yl if lo is None else lo + yl
        hi = yh if hi is None else hi + yh
    routed = jnp.concatenate([lo, hi], axis=-1)
    x2 = x1_ref[...] + g2_ref[...] * (routed + shared)
    o_ref[...] = _rms(x2, NORM_EPS) * gf_ref[...]


def _combine(yg, wts_t, h2p_all, x1, g2, g_final, wsg_bf, wsu_bf, wsd_bf, rows_per_batch, row_offset):
    r, d = x1.shape
    lanes = h2p_all.shape[1]
    tc = COMBINE_TILE
    assert r % tc == 0 and row_offset % tc == 0
    off = row_offset // tc
    ff = wsg_bf.shape[1]
    const = lambda shp: pl.BlockSpec(shp, lambda i: (0,) * len(shp))
    return pl.pallas_call(
        functools.partial(_combine_kernel, tc=tc),
        grid=(r // tc,),
        in_specs=[
            pl.BlockSpec((tc, TOP_K), lambda i: (i, 0)),
            pl.BlockSpec((TOP_K * tc * TILE_SUBLANES, lanes), lambda i: (i + off, 0)),
            pl.BlockSpec((tc * TILE_SUBLANES, lanes), lambda i: (i + off, 0)),
            pl.BlockSpec((tc, d), lambda i: (i, 0)),
            _mod_spec(rows_per_batch, tc, d),
            const((1, d)), const((d, ff)), const((d, ff)), const((ff, d)),
        ],
        out_specs=pl.BlockSpec((tc, d), lambda i: (i, 0)),
        out_shape=jax.ShapeDtypeStruct((r, d), F32),
        compiler_params=_cparams(1),
        name="combine",
    )(wts_t, yg, h2p_all, x1, _mod_array(g2, rows_per_batch, tc), g_final.reshape(1, d),
      wsg_bf, wsu_bf, wsd_bf)


def kernel(x_prompt, x_sample, cache_k, cache_v, state_h, state_conv, page_table, c_prompt, c_sample,
           w_ada, b_ada, g_norm1, w_in, lambda_q1, lambda_k1, lambda_q2, lambda_k2, g_subln,
           conv_w, conv_b, w_rg_a, b_rg_a, w_rg_i, b_rg_i, rg_lambda, g_rgnorm, w_o, g_norm2,
           w_router, router_bias, w_e_gate, w_e_up, w_e_down, w_s_gate, w_s_up, w_s_down, g_final):
    depth = w_ada.shape[0]
    assert depth == 1, "single-layer step"
    bp, seq, d = x_prompt.shape
    bs, n_new, _ = x_sample.shape
    n_heads = cache_k.shape[3]
    k_row = cache_k.shape[4]
    v_head = cache_v.shape[4]
    aw = n_heads * v_head
    rw = d - aw
    assert k_row == v_head and w_in.shape[2] == 3 * aw + 2 * rw and aw == rw
    qk_half = k_row // 2
    n_exp = w_router.shape[2]
    lam_init = 0.8 - 0.6 * math.exp(-0.3 * 0)
    tp, ts = bp * seq, bs * n_new
    t_all = tp + ts

    w_in_bf = w_in[0].astype(BF16)
    wo_bf = w_o[0].astype(BF16)
    wr_t = w_router[0].T
    wai_bf = jnp.concatenate([w_rg_a[0], w_rg_i[0]], axis=-1).astype(BF16)
    wsg_bf, wsu_bf, wsd_bf = w_s_gate[0].astype(BF16), w_s_up[0].astype(BF16), w_s_down[0].astype(BF16)
    lamv = jnp.stack([lambda_q1[0], lambda_k1[0], lambda_q2[0], lambda_k2[0]])
    slopes = jnp.exp2(-8.0 * jnp.arange(1, n_heads + 1, dtype=F32) / n_heads)

    mod = _ada(jnp.concatenate([c_prompt, c_sample], axis=0), w_ada[0], b_ada[0])
    mod_p = [mod[:bp, i * d:(i + 1) * d] for i in range(6)]
    mod_s = [mod[bp:, i * d:(i + 1) * d] for i in range(6)]

    xp2, xs2 = x_prompt.reshape(tp, d), x_sample.reshape(ts, d)
    qscale = qk_half ** -0.5 * LOG2E
    qp, kp, vp, up, gp, kpb, vpb = _inproj(xp2, mod_p[0], mod_p[1], g_norm1[0], w_in_bf, seq, qscale)
    qs, ks, vs, us, gs, ksb, vsb = _inproj(xs2, mod_s[0], mod_s[1], g_norm1[0], w_in_bf, n_new, qscale)

    att_p = _pattn(qp, kpb, vpb, slopes, lamv, g_subln[0], bp, seq, n_heads, lam_init)

    n_grp = 2 * n_heads
    own = (jnp.arange(aw)[None, :] // qk_half) == (jnp.arange(n_grp * n_new)[:, None] // n_new)
    wq = jnp.where(own[None], jnp.tile(qs.reshape(bs, n_new, aw), (1, n_grp, 1)), jnp.zeros((), BF16))
    n_pool, page = cache_k.shape[1], cache_k.shape[2]
    att_s = _sattn(wq, ksb.reshape(bs, n_new, aw), vsb.reshape(bs, n_new, aw),
                   cache_k.reshape(n_pool, page * n_heads, k_row), cache_v.reshape(n_pool, page * n_heads, v_head),
                   page_table, lamv, g_subln[0], n_heads, lam_init).reshape(ts, aw)

    rg_args = (conv_w[0], conv_b[0], wai_bf, b_rg_a[0], b_rg_i[0], rg_lambda[0], g_rgnorm[0])
    rec_p, h_p, conv_p = _rglru(up, gp, jnp.zeros((bp, CONV_WIDTH - 1, rw), F32), jnp.zeros((bp, rw), F32),
                                *rg_args, bp, seq)
    rec_s, h_s, conv_s = _rglru(us, gs, state_conv[0], state_h[0], *rg_args, bs, n_new)

    x1p, h2p_all, st_all = _oproj(att_p, rec_p, xp2, mod_p[2], mod_p[3], mod_p[4], g_norm2[0], wo_bf, wr_t,
                                  seq, 0, t_all)
    x1s, h2p_all, st_all = _oproj(att_s, rec_s, xs2, mod_s[2], mod_s[3], mod_s[4], g_norm2[0], wo_bf, wr_t,
                                  n_new, tp, t_all, h2p_all, st_all)

    idx, wts, rank, counts = _route(st_all, router_bias[0])

    tm = EXPERT_TILE_ROWS
    counts = counts.reshape(n_exp)
    ptiles = (counts + tm - 1) // tm
    pend = jnp.cumsum(ptiles)
    pstart = pend - ptiles
    n_tiles = (t_all * TOP_K) // tm + n_exp
    tile_ids = jnp.arange(n_tiles, dtype=I32)
    tile_e = jnp.minimum(jnp.sum(pend[None, :] <= tile_ids[:, None], axis=1), n_exp - 1).astype(I32)
    tile_nv = jnp.where(tile_ids < pend[-1],
                        jnp.clip(counts[tile_e] - (tile_ids - pstart[tile_e]) * tm, 0, tm), 0).astype(I32)
    tile_ord = (jnp.cumsum(ptiles > 0) - 1)[tile_e].astype(I32)
    next_tile = pend[tile_e]
    e_next = tile_e[jnp.minimum(next_tile, n_tiles - 1)]
    tile_next = jnp.where(next_tile < pend[-1], e_next, -1).astype(I32)
    next_tile2 = pend[e_next]
    tile_next2 = jnp.where(jnp.logical_and(next_tile < pend[-1], next_tile2 < pend[-1]),
                           tile_e[jnp.minimum(next_tile2, n_tiles - 1)], -1).astype(I32)
    n_copies = 3 * EXPERT_WEIGHT_DMA_CHUNKS
    tile_j, tile_n = tile_ids - pstart[tile_e], jnp.maximum(ptiles[tile_e] - 1, 1)
    tile_lo = ((n_copies * tile_j + tile_n - 1) // tile_n).astype(I32)
    tile_hi = ((n_copies * (tile_j + 1) + tile_n - 1) // tile_n).astype(I32)
    pos = _pos(idx, rank, pstart * tm)

    lanes = h2p_all.shape[1]
    tile3 = lambda a2: a2.reshape(-1, TILE_SUBLANES, lanes)
    xs_rows = _sc_scatter_rows(tile3(h2p_all), _tile_pos(pos, SC_CHUNK_ROWS), n_tiles * tm)
    ys = _experts(xs_rows.reshape(-1, lanes), (tile_e, tile_nv, tile_ord, tile_next, tile_next2, tile_lo, tile_hi),
                  w_e_gate[0], w_e_up[0], w_e_down[0])
    yg = _sc_gather_rows(tile3(ys), _tile_pos(pos, COMBINE_TILE).reshape(-1, SC_CHUNK_ROWS)).reshape(-1, lanes)

    wts_t = wts.T
    y_p = _combine(yg, wts_t[:tp], h2p_all, x1p, mod_p[5], g_final, wsg_bf, wsu_bf, wsd_bf, seq, 0)
    y_s = _combine(yg, wts_t[tp:], h2p_all, x1s, mod_s[5], g_final, wsg_bf, wsu_bf, wsd_bf, n_new, tp)

    return (y_p.reshape(bp, seq, d), y_s.reshape(bs, n_new, d),
            kp.reshape(1, bp, seq, n_heads, k_row), vp.reshape(1, bp, seq, n_heads, v_head),
            h_p.reshape(1, bp, rw), conv_p.reshape(1, bp, CONV_WIDTH - 1, rw),
            ks.reshape(1, bs, n_new, n_heads, k_row), vs.reshape(1, bs, n_new, n_heads, v_head),
            h_s.reshape(1, bs, rw), conv_s.reshape(1, bs, CONV_WIDTH - 1, rw))
```
